```python
import math
import jax
import jax.numpy as jnp
from jax import lax
import numpy as np

D_MODEL = 2048
BATCH = 1
SEQ = 8192
DEPTH = 2
DEC_BATCH = 128
DEC_SEQ = 1
PAST_LEN = 2048
PAGE_SIZE = 128

MIX_WIDTH = D_MODEL
RET_HEADS = 4
RET_WIDTH = MIX_WIDTH // 4
RET_DK = RET_WIDTH // RET_HEADS
RET_DV = RET_WIDTH // RET_HEADS
RET_CHUNK = 128
SSD_WIDTH = MIX_WIDTH // 2
SSD_HEADDIM = 64
SSD_HEADS = SSD_WIDTH // SSD_HEADDIM
SSD_GROUPS = 2
SSD_STATE = 128
SSD_CONV = 4
SSD_CONV_DIM = SSD_WIDTH + 2 * SSD_GROUPS * SSD_STATE
SSD_CHUNK = 128
NSA_WIDTH = MIX_WIDTH // 4
NSA_HEAD_DIM = 64
NSA_HEADS = NSA_WIDTH // NSA_HEAD_DIM
NSA_KV_HEADS = 2
NSA_KV_WIDTH = NSA_KV_HEADS * NSA_HEAD_DIM
CMP_BLOCK = 32
CMP_STRIDE = 16
SLC_BLOCK = 64
SLC_TOPN = 16
N_LOCAL = 2
WINDOW = 512
Q_BLOCK = 128
FORCE_SCORE = 1.0e6
MEM_LEN = 256
MEM_HEADS = 4
MEM_HEAD_DIM = 128
MEM_WIDTH = MEM_HEADS * MEM_HEAD_DIM

ROPE_THETA = 10000.0
EPS = 1e-6

IN_SIZES = (RET_WIDTH, RET_WIDTH, RET_WIDTH, RET_WIDTH,
            SSD_WIDTH, SSD_CONV_DIM, SSD_HEADS,
            NSA_WIDTH, 6 * NSA_KV_WIDTH, 3 * NSA_HEADS, NSA_WIDTH)
IN_COLS = sum(IN_SIZES)

kernel_name = "hybrid_retention_ssd_nsa_memory_decode_step"

f32 = jnp.float32


def _rms(t):
    tf = t.astype(f32)
    return tf * lax.rsqrt(jnp.mean(tf * tf, axis=-1, keepdims=True) + EPS)


def rms_norm(t, w):
    return (_rms(t) * w.astype(f32)).astype(t.dtype)


def rope(t, pos):
    half = t.shape[-1] // 2
    inv = jnp.exp(-math.log(ROPE_THETA) * jnp.arange(half, dtype=f32) / half)
    ang = pos.astype(f32)[:, None] * inv[None, :]
    cos = jnp.cos(ang)[:, None, :]
    sin = jnp.sin(ang)[:, None, :]
    t1 = t[..., :half].astype(f32)
    t2 = t[..., half:].astype(f32)
    return jnp.concatenate([t1 * cos - t2 * sin, t2 * cos + t1 * sin], axis=-1).astype(t.dtype)


def _split_cols(t):
    out, s = [], 0
    for n in IN_SIZES:
        out.append(t[..., s:s + n])
        s += n
    return out


def masked_softmax(s, mask):
    s = jnp.where(mask, s, -jnp.inf)
    m = jnp.max(s, axis=-1, keepdims=True)
    m = jnp.where(jnp.isfinite(m), m, 0.0)
    p = jnp.exp(s - m)
    z = jnp.sum(p, axis=-1, keepdims=True)
    return p / jnp.where(z > 0, z, 1.0)


def retention_log_decay():
    return jnp.log1p(-jnp.exp2(-5.0 - jnp.arange(RET_HEADS, dtype=f32)))


def retention(q, k, v, s0, log_gamma):
    B, L, H, _ = q.shape
    C = math.gcd(L, RET_CHUNK)
    n = L // C
    idx = jnp.arange(C, dtype=f32)
    diff = idx[:, None] - idx[None, :]
    decay = jnp.where(diff[None] >= 0, jnp.exp(log_gamma[:, None, None] * jnp.maximum(diff, 0.0)[None]), 0.0)
    q_dec = jnp.exp(log_gamma[:, None] * (idx + 1.0)[None]).T[None, :, :, None]
    k_dec = jnp.exp(log_gamma[:, None] * (C - 1.0 - idx)[None]).T[None, :, :, None]
    chunk_dec = jnp.exp(log_gamma * C)[None, :, None, None]

    def to_chunks(t):
        return t.astype(f32).reshape(B, n, C, H, t.shape[-1]).transpose(1, 0, 2, 3, 4)

    def step(S, inp):
        qc, kc, vc = inp
        att = jnp.einsum('bihd,bjhd->bhij', qc, kc) * decay
        o = jnp.einsum('bhij,bjhe->bihe', att, vc) + jnp.einsum('bihd,bhde->bihe', qc, S) * q_dec
        S = S * chunk_dec + jnp.einsum('bjhd,bjhe->bhde', kc * k_dec, vc)
        return S, o

    S, o = lax.scan(step, s0.astype(f32), (to_chunks(q), to_chunks(k), to_chunks(v)))
    return o.transpose(1, 0, 2, 3, 4).reshape(B, L, H, -1), S


def causal_dwconv(u, buf, w, b):
    full = jnp.concatenate([buf.astype(u.dtype), u], axis=1)
    L = u.shape[1]
    out = b
    for j in range(SSD_CONV):
        out = out + full[:, j:j + L] * w[j]
    return out, full[:, full.shape[1] - (SSD_CONV - 1):]


def ssd_scan(x, dt, A, Bm, Cm, h0):
    Bsz, L, H, P = x.shape
    C = math.gcd(L, SSD_CHUNK)
    n = L // C
    rep = H // SSD_GROUPS
    Bh = jnp.repeat(Bm.astype(f32), rep, axis=2)
    Ch = jnp.repeat(Cm.astype(f32), rep, axis=2)
    a = dt * A
    causal = jnp.tril(jnp.ones((C, C), bool))[None, :, :, None]

    def chunks(t):
        return jnp.moveaxis(t.astype(f32).reshape((Bsz, n, C) + t.shape[2:]), 1, 0)

    def step(h, inp):
        xc, dtc, ac, bc, cc = inp
        cum = jnp.cumsum(ac, axis=1)
        seg = cum[:, :, None, :] - cum[:, None, :, :]
        Lm = jnp.exp(jnp.where(causal, seg, -jnp.inf))
        sc = jnp.einsum('bihn,bjhn->bijh', cc, bc) * Lm * dtc[:, None, :, :]
        y = jnp.einsum('bijh,bjhp->bihp', sc, xc)
        y = y + jnp.einsum('bihn,bhpn->bihp', cc, h) * jnp.exp(cum)[..., None]
        wgt = jnp.exp(cum[:, -1:, :] - cum) * dtc
        h = h * jnp.exp(cum[:, -1, :])[:, :, None, None] + jnp.einsum('bjh,bjhp,bjhn->bhpn', wgt, xc, bc)
        return h, y

    h, y = lax.scan(step, h0.astype(f32), (chunks(x), chunks(dt), chunks(a), chunks(Bh), chunks(Ch)))
    return jnp.moveaxis(y, 0, 1).reshape(Bsz, L, H, P), h


def compress_kv(kc_raw, vc_raw, lp):
    B, T, G, D = kc_raw.shape
    n_cmp = (T - CMP_BLOCK) // CMP_STRIDE + 1
    start = jnp.arange(n_cmp, dtype=jnp.int32) * CMP_STRIDE
    idx = start[:, None] + jnp.arange(CMP_BLOCK, dtype=jnp.int32)[None, :]

    def mlp(raw, j):
        blk = raw[:, idx] + lp['nsa_cmp_pe'][j][None, None, :, None, :]
        blk = jnp.swapaxes(blk, 2, 3).reshape(B, n_cmp, G, CMP_BLOCK * D)
        return jax.nn.silu(blk @ lp['nsa_cmp_w1'][j]) @ lp['nsa_cmp_w2'][j]

    kc = rms_norm(mlp(kc_raw, 0), lp['nsa_kc_norm'])
    vc = mlp(vc_raw, 1)
    return kc, vc, start + (CMP_BLOCK - 1)


def to_slc_blocks(t):
    B, T, G, D = t.shape
    ns = -(-T // SLC_BLOCK)
    t = jnp.pad(t, ((0, 0), (0, ns * SLC_BLOCK - T), (0, 0), (0, 0)))
    return t.reshape(B, ns, SLC_BLOCK, G, D).transpose(0, 3, 1, 2, 4)


def nsa_attend(q, qpos, kc, vc, kc_end, ks_blk, vs_blk, kw, vw, kw_pos, gates):
    B, Q, H, D = q.shape
    G = kc.shape[2]
    hpg = H // G
    ns = ks_blk.shape[2]
    qg = q.astype(f32).reshape(B, Q, G, hpg, D) * (D ** -0.5)
    mask_c = kc_end[None, :] <= qpos[:, None]
    p_c = masked_softmax(jnp.einsum('bqghd,bngd->bqghn', qg, kc.astype(f32)), mask_c[None, :, None, None, :])
    o_c = jnp.einsum('bqghn,bngd->bqghd', p_c, vc.astype(f32))
    blk = jnp.arange(ns, dtype=jnp.int32)
    c_start = kc_end - (CMP_BLOCK - 1)
    overlap = ((c_start[:, None] < (blk[None, :] + 1) * SLC_BLOCK) & (kc_end[:, None] >= blk[None, :] * SLC_BLOCK)).astype(f32)
    imp = jnp.einsum('bqghn,nj->bqgj', p_c, overlap)
    cur = qpos // SLC_BLOCK
    forced = (blk[None, :] == 0) | ((blk[None, :] <= cur[:, None]) & (blk[None, :] > cur[:, None] - N_LOCAL))
    valid = blk[None, :] <= cur[:, None]
    score = jnp.where(forced[None, :, None, :], FORCE_SCORE, imp)
    score = jnp.where(valid[None, :, None, :], score, -FORCE_SCORE)
    n_sel = min(SLC_TOPN, ns)
    top_val, top_idx = lax.top_k(score, n_sel)
    bi = jnp.arange(B)[:, None, None, None]
    gi = jnp.arange(G)[None, None, :, None]
    kg = ks_blk[bi, gi, top_idx].astype(f32)
    vg = vs_blk[bi, gi, top_idx].astype(f32)
    kpos = top_idx[..., None] * SLC_BLOCK + jnp.arange(SLC_BLOCK, dtype=jnp.int32)
    mask_s = (top_val > -0.5 * FORCE_SCORE)[..., None] & (kpos <= qpos[None, :, None, None, None])
    s_s = jnp.einsum('bqghd,bqgnkd->bqghnk', qg, kg).reshape(B, Q, G, hpg, n_sel * SLC_BLOCK)
    p_s = masked_softmax(s_s, mask_s.reshape(B, Q, G, 1, n_sel * SLC_BLOCK)).reshape(B, Q, G, hpg, n_sel, SLC_BLOCK)
    o_s = jnp.einsum('bqghnk,bqgnkd->bqghd', p_s, vg)
    rel = qpos[:, None] - kw_pos[None, :]
    mask_w = (rel >= 0) & (rel <= WINDOW) & (kw_pos[None, :] >= 0)
    p_w = masked_softmax(jnp.einsum('bqghd,bkgd->bqghk', qg, kw.astype(f32)), mask_w[None, :, None, None, :])
    o_w = jnp.einsum('bqghk,bkgd->bqghd', p_w, vw.astype(f32))
    g = gates.astype(f32).reshape(B, Q, G, hpg, 3)
    o = g[..., 0:1] * o_c + g[..., 1:2] * o_s + g[..., 2:3] * o_w
    return o.reshape(B, Q, H * D)


def nsa_prompt(q, kc_raw, vc_raw, ks, vs, kw, vw, gates, lp):
    B, L = q.shape[:2]
    kc, vc, kc_end = compress_kv(kc_raw, vc_raw, lp)
    ks_blk = to_slc_blocks(ks)
    vs_blk = to_slc_blocks(vs)
    kw_pad = jnp.pad(kw, ((0, 0), (WINDOW, 0), (0, 0), (0, 0)))
    vw_pad = jnp.pad(vw, ((0, 0), (WINDOW, 0), (0, 0), (0, 0)))
    qb = math.gcd(L, Q_BLOCK)
    nb = L // qb

    def block(i):
        s = i * qb
        q_i = lax.dynamic_slice_in_dim(q, s, qb, axis=1)
        g_i = lax.dynamic_slice_in_dim(gates, s, qb, axis=1)
        kw_i = lax.dynamic_slice_in_dim(kw_pad, s, WINDOW + qb, axis=1)
        vw_i = lax.dynamic_slice_in_dim(vw_pad, s, WINDOW + qb, axis=1)
        qpos = s + jnp.arange(qb, dtype=jnp.int32)
        kpos = s - WINDOW + jnp.arange(WINDOW + qb, dtype=jnp.int32)
        return nsa_attend(q_i, qpos, kc, vc, kc_end, ks_blk, vs_blk, kw_i, vw_i, kpos, g_i)

    o = lax.map(block, jnp.arange(nb, dtype=jnp.int32))
    return jnp.moveaxis(o, 0, 1).reshape(B, L, -1)


def mixer_front(x, pos, lp, ret_s0, conv_buf, ssm_h0):
    B, L, _ = x.shape
    h = rms_norm(x, lp['norm_mix'])
    rq, rk, rv, rg, sz, sxbc, sdt, nq, nkv, ngate, nz = _split_cols(h @ lp['w_in'])
    q = rope(rq.reshape(B, L, RET_HEADS, RET_DK), pos)
    k = rope(rk.reshape(B, L, RET_HEADS, RET_DK), pos) * (RET_DK ** -0.5)
    v = rv.reshape(B, L, RET_HEADS, RET_DV)
    ro, ret_s = retention(q, k, v, ret_s0, retention_log_decay())
    ret_out = _rms(ro).reshape(B, L, RET_WIDTH) * jax.nn.silu(rg.astype(f32))
    xbc, conv_new = causal_dwconv(sxbc, conv_buf, lp['ssd_conv_w'], lp['ssd_conv_b'])
    xbc = jax.nn.silu(xbc)
    gn = SSD_GROUPS * SSD_STATE
    xs = xbc[..., :SSD_WIDTH].reshape(B, L, SSD_HEADS, SSD_HEADDIM)
    bm = xbc[..., SSD_WIDTH:SSD_WIDTH + gn].reshape(B, L, SSD_GROUPS, SSD_STATE)
    cm = xbc[..., SSD_WIDTH + gn:].reshape(B, L, SSD_GROUPS, SSD_STATE)
    dt = jax.nn.softplus((sdt + lp['ssd_dt_bias']).astype(f32))
    A = -jnp.exp(lp['ssd_a_log'].astype(f32))
    y, ssm_h = ssd_scan(xs, dt, A, bm, cm, ssm_h0)
    y = y + lp['ssd_d'].astype(f32)[:, None] * xs.astype(f32)
    ssd_out = rms_norm(y.reshape(B, L, SSD_WIDTH) * jax.nn.silu(sz.astype(f32)), lp['ssd_norm'])
    nsa_q = rope(rms_norm(nq.reshape(B, L, NSA_HEADS, NSA_HEAD_DIM), lp['nsa_q_norm']), pos)
    kv6 = [nkv[..., i * NSA_KV_WIDTH:(i + 1) * NSA_KV_WIDTH].reshape(B, L, NSA_KV_HEADS, NSA_HEAD_DIM) for i in range(6)]
    kc_raw = rope(kv6[0], pos)
    vc_raw = kv6[1]
    ks = rope(rms_norm(kv6[2], lp['nsa_ks_norm']), pos)
    vs = kv6[3]
    kw = rope(rms_norm(kv6[4], lp['nsa_kw_norm']), pos)
    vw = kv6[5]
    gates = jax.nn.sigmoid(ngate.astype(f32)).reshape(B, L, NSA_HEADS, 3)
    return dict(ret_out=ret_out, ret_s=ret_s, ssd_out=ssd_out, ssm_h=ssm_h, conv_new=conv_new,
                q=nsa_q, kc=kc_raw, vc=vc_raw, ks=ks, vs=vs, kw=kw, vw=vw, gates=gates, nz=nz)


def mixer_back(x, f, nsa_o, lp):
    nsa_out = nsa_o * jax.nn.silu(f['nz'].astype(f32))
    cat = jnp.concatenate([f['ret_out'], f['ssd_out'].astype(f32), nsa_out], axis=-1)
    return x + (cat.astype(x.dtype) @ lp['w_out'])


def memory_kv(mem, lp):
    B, M, _ = mem.shape
    m = rms_norm(mem, lp['norm_mem'])
    k = rms_norm((m @ lp['mem_wk']).reshape(B, M, MEM_HEADS, MEM_HEAD_DIM), lp['mem_k_norm'])
    v = (m @ lp['mem_wv']).reshape(B, M, MEM_HEADS, MEM_HEAD_DIM)
    return jnp.stack([k, v], axis=2)


def memory_cross_attn(x, mkv, lp):
    B, L, _ = x.shape
    h = rms_norm(x, lp['norm_cross'])
    q = rms_norm((h @ lp['mem_wq']).reshape(B, L, MEM_HEADS, MEM_HEAD_DIM), lp['mem_q_norm']).astype(f32)
    s = jnp.einsum('blhd,bmhd->bhlm', q, mkv[:, :, 0].astype(f32)) * (MEM_HEAD_DIM ** -0.5)
    p = jax.nn.softmax(s, axis=-1)
    o = jnp.einsum('bhlm,bmhd->blhd', p, mkv[:, :, 1].astype(f32)).reshape(B, L, MEM_WIDTH)
    return x + (o.astype(x.dtype) @ lp['mem_wo'])


def setup_inputs(seed: int = 0) -> dict:
    key = jax.random.key(seed)
    keys = iter(jax.random.split(key, 48))

    def nrm(shape, scale=1.0):
        return jax.random.normal(next(keys), shape, f32) * scale

    def gain(shape):
        return 1.0 + nrm(shape, 0.02)

    n_pages = PAST_LEN // PAGE_SIZE
    n_used = DEC_BATCH * n_pages
    n_pool = n_used + max(1, n_used // 4)
    wbuf = min(WINDOW, PAST_LEN)
    perm = jax.random.permutation(next(keys), n_pool)[:n_used]
    page_table = perm.reshape(DEC_BATCH, n_pages).astype(jnp.int32)
    dt0 = jnp.exp(jax.random.uniform(next(keys), (DEPTH, SSD_HEADS), f32) * (math.log(0.1) - math.log(1e-3)) + math.log(1e-3))
    dt_bias = dt0 + jnp.log(-jnp.expm1(-dt0))
    a_log = jnp.log(jax.random.uniform(next(keys), (DEPTH, SSD_HEADS), f32, 1.0, 16.0))
    return {
        'x_prompt': nrm((BATCH, SEQ, D_MODEL)),
        'x_sample': nrm((DEC_BATCH, DEC_SEQ, D_MODEL)),
        'mem_prompt': nrm((BATCH, MEM_LEN, D_MODEL)),
        'state_ret': nrm((DEPTH, DEC_BATCH, RET_HEADS, RET_DK, RET_DV), 0.1),
        'state_ssm': nrm((DEPTH, DEC_BATCH, SSD_HEADS, SSD_HEADDIM, SSD_STATE), 0.1),
        'state_conv': nrm((DEPTH, DEC_BATCH, SSD_CONV - 1, SSD_CONV_DIM)),
        'cache_nsa_kv': nrm((DEPTH, n_pool, PAGE_SIZE, 4, NSA_KV_HEADS, NSA_HEAD_DIM)),
        'cache_win_kv': nrm((DEPTH, DEC_BATCH, wbuf, 2, NSA_KV_HEADS, NSA_HEAD_DIM)),
        'cache_mem_kv': nrm((DEPTH, DEC_BATCH, MEM_LEN, 2, MEM_HEADS, MEM_HEAD_DIM)),
        'page_table': page_table,
        'norm_mix': gain((DEPTH, D_MODEL)),
        'w_in': nrm((DEPTH, D_MODEL, IN_COLS), D_MODEL ** -0.5),
        'ssd_conv_w': nrm((DEPTH, SSD_CONV, SSD_CONV_DIM), SSD_CONV ** -0.5),
        'ssd_conv_b': nrm((DEPTH, SSD_CONV_DIM), 0.02),
        'ssd_dt_bias': dt_bias,
        'ssd_a_log': a_log,
        'ssd_d': gain((DEPTH, SSD_HEADS)),
        'ssd_norm': gain((DEPTH, SSD_WIDTH)),
        'nsa_q_norm': gain((DEPTH, NSA_HEAD_DIM)),
        'nsa_kc_norm': gain((DEPTH, NSA_HEAD_DIM)),
        'nsa_ks_norm': gain((DEPTH, NSA_HEAD_DIM)),
        'nsa_kw_norm': gain((DEPTH, NSA_HEAD_DIM)),
        'nsa_cmp_pe': nrm((DEPTH, 2, CMP_BLOCK, NSA_HEAD_DIM), 0.02),
        'nsa_cmp_w1': nrm((DEPTH, 2, CMP_BLOCK * NSA_HEAD_DIM, NSA_HEAD_DIM), (CMP_BLOCK * NSA_HEAD_DIM) ** -0.5),
        'nsa_cmp_w2': nrm((DEPTH, 2, NSA_HEAD_DIM, NSA_HEAD_DIM), NSA_HEAD_DIM ** -0.5),
        'w_out': nrm((DEPTH, MIX_WIDTH, D_MODEL), MIX_WIDTH ** -0.5),
        'norm_cross': gain((DEPTH, D_MODEL)),
        'norm_mem': gain((DEPTH, D_MODEL)),
        'mem_wq': nrm((DEPTH, D_MODEL, MEM_WIDTH), D_MODEL ** -0.5),
        'mem_wk': nrm((DEPTH, D_MODEL, MEM_WIDTH), D_MODEL ** -0.5),
        'mem_wv': nrm((DEPTH, D_MODEL, MEM_WIDTH), D_MODEL ** -0.5),
        'mem_q_norm': gain((DEPTH, MEM_HEAD_DIM)),
        'mem_k_norm': gain((DEPTH, MEM_HEAD_DIM)),
        'mem_wo': nrm((DEPTH, MEM_WIDTH, D_MODEL), MEM_WIDTH ** -0.5),
    }


def reference(x_prompt, x_sample, mem_prompt, state_ret, state_ssm, state_conv, cache_nsa_kv, cache_win_kv,
              cache_mem_kv, page_table, norm_mix, w_in, ssd_conv_w, ssd_conv_b, ssd_dt_bias, ssd_a_log, ssd_d,
              ssd_norm, nsa_q_norm, nsa_kc_norm, nsa_ks_norm, nsa_kw_norm, nsa_cmp_pe, nsa_cmp_w1, nsa_cmp_w2,
              w_out, norm_cross, norm_mem, mem_wq, mem_wk, mem_wv, mem_q_norm, mem_k_norm, mem_wo):
    B, L, _ = x_prompt.shape
    DB, DL, _ = x_sample.shape
    n_pages = page_table.shape[1]
    past_len = n_pages * cache_nsa_kv.shape[2]
    wbuf = cache_win_kv.shape[2]
    wp = min(WINDOW, L)
    pos_p = jnp.arange(L, dtype=jnp.int32)
    pos_s = past_len + jnp.arange(DL, dtype=jnp.int32)
    win_pos = past_len - wbuf + jnp.arange(wbuf + DL, dtype=jnp.int32)
    xp, xs = x_prompt, x_sample
    ret_p, ret_s, ssm_p, ssm_s, conv_p, conv_s = [], [], [], [], [], []
    nsa_p, nsa_s, win_p, win_s, mem_p = [], [], [], [], []
    for l in range(DEPTH):
        lp = dict(norm_mix=norm_mix[l], w_in=w_in[l], ssd_conv_w=ssd_conv_w[l], ssd_conv_b=ssd_conv_b[l],
                  ssd_dt_bias=ssd_dt_bias[l], ssd_a_log=ssd_a_log[l], ssd_d=ssd_d[l], ssd_norm=ssd_norm[l],
                  nsa_q_norm=nsa_q_norm[l], nsa_kc_norm=nsa_kc_norm[l], nsa_ks_norm=nsa_ks_norm[l],
                  nsa_kw_norm=nsa_kw_norm[l], nsa_cmp_pe=nsa_cmp_pe[l], nsa_cmp_w1=nsa_cmp_w1[l],
                  nsa_cmp_w2=nsa_cmp_w2[l], w_out=w_out[l], norm_cross=norm_cross[l], norm_mem=norm_mem[l],
                  mem_wq=mem_wq[l], mem_wk=mem_wk[l], mem_wv=mem_wv[l], mem_q_norm=mem_q_norm[l],
                  mem_k_norm=mem_k_norm[l], mem_wo=mem_wo[l])
        fp = mixer_front(xp, pos_p, lp,
                         jnp.zeros((B, RET_HEADS, RET_DK, RET_DV), f32),
                         jnp.zeros((B, SSD_CONV - 1, SSD_CONV_DIM), xp.dtype),
                         jnp.zeros((B, SSD_HEADS, SSD_HEADDIM, SSD_STATE), f32))
        o_p = nsa_prompt(fp['q'], fp['kc'], fp['vc'], fp['ks'], fp['vs'], fp['kw'], fp['vw'], fp['gates'], lp)
        xp = mixer_back(xp, fp, o_p, lp)
        mkv_p = memory_kv(mem_prompt, lp)
        xp = memory_cross_attn(xp, mkv_p, lp)
        ret_p.append(fp['ret_s'])
        ssm_p.append(fp['ssm_h'])
        conv_p.append(fp['conv_new'])
        nsa_p.append(jnp.stack([fp['kc'], fp['vc'], fp['ks'], fp['vs']], axis=2))
        win_p.append(jnp.stack([fp['kw'], fp['vw']], axis=2)[:, L - wp:])
        mem_p.append(mkv_p)
        fs = mixer_front(xs, pos_s, lp, state_ret[l], state_conv[l], state_ssm[l])
        new_rows = jnp.stack([fs['kc'], fs['vc'], fs['ks'], fs['vs']], axis=2)
        past = cache_nsa_kv[l][page_table].reshape(DB, past_len, 4, NSA_KV_HEADS, NSA_HEAD_DIM)
        full = jnp.concatenate([past, new_rows.astype(past.dtype)], axis=1)
        kc_c, vc_c, kc_end = compress_kv(full[:, :, 0], full[:, :, 1], lp)
        win = jnp.concatenate([cache_win_kv[l], jnp.stack([fs['kw'], fs['vw']], axis=2).astype(cache_win_kv.dtype)], axis=1)
        o_s = nsa_attend(fs['q'], pos_s, kc_c, vc_c, kc_end, to_slc_blocks(full[:, :, 2]), to_slc_blocks(full[:, :, 3]),
                         win[:, :, 0], win[:, :, 1], win_pos, fs['gates'])
        xs = mixer_back(xs, fs, o_s, lp)
        xs = memory_cross_attn(xs, cache_mem_kv[l], lp)
        ret_s.append(fs['ret_s'])
        ssm_s.append(fs['ssm_h'])
        conv_s.append(fs['conv_new'])
        nsa_s.append(new_rows)
        win_s.append(win[:, DL:])
    return (xp, xs,
            jnp.stack(ret_p), jnp.stack(ret_s),
            jnp.stack(ssm_p), jnp.stack(ssm_s),
            jnp.stack(conv_p), jnp.stack(conv_s),
            jnp.stack(nsa_p), jnp.stack(nsa_s),
            jnp.stack(win_p), jnp.stack(win_s),
            jnp.stack(mem_p))
```

```python
import functools
import math

import numpy as np
import jax
import jax.numpy as jnp
from jax import lax
from jax.experimental import pallas as pl
from jax.experimental.pallas import tpu as pltpu

F32 = jnp.float32
BF16 = jnp.bfloat16

D_MODEL = 2048
RET_HEADS = 4
RET_DK = 128
RET_WIDTH = 512
CHUNK = 128
SSD_WIDTH = 1024
SSD_HEADDIM = 64
SSD_HEADS = 16
SSD_GROUPS = 2
SSD_STATE = 128
SSD_CONV = 4
SSD_CONV_DIM = 1536
NSA_WIDTH = 512
NSA_HEADS = 8
NSA_HD = 64
NSA_KV_HEADS = 2
CMP_BLOCK = 32
CMP_STRIDE = 16
SLC_BLOCK = 64
SLC_TOPN = 16
N_LOCAL = 2
WINDOW = 512
FORCE_SCORE = 1.0e6
MEM_HEADS = 4
MEM_HD = 128
MEM_WIDTH = 512
ROPE_THETA = 10000.0
EPS = 1e-6
NEG = -1.0e30

IN_SIZES = (512, 512, 512, 512, 1024, 1536, 16, 512, 768, 24, 512)

C_SZ = 2048
C_XBC = 3072
C_NZ = 4608
C_NQ = 5120
C_NKV = 6144
C_SDT = 6912
C_GATE = 7040
N_PROJ = 7168

LANES = 128
VMEM_LIMIT = 56 * 1024 * 1024


def _cparams(sem):
    return pltpu.CompilerParams(dimension_semantics=sem, vmem_limit_bytes=VMEM_LIMIT)


def _bf(x):
    return x.astype(BF16)


def _dot(a, b):
    return jnp.dot(a, b, preferred_element_type=F32)


def _dot_nt(a, b):
    return lax.dot_general(a, b, (((1,), (1,)), ((), ())), preferred_element_type=F32)


def _dot_tn(a, b):
    return lax.dot_general(a, b, (((0,), (0,)), ((), ())), preferred_element_type=F32)


def _dot_split(a, b_bf16):
    hi = a.astype(BF16)
    r = a - hi.astype(F32)
    mid = r.astype(BF16)
    lo = (r - mid.astype(F32)).astype(BF16)
    return _dot(hi, b_bf16) + _dot(mid, b_bf16) + _dot(lo, b_bf16)


def _silu(x):
    return x * jax.nn.sigmoid(x)


def _softplus(x):
    return jnp.maximum(x, 0.0) + jnp.log1p(jnp.exp(-jnp.abs(x)))


def _rope128(x, cos, sin):
    return x * cos + pltpu.roll(x, 64, 1) * sin


def _rope64(x, cos, sin):
    lane = lax.broadcasted_iota(jnp.int32, x.shape, 1)
    first = (lane & 63) < 32
    partner = jnp.where(first, pltpu.roll(x, 96, 1), pltpu.roll(x, 32, 1))
    return x * cos + partner * sin


def _half_mean_mat():
    r = lax.broadcasted_iota(jnp.int32, (LANES, LANES), 0)
    c = lax.broadcasted_iota(jnp.int32, (LANES, LANES), 1)
    return jnp.where((r >> 6) == (c >> 6), 1.0 / 64.0, 0.0).astype(BF16)


def _rms64(x, w, bd):
    ms = _dot_split(x * x, bd)
    return x * lax.rsqrt(ms + EPS) * w


def _inproj_kernel(x_ref, nw_ref, w_ref, o_ref, h_ref):
    @pl.when(pl.program_id(1) == 0)
    def _():
        nw = nw_ref[...]
        rows = min(128, x_ref.shape[0])

        def body(i, c):
            r = pl.ds(pl.multiple_of(i * rows, rows), rows)
            x = x_ref[r, :]
            ms = jnp.mean(x * x, axis=-1, keepdims=True)
            h_ref[r, :] = (x * lax.rsqrt(ms + EPS) * nw).astype(BF16)
            return c

        lax.fori_loop(0, x_ref.shape[0] // rows, body, 0)

    o_ref[...] = _dot(h_ref[...], w_ref[...])


def _inproj(x, nw, w):
    m = x.shape[0]
    tm = min(m, 1024)
    tn = 512
    return pl.pallas_call(
        _inproj_kernel,
        grid=(m // tm, N_PROJ // tn),
        in_specs=[
            pl.BlockSpec((tm, D_MODEL), lambda i, j: (i, 0)),
            pl.BlockSpec((1, D_MODEL), lambda i, j: (0, 0)),
            pl.BlockSpec((D_MODEL, tn), lambda i, j: (0, j)),
        ],
        out_specs=pl.BlockSpec((tm, tn), lambda i, j: (i, j)),
        out_shape=jax.ShapeDtypeStruct((m, N_PROJ), F32),
        scratch_shapes=[pltpu.VMEM((tm, D_MODEL), BF16)],
        compiler_params=_cparams(("arbitrary", "arbitrary")),
        name="inproj",
    )(x, nw.reshape(1, D_MODEL), w)


RET_T = 512


def _ret_prompt_kernel(q_ref, k_ref, v_ref, g_ref, cos_ref, sin_ref, dec_ref, qd_ref, kd_ref, cd_ref,
                       o_ref, so_ref, s_ref):
    i = pl.program_id(0)

    @pl.when(i == 0)
    def _():
        s_ref[...] = jnp.zeros(s_ref.shape, F32)

    for c in range(RET_T // CHUNK):
        rows = slice(c * CHUNK, (c + 1) * CHUNK)
        cos = cos_ref[rows, :]
        sin = sin_ref[rows, :]
        for h in range(RET_HEADS):
            cols = slice(h * 128, (h + 1) * 128)
            q = _rope128(q_ref[rows, cols], cos, sin)
            k = _rope128(k_ref[rows, cols], cos, sin) * (RET_DK ** -0.5)
            v = v_ref[rows, cols]
            s = s_ref[h]
            qb = _bf(q)
            vb = _bf(v)
            att = _dot_nt(qb, _bf(k)) * dec_ref[h]
            o = _dot(_bf(att), vb) + _dot(qb, _bf(s)) * qd_ref[h]
            s_ref[h] = s * cd_ref[h] + _dot_tn(_bf(k * kd_ref[h]), vb)
            r = o * lax.rsqrt(jnp.mean(o * o, axis=-1, keepdims=True) + EPS)
            o_ref[rows, cols] = r * _silu(g_ref[rows, cols])

    @pl.when(i == pl.num_programs(0) - 1)
    def _():
        so_ref[...] = s_ref[...]


def _ret_consts():
    lg = np.log1p(-np.exp2(-5.0 - np.arange(RET_HEADS, dtype=np.float64)))
    idx = np.arange(CHUNK, dtype=np.float64)
    diff = idx[:, None] - idx[None, :]
    dec = np.where(diff[None] >= 0, np.exp(lg[:, None, None] * np.maximum(diff, 0.0)[None]), 0.0)
    qd = np.exp(lg[:, None] * (idx + 1.0)[None])
    kd = np.exp(lg[:, None] * (CHUNK - 1.0 - idx)[None])
    cd = np.exp(lg * CHUNK)
    bc = lambda a: np.ascontiguousarray(np.broadcast_to(a[:, :, None], (RET_HEADS, CHUNK, LANES)))
    return (jnp.asarray(dec, F32), jnp.asarray(bc(qd), F32), jnp.asarray(bc(kd), F32),
            jnp.asarray(np.broadcast_to(cd[:, None, None], (RET_HEADS, 1, LANES)).copy(), F32))


def _ret_prompt(proj, cos, sin):
    l = proj.shape[0]
    dec, qd, kd, cd = _ret_consts()
    full3 = lambda shape: pl.BlockSpec(shape, lambda i: (0, 0, 0))
    return pl.pallas_call(
        _ret_prompt_kernel,
        grid=(l // RET_T,),
        in_specs=[
            pl.BlockSpec((RET_T, 512), lambda i: (i, 0)),
            pl.BlockSpec((RET_T, 512), lambda i: (i, 1)),
            pl.BlockSpec((RET_T, 512), lambda i: (i, 2)),
            pl.BlockSpec((RET_T, 512), lambda i: (i, 3)),
            pl.BlockSpec((RET_T, LANES), lambda i: (i, 0)),
            pl.BlockSpec((RET_T, LANES), lambda i: (i, 0)),
            full3((RET_HEADS, CHUNK, CHUNK)),
            full3((RET_HEADS, CHUNK, LANES)),
            full3((RET_HEADS, CHUNK, LANES)),
            full3((RET_HEADS, 1, LANES)),
        ],
        out_specs=[
            pl.BlockSpec((RET_T, 512), lambda i: (i, 0)),
            full3((RET_HEADS, RET_DK, RET_DK)),
        ],
        out_shape=[
            jax.ShapeDtypeStruct((l, RET_WIDTH), F32),
            jax.ShapeDtypeStruct((RET_HEADS, RET_DK, RET_DK), F32),
        ],
        scratch_shapes=[pltpu.VMEM((RET_HEADS, RET_DK, RET_DK), F32)],
        compiler_params=_cparams(("arbitrary",)),
        name="ret_prompt",
    )(proj, proj, proj, proj, cos, sin, dec, qd, kd, cd)


def _head_expand_mat():
    r = lax.broadcasted_iota(jnp.int32, (LANES, SSD_WIDTH), 0)
    c = lax.broadcasted_iota(jnp.int32, (LANES, SSD_WIDTH), 1)
    return jnp.where(r == (c >> 6), 1.0, 0.0).astype(BF16)


def _ssd_prompt_kernel(z_ref, xbc_ref, dt_ref, cw_ref, cb_ref, dtb_ref, alog_ref, dexp_ref, nw_ref,
                       y_ref, ho_ref, co_ref, ext_ref, ht_ref):
    i = pl.program_id(0)

    @pl.when(i == 0)
    def _():
        ext_ref[0:8, :] = jnp.zeros((8, SSD_CONV_DIM), F32)
        ht_ref[...] = jnp.zeros(ht_ref.shape, F32)

    u = xbc_ref[...]
    ext_ref[8:8 + CHUNK, :] = u
    cw = cw_ref[...]
    conv = (cb_ref[...] + cw[3:4, :] * u + cw[2:3, :] * ext_ref[7:7 + CHUNK, :]
            + cw[1:2, :] * ext_ref[6:6 + CHUNK, :] + cw[0:1, :] * ext_ref[5:5 + CHUNK, :])
    ext_ref[0:8, :] = u[CHUNK - 8:CHUNK, :]
    xbc = _silu(conv)
    xs = xbc[:, 0:SSD_WIDTH]

    dt = _softplus(dt_ref[...] + dtb_ref[...])
    a = dt * (-jnp.exp(alog_ref[...]))
    ri = lax.broadcasted_iota(jnp.int32, (CHUNK, CHUNK), 0)
    ci = lax.broadcasted_iota(jnp.int32, (CHUNK, CHUNK), 1)
    causal = ri >= ci
    tri = jnp.where(causal, 1.0, 0.0).astype(BF16)
    cum = _dot_tri(tri, a)
    cum_t = cum.T
    dt_t = dt.T
    cum_last = cum[CHUNK - 1:CHUNK, :]
    eh = _head_expand_mat()
    ecum_x = _dot_split(jnp.exp(cum), eh)
    wgt_x = _dot_split(jnp.exp(cum_last - cum) * dt, eh)
    elast_x = _dot_split(jnp.broadcast_to(jnp.exp(cum_last), (8, LANES)), eh)[0:1, :]

    lane = lax.broadcasted_iota(jnp.int32, (CHUNK, LANES), 1)
    lo_half = lane < 64
    xw = _bf(xs * wgt_x)
    y_parts = []
    ch_parts = []
    for g in range(SSD_GROUPS):
        bg = xbc[:, SSD_WIDTH + g * 128:SSD_WIDTH + (g + 1) * 128]
        cg = xbc[:, SSD_WIDTH + 256 + g * 128:SSD_WIDTH + 256 + (g + 1) * 128]
        cgb = _bf(cg)
        cb = _dot_nt(cgb, _bf(bg))
        ht = ht_ref[g]
        ch_parts.append(_dot(cgb, _bf(ht)))
        for k in range(4):
            h0 = g * 8 + 2 * k
            xp = _bf(xs[:, h0 * 64:(h0 + 2) * 64])
            ys = []
            for hh in (h0, h0 + 1):
                seg = cum[:, hh:hh + 1] - cum_t[hh:hh + 1, :]
                lm = jnp.where(causal, jnp.exp(jnp.minimum(seg, 0.0)), 0.0)
                sc = cb * lm * dt_t[hh:hh + 1, :]
                ys.append(_dot(_bf(sc), xp))
            y_parts.append(jnp.where(lo_half, ys[0], ys[1]))
        bgt = _bf(bg.T)
        ht_ref[g] = ht * elast_x[:, g * 512:(g + 1) * 512] + _dot(bgt, xw[:, g * 512:(g + 1) * 512])
    y = jnp.concatenate(y_parts, axis=1) + jnp.concatenate(ch_parts, axis=1) * ecum_x + dexp_ref[...] * xs
    gated = y * _silu(z_ref[...])
    y_ref[...] = gated * lax.rsqrt(jnp.mean(gated * gated, axis=-1, keepdims=True) + EPS) * nw_ref[...]

    @pl.when(i == pl.num_programs(0) - 1)
    def _():
        co_ref[...] = u[CHUNK - 8:CHUNK, :]
        for g in range(SSD_GROUPS):
            htf = ht_ref[g]
            for k in range(4):
                h0 = g * 8 + 2 * k
                ho_ref[h0:h0 + 2] = htf[:, k * 128:(k + 1) * 128].T.reshape(2, SSD_HEADDIM, SSD_STATE)


def _dot_tri(tri_bf16, a):
    hi = a.astype(BF16)
    r = a - hi.astype(F32)
    mid = r.astype(BF16)
    lo = (r - mid.astype(F32)).astype(BF16)
    return _dot(tri_bf16, hi) + _dot(tri_bf16, mid) + _dot(tri_bf16, lo)


def _pad_lanes(v, n=LANES):
    v = v.reshape(1, -1)
    return jnp.pad(v, ((0, 0), (0, n - v.shape[1])))


def _ssd_prompt(proj, conv_w, conv_b, dt_bias, a_log, d, norm_w):
    l = proj.shape[0]
    full2 = lambda shape: pl.BlockSpec(shape, lambda i: (0, 0))
    return pl.pallas_call(
        _ssd_prompt_kernel,
        grid=(l // CHUNK,),
        in_specs=[
            pl.BlockSpec((CHUNK, SSD_WIDTH), lambda i: (i, C_SZ // SSD_WIDTH)),
            pl.BlockSpec((CHUNK, SSD_CONV_DIM), lambda i: (i, C_XBC // SSD_CONV_DIM)),
            pl.BlockSpec((CHUNK, LANES), lambda i: (i, C_SDT // LANES)),
            full2((SSD_CONV, SSD_CONV_DIM)),
            full2((1, SSD_CONV_DIM)),
            full2((1, LANES)),
            full2((1, LANES)),
            full2((1, SSD_WIDTH)),
            full2((1, SSD_WIDTH)),
        ],
        out_specs=[
            pl.BlockSpec((CHUNK, SSD_WIDTH), lambda i: (i, 0)),
            pl.BlockSpec((SSD_HEADS, SSD_HEADDIM, SSD_STATE), lambda i: (0, 0, 0)),
            full2((8, SSD_CONV_DIM)),
        ],
        out_shape=[
            jax.ShapeDtypeStruct((l, SSD_WIDTH), F32),
            jax.ShapeDtypeStruct((SSD_HEADS, SSD_HEADDIM, SSD_STATE), F32),
            jax.ShapeDtypeStruct((8, SSD_CONV_DIM), F32),
        ],
        scratch_shapes=[
            pltpu.VMEM((8 + CHUNK, SSD_CONV_DIM), F32),
            pltpu.VMEM((SSD_GROUPS, SSD_STATE, 512), F32),
        ],
        compiler_params=_cparams(("arbitrary",)),
        name="ssd_prompt",
    )(proj, proj, proj, conv_w, conv_b.reshape(1, -1), _pad_lanes(dt_bias), _pad_lanes(a_log),
      jnp.repeat(d, SSD_HEADDIM).reshape(1, -1), norm_w.reshape(1, -1))


def _nsa_prep_kernel(nq_ref, nkv_ref, cos_ref, sin_ref, qn_ref, ksn_ref, kwn_ref,
                     qb_ref, cache_ref, win_ref, kvb_ref):
    cos = cos_ref[...]
    sin = sin_ref[...]
    bd = _half_mean_mat()
    qn = qn_ref[...]
    for h in range(NSA_HEADS):
        cols = slice(h * 128, (h + 1) * 128)
        x = nq_ref[:, cols]
        ms = jnp.sum(x * x, axis=-1, keepdims=True) * (1.0 / NSA_HD)
        qh = _rope64(x * lax.rsqrt(ms + EPS) * qn, cos, sin)
        qb_ref[:, cols] = _bf(qh * (NSA_HD ** -0.5))
    kc = _rope64(nkv_ref[:, 0:128], cos, sin)
    vc = nkv_ref[:, 128:256]
    ks = _rope64(_rms64(nkv_ref[:, 256:384], ksn_ref[...], bd), cos, sin)
    vs = nkv_ref[:, 384:512]
    kw = _rope64(_rms64(nkv_ref[:, 512:640], kwn_ref[...], bd), cos, sin)
    vw = nkv_ref[:, 640:768]
    cache_ref[:, 0:128] = kc
    cache_ref[:, 128:256] = vc
    cache_ref[:, 256:384] = ks
    cache_ref[:, 384:512] = vs
    win_ref[:, 0:128] = kw
    win_ref[:, 128:256] = vw
    kvb_ref[0] = _bf(ks)
    kvb_ref[1] = _bf(vs)
    kvb_ref[2] = _bf(pltpu.roll(vs, 64, 1))
    kvb_ref[3] = _bf(kw)
    kvb_ref[4] = _bf(vw)
    kvb_ref[5] = _bf(pltpu.roll(vw, 64, 1))


def _nsa_prep(proj, cos, sin, qn, ksn, kwn):
    m = proj.shape[0]
    t = min(m, 256)
    two = lambda w: jnp.concatenate([w, w]).reshape(1, LANES)
    full2 = lambda shape: pl.BlockSpec(shape, lambda i: (0, 0))
    return pl.pallas_call(
        _nsa_prep_kernel,
        grid=(m // t,),
        in_specs=[
            pl.BlockSpec((t, 1024), lambda i: (i, C_NQ // 1024)),
            pl.BlockSpec((t, 768), lambda i: (i, C_NKV // 768)),
            pl.BlockSpec((t, LANES), lambda i: (i, 0)),
            pl.BlockSpec((t, LANES), lambda i: (i, 0)),
            full2((1, LANES)), full2((1, LANES)), full2((1, LANES)),
        ],
        out_specs=[
            pl.BlockSpec((t, 1024), lambda i: (i, 0)),
            pl.BlockSpec((t, 512), lambda i: (i, 0)),
            pl.BlockSpec((t, 256), lambda i: (i, 0)),
            pl.BlockSpec((6, t, LANES), lambda i: (0, i, 0)),
        ],
        out_shape=[
            jax.ShapeDtypeStruct((m, 1024), BF16),
            jax.ShapeDtypeStruct((m, 512), F32),
            jax.ShapeDtypeStruct((m, 256), F32),
            jax.ShapeDtypeStruct((6, m, LANES), BF16),
        ],
        compiler_params=_cparams(("arbitrary",)),
        name="nsa_prep",
    )(proj, proj, cos, sin, two(qn), two(ksn), two(kwn))


def _compress_kernel(rk_ref, rv_ref, w1_ref, pe_ref, w2_ref, kn_ref, kc_ref, vc_ref):
    tr = rk_ref.shape[0]
    bd = _half_mean_mat()

    def mlp(r, j):
        ha = _dot(_bf(r + pe_ref[2 * j:2 * j + 1, :]), w1_ref[2 * j])
        hb = _dot(_bf(r + pe_ref[2 * j + 1:2 * j + 2, :]), w1_ref[2 * j + 1])
        hid = ha + pltpu.roll(hb, tr - 1, 0)
        return _dot(_bf(_silu(hid)), w2_ref[j])

    kc = _rms64(mlp(rk_ref[...], 0), kn_ref[...], bd)
    vc = mlp(rv_ref[...], 1)
    kc_ref[...] = _bf(kc)
    vc_ref[0] = _bf(vc)
    vc_ref[1] = _bf(pltpu.roll(vc, 64, 1))


def _compress_weights(pe, w1, w2):
    w1r = w1.reshape(2, 2, 16, 64, 64)
    z = jnp.zeros_like(w1r)
    top = jnp.concatenate([w1r, z], axis=-1)
    bot = jnp.concatenate([z, w1r], axis=-1)
    w1x = jnp.stack([top, bot], axis=3)
    w1x = w1x.reshape(4, 2048, LANES).astype(BF16)
    per = pe.reshape(2, 2, 16, 1, 64)
    pex = jnp.broadcast_to(per, (2, 2, 16, 2, 64)).reshape(4, 2048)
    z2 = jnp.zeros_like(w2)
    w2x = jnp.concatenate([jnp.concatenate([w2, z2], -1), jnp.concatenate([z2, w2], -1)], axis=1).astype(BF16)
    return w1x, pex, w2x


def _compress(rk, rv, w1x, pex, w2x, kn):
    nr = rk.shape[0]
    tr = min(nr, 512)
    two = jnp.concatenate([kn, kn]).reshape(1, LANES)
    return pl.pallas_call(
        _compress_kernel,
        grid=(nr // tr,),
        in_specs=[
            pl.BlockSpec((tr, 2048), lambda i: (i, 0)),
            pl.BlockSpec((tr, 2048), lambda i: (i, 0)),
            pl.BlockSpec((4, 2048, LANES), lambda i: (0, 0, 0)),
            pl.BlockSpec((4, 2048), lambda i: (0, 0)),
            pl.BlockSpec((2, LANES, LANES), lambda i: (0, 0, 0)),
            pl.BlockSpec((1, LANES), lambda i: (0, 0)),
        ],
        out_specs=[
            pl.BlockSpec((tr, LANES), lambda i: (i, 0)),
            pl.BlockSpec((2, tr, LANES), lambda i: (0, i, 0)),
        ],
        out_shape=[
            jax.ShapeDtypeStruct((nr, LANES), BF16),
            jax.ShapeDtypeStruct((2, nr, LANES), BF16),
        ],
        compiler_params=_cparams(("arbitrary",)),
        name="nsa_compress",
    )(rk, rv, w1x, pex, w2x, two)


QB = 128
SEL_TK = 256
WIN_TK = 128


def _nsa_prompt_kernel(q_ref, kc_ref, vc_ref, kvb_ref, ov_ref, gate_ref, eg_ref, nz_ref, o_ref,
                       m_ref, l_ref, acc_ref, st_ref):
    g = pl.program_id(0)
    qi = pl.program_id(1)
    t0 = qi * QB
    nc = kc_ref.shape[0]
    q = q_ref[...]
    qs = jnp.concatenate([q[:, 0:128], q[:, 256:384], q[:, 128:256], q[:, 384:512]], axis=0)
    trow = t0 + (lax.broadcasted_iota(jnp.int32, (4 * QB, 1), 0) & (QB - 1))
    tq = t0 + lax.broadcasted_iota(jnp.int32, (QB, 1), 0)

    def pv(pb, ve, vo):
        return jnp.concatenate([_dot(pb[0:2 * QB], ve), _dot(pb[2 * QB:4 * QB], vo)], axis=0)

    sc = _dot_nt(qs, kc_ref[...])
    n_io = lax.broadcasted_iota(jnp.int32, (1, nc), 1)
    mask_c = (n_io * CMP_STRIDE + (CMP_BLOCK - 1)) <= trow
    sc = jnp.where(mask_c, sc, NEG)
    mc = jnp.max(sc, axis=1, keepdims=True)
    pc = jnp.where(mask_c, jnp.exp(sc - mc), 0.0)
    zc = jnp.sum(pc, axis=1, keepdims=True)
    pc = pc / jnp.where(zc > 0, zc, 1.0)
    o_c = pv(_bf(pc), vc_ref[g], vc_ref[1 - g])

    p4 = pc[0:QB] + pc[QB:2 * QB] + pc[2 * QB:3 * QB] + pc[3 * QB:4 * QB]
    imp = _dot_split(p4, ov_ref[...])
    j_io = lax.broadcasted_iota(jnp.int32, (1, LANES), 1)
    cur = tq >> 6
    forced = (j_io == 0) | ((j_io <= cur) & (j_io > cur - N_LOCAL))
    valid = j_io <= cur
    score = jnp.where(forced, FORCE_SCORE, imp)
    score = jnp.where(valid, score, -FORCE_SCORE)
    score_t = score.T
    st_ref[...] = score_t
    jb = lax.broadcasted_iota(jnp.int32, (LANES, QB), 0)
    n_blk = ((t0 + QB - 1) >> 6) + 1

    def rank_body(i, rank):
        row = st_ref[pl.ds(i, 1), :]
        beats = (row > score_t) | ((row == score_t) & (i < jb))
        return rank + jnp.where(beats, 1.0, 0.0)

    rank_t = lax.fori_loop(0, n_blk, rank_body, jnp.zeros((LANES, QB), F32))
    sel = jnp.where((rank_t.T < SLC_TOPN) & valid, 1.0, 0.0).astype(BF16)

    def attend(kidx, vidx_e, vidx_o, lo, hi, tk, mask_fn):
        m_ref[...] = jnp.full(m_ref.shape, NEG, F32)
        l_ref[...] = jnp.zeros(l_ref.shape, F32)
        acc_ref[...] = jnp.zeros(acc_ref.shape, F32)

        def body(kt, c):
            k0 = pl.multiple_of(kt * tk, tk)
            kk = kvb_ref[kidx, pl.ds(k0, tk), :]
            s = _dot_nt(qs, kk)
            mask = mask_fn(k0)
            s = jnp.where(mask, s, NEG)
            m_old = m_ref[...]
            m_new = jnp.maximum(m_old, jnp.max(s, axis=1, keepdims=True))
            p = jnp.where(mask, jnp.exp(s - m_new), 0.0)
            alpha = jnp.exp(m_old - m_new)
            l_ref[...] = alpha * l_ref[...] + jnp.sum(p, axis=1, keepdims=True)
            acc_ref[...] = alpha * acc_ref[...] + pv(_bf(p), kvb_ref[vidx_e, pl.ds(k0, tk), :],
                                                     kvb_ref[vidx_o, pl.ds(k0, tk), :])
            m_ref[...] = m_new
            return c

        lax.fori_loop(lo, hi, body, 0)
        l = l_ref[...]
        return acc_ref[...] / jnp.where(l > 0, l, 1.0)

    jrow = lax.broadcasted_iota(jnp.int32, (LANES, 1), 0)

    def mask_sel(k0):
        kpos = k0 + lax.broadcasted_iota(jnp.int32, (1, SEL_TK), 1)
        et = jnp.where(jrow == (kpos >> 6), 1.0, 0.0).astype(BF16)
        sm = _dot(sel, et) > 0.5
        sm4 = jnp.concatenate([sm, sm, sm, sm], axis=0)
        return sm4 & (kpos <= trow)

    def mask_win(k0):
        kpos = k0 + lax.broadcasted_iota(jnp.int32, (1, WIN_TK), 1)
        rel = trow - kpos
        return (rel >= 0) & (rel <= WINDOW)

    o_s = attend(0, 1 + g, 2 - g, 0, (t0 + QB - 1) // SEL_TK + 1, SEL_TK, mask_sel)
    o_w = attend(3, 4 + g, 5 - g, jnp.maximum(qi - WINDOW // WIN_TK, 0), qi + 1, WIN_TK, mask_win)

    gx = _dot_split(jax.nn.sigmoid(gate_ref[...]), eg_ref[0])
    lane = lax.broadcasted_iota(jnp.int32, (QB, LANES), 1)
    lo_half = lane < 64
    for k in range(2):
        ra = slice(k * QB, (k + 1) * QB)
        rb = slice(2 * QB + k * QB, 2 * QB + (k + 1) * QB)
        cols = slice(k * 128, (k + 1) * 128)
        tile = lambda o: jnp.where(lo_half, o[ra], o[rb])
        o = (gx[:, k * 128:(k + 1) * 128] * tile(o_c)
             + gx[:, 256 + k * 128:256 + (k + 1) * 128] * tile(o_s)
             + gx[:, 512 + k * 128:512 + (k + 1) * 128] * tile(o_w))
        o_ref[:, cols] = o * _silu(nz_ref[:, cols])


def _nsa_consts(nc):
    n = np.arange(nc)[:, None]
    j = np.arange(LANES)[None, :]
    ov = ((n * CMP_STRIDE < (j + 1) * SLC_BLOCK) & (n * CMP_STRIDE + CMP_BLOCK - 1 >= j * SLC_BLOCK))
    eg = np.zeros((NSA_KV_HEADS, LANES, 3 * 256), np.float32)
    for g in range(NSA_KV_HEADS):
        for hh in range(4):
            for c in range(3):
                eg[g, (g * 4 + hh) * 3 + c, c * 256 + hh * 64:c * 256 + (hh + 1) * 64] = 1.0
    return jnp.asarray(ov.astype(np.float32), BF16), jnp.asarray(eg, BF16)


def _nsa_prompt(qb, kc, vc2, kvb, proj):
    l = qb.shape[0]
    nc = kc.shape[0]
    assert l // SLC_BLOCK <= LANES
    ov, eg = _nsa_consts(nc)
    return pl.pallas_call(
        _nsa_prompt_kernel,
        grid=(NSA_KV_HEADS, l // QB),
        in_specs=[
            pl.BlockSpec((QB, 512), lambda g, i: (i, g)),
            pl.BlockSpec((nc, LANES), lambda g, i: (0, 0)),
            pl.BlockSpec((2, nc, LANES), lambda g, i: (0, 0, 0)),
            pl.BlockSpec((6, l, LANES), lambda g, i: (0, 0, 0)),
            pl.BlockSpec((nc, LANES), lambda g, i: (0, 0)),
            pl.BlockSpec((QB, LANES), lambda g, i: (i, C_GATE // LANES)),
            pl.BlockSpec((1, LANES, 768), lambda g, i: (g, 0, 0)),
            pl.BlockSpec((QB, 256), lambda g, i: (i, C_NZ // 256 + g)),
        ],
        out_specs=pl.BlockSpec((QB, 256), lambda g, i: (i, g)),
        out_shape=jax.ShapeDtypeStruct((l, NSA_WIDTH), F32),
        scratch_shapes=[
            pltpu.VMEM((4 * QB, 1), F32),
            pltpu.VMEM((4 * QB, 1), F32),
            pltpu.VMEM((4 * QB, LANES), F32),
            pltpu.VMEM((LANES, QB), F32),
        ],
        compiler_params=_cparams(("arbitrary", "arbitrary")),
        name="nsa_prompt",
    )(qb, kc, vc2, kvb, ov, proj, eg, proj)


def _mem_kv_kernel(mem_ref, nw_ref, wk_ref, wv_ref, kn_ref, kv_ref, kvb_ref):
    x = mem_ref[...]
    m = _bf(x * lax.rsqrt(jnp.mean(x * x, axis=-1, keepdims=True) + EPS) * nw_ref[...])
    k = _dot(m, wk_ref[...])
    v = _dot(m, wv_ref[...])
    for h in range(MEM_HEADS):
        cols = slice(h * 128, (h + 1) * 128)
        kh = k[:, cols]
        kh = kh * lax.rsqrt(jnp.mean(kh * kh, axis=-1, keepdims=True) + EPS) * kn_ref[...]
        kv_ref[:, cols] = kh
        kvb_ref[:, cols] = _bf(kh)
    kv_ref[:, MEM_WIDTH:2 * MEM_WIDTH] = v
    kvb_ref[:, MEM_WIDTH:2 * MEM_WIDTH] = _bf(v)


def _mem_kv(mem, nw, wk, wv, kn):
    ml = mem.shape[0]
    return pl.pallas_call(
        _mem_kv_kernel,
        out_shape=[jax.ShapeDtypeStruct((ml, 2 * MEM_WIDTH), F32),
                   jax.ShapeDtypeStruct((ml, 2 * MEM_WIDTH), BF16)],
        compiler_params=pltpu.CompilerParams(vmem_limit_bytes=VMEM_LIMIT),
        name="mem_kv",
    )(mem, nw.reshape(1, -1), wk, wv, kn.reshape(1, -1))


TAIL_T = 256


def _tail_kernel(x_ref, ret_ref, ssd_ref, nsa_ref, wout_ref, ncw_ref, wq_ref, qn_ref, kvb_ref, wo_ref, y_ref):
    x1 = (x_ref[...] + _dot(_bf(ret_ref[...]), wout_ref[0:512, :])
          + _dot(_bf(ssd_ref[...]), wout_ref[512:1536, :])
          + _dot(_bf(nsa_ref[...]), wout_ref[1536:2048, :]))
    h = _bf(x1 * lax.rsqrt(jnp.mean(x1 * x1, axis=-1, keepdims=True) + EPS) * ncw_ref[...])
    q = _dot(h, wq_ref[...])
    outs = []
    for hd in range(MEM_HEADS):
        cols = slice(hd * 128, (hd + 1) * 128)
        qh = q[:, cols]
        qh = _bf(qh * lax.rsqrt(jnp.mean(qh * qh, axis=-1, keepdims=True) + EPS) * qn_ref[...])
        s = _dot_nt(qh, kvb_ref[:, cols]) * (MEM_HD ** -0.5)
        s = s - jnp.max(s, axis=-1, keepdims=True)
        p = jnp.exp(s)
        p = p / jnp.sum(p, axis=-1, keepdims=True)
        outs.append(_dot(_bf(p), kvb_ref[:, MEM_WIDTH + hd * 128:MEM_WIDTH + (hd + 1) * 128]))
    o = _bf(jnp.concatenate(outs, axis=1))
    y_ref[...] = x1 + _dot(o, wo_ref[...])


def _tail(x, ret, ssd, nsa, wout, ncw, wq, qn, kvb, wo):
    m = x.shape[0]
    t = min(m, TAIL_T)
    ml = kvb.shape[0]
    full2 = lambda shape: pl.BlockSpec(shape, lambda i: (0, 0))
    return pl.pallas_call(
        _tail_kernel,
        grid=(m // t,),
        in_specs=[
            pl.BlockSpec((t, D_MODEL), lambda i: (i, 0)),
            pl.BlockSpec((t, RET_WIDTH), lambda i: (i, 0)),
            pl.BlockSpec((t, SSD_WIDTH), lambda i: (i, 0)),
            pl.BlockSpec((t, NSA_WIDTH), lambda i: (i, 0)),
            full2((D_MODEL, D_MODEL)),
            full2((1, D_MODEL)),
            full2((D_MODEL, MEM_WIDTH)),
            full2((1, MEM_HD)),
            full2((ml, 2 * MEM_WIDTH)),
            full2((MEM_WIDTH, D_MODEL)),
        ],
        out_specs=pl.BlockSpec((t, D_MODEL), lambda i: (i, 0)),
        out_shape=jax.ShapeDtypeStruct((m, D_MODEL), F32),
        compiler_params=_cparams(("arbitrary",)),
        name="layer_tail",
    )(x, ret, ssd, nsa, wout, ncw.reshape(1, -1), wq, qn.reshape(1, -1), kvb, wo)


def _prep_w_in(w):
    k = w.shape[0]
    nq = w[:, 4624:5136].reshape(k, NSA_HEADS, NSA_HD)
    z = jnp.zeros_like(nq)
    nq_pad = jnp.concatenate([
        jnp.concatenate([nq[:, :4], z[:, :4]], axis=-1),
        jnp.concatenate([z[:, 4:], nq[:, 4:]], axis=-1)], axis=1).reshape(k, NSA_HEADS * LANES)
    padc = lambda a: jnp.pad(a, ((0, 0), (0, LANES - a.shape[1])))
    return jnp.concatenate([
        w[:, 0:4608],
        w[:, 5928:6440],
        nq_pad,
        w[:, 5136:5904],
        padc(w[:, 4608:4624]),
        padc(w[:, 5904:5928]),
    ], axis=1).astype(BF16)


def _rope_tables(pos, head_dim, rows):
    half = head_dim // 2
    inv = jnp.exp(-math.log(ROPE_THETA) * jnp.arange(half, dtype=F32) / half)
    ang = pos.astype(F32)[:, None] * inv[None, :]
    cos = jnp.cos(ang)
    sin = jnp.sin(ang)
    reps = LANES // head_dim
    cos_t = jnp.tile(jnp.concatenate([cos, cos], axis=-1), (1, reps))
    sin_t = jnp.tile(jnp.concatenate([-sin, sin], axis=-1), (1, reps))
    if cos_t.shape[0] != rows:
        cos_t = jnp.broadcast_to(cos_t, (rows, LANES))
        sin_t = jnp.broadcast_to(sin_t, (rows, LANES))
    return cos_t, sin_t


def _j_rms(t):
    return t * lax.rsqrt(jnp.mean(t * t, axis=-1, keepdims=True) + EPS)


def _j_masked_softmax(s, mask):
    s = jnp.where(mask, s, -jnp.inf)
    m = jnp.max(s, axis=-1, keepdims=True)
    m = jnp.where(jnp.isfinite(m), m, 0.0)
    p = jnp.exp(s - m)
    z = jnp.sum(p, axis=-1, keepdims=True)
    return p / jnp.where(z > 0, z, 1.0)


def _j_compress(kc_raw, vc_raw, pe, w1, w2, kn):
    b, t, g, d = kc_raw.shape
    n_cmp = (t - CMP_BLOCK) // CMP_STRIDE + 1
    start = jnp.arange(n_cmp, dtype=jnp.int32) * CMP_STRIDE
    idx = start[:, None] + jnp.arange(CMP_BLOCK, dtype=jnp.int32)[None, :]

    def mlp(raw, j):
        blk = raw[:, idx] + pe[j][None, None, :, None, :]
        blk = jnp.swapaxes(blk, 2, 3).reshape(b, n_cmp, g, CMP_BLOCK * d)
        return jax.nn.silu(blk @ w1[j]) @ w2[j]

    return _j_rms(mlp(kc_raw, 0)) * kn, mlp(vc_raw, 1), start + (CMP_BLOCK - 1)


def _j_slc_blocks(t):
    b, tt, g, d = t.shape
    ns = -(-tt // SLC_BLOCK)
    t = jnp.pad(t, ((0, 0), (0, ns * SLC_BLOCK - tt), (0, 0), (0, 0)))
    return t.reshape(b, ns, SLC_BLOCK, g, d).transpose(0, 3, 1, 2, 4)


def _j_nsa_attend(q, qpos, kc, vc, kc_end, ks_blk, vs_blk, kw, vw, kw_pos, gates):
    b, nq, h, d = q.shape
    g = kc.shape[2]
    hpg = h // g
    ns = ks_blk.shape[2]
    qg = q.reshape(b, nq, g, hpg, d) * (d ** -0.5)
    mask_c = kc_end[None, :] <= qpos[:, None]
    p_c = _j_masked_softmax(jnp.einsum('bqghd,bngd->bqghn', qg, kc), mask_c[None, :, None, None, :])
    o_c = jnp.einsum('bqghn,bngd->bqghd', p_c, vc)
    blk = jnp.arange(ns, dtype=jnp.int32)
    c_start = kc_end - (CMP_BLOCK - 1)
    overlap = ((c_start[:, None] < (blk[None, :] + 1) * SLC_BLOCK)
               & (kc_end[:, None] >= blk[None, :] * SLC_BLOCK)).astype(F32)
    imp = jnp.einsum('bqghn,nj->bqgj', p_c, overlap)
    cur = qpos // SLC_BLOCK
    forced = (blk[None, :] == 0) | ((blk[None, :] <= cur[:, None]) & (blk[None, :] > cur[:, None] - N_LOCAL))
    valid = blk[None, :] <= cur[:, None]
    score = jnp.where(forced[None, :, None, :], FORCE_SCORE, imp)
    score = jnp.where(valid[None, :, None, :], score, -FORCE_SCORE)
    n_sel = min(SLC_TOPN, ns)
    top_val, top_idx = lax.top_k(score, n_sel)
    bi = jnp.arange(b)[:, None, None, None]
    gi = jnp.arange(g)[None, None, :, None]
    kg = ks_blk[bi, gi, top_idx]
    vg = vs_blk[bi, gi, top_idx]
    kpos = top_idx[..., None] * SLC_BLOCK + jnp.arange(SLC_BLOCK, dtype=jnp.int32)
    mask_s = (top_val > -0.5 * FORCE_SCORE)[..., None] & (kpos <= qpos[None, :, None, None, None])
    s_s = jnp.einsum('bqghd,bqgnkd->bqghnk', qg, kg).reshape(b, nq, g, hpg, n_sel * SLC_BLOCK)
    p_s = _j_masked_softmax(s_s, mask_s.reshape(b, nq, g, 1, n_sel * SLC_BLOCK)).reshape(
        b, nq, g, hpg, n_sel, SLC_BLOCK)
    o_s = jnp.einsum('bqghnk,bqgnkd->bqghd', p_s, vg)
    rel = qpos[:, None] - kw_pos[None, :]
    mask_w = (rel >= 0) & (rel <= WINDOW) & (kw_pos[None, :] >= 0)
    p_w = _j_masked_softmax(jnp.einsum('bqghd,bkgd->bqghk', qg, kw), mask_w[None, :, None, None, :])
    o_w = jnp.einsum('bqghk,bkgd->bqghd', p_w, vw)
    gt = gates.reshape(b, nq, g, hpg, 3)
    o = gt[..., 0:1] * o_c + gt[..., 1:2] * o_s + gt[..., 2:3] * o_w
    return o.reshape(b, nq, h * d)


def _j_rope(t, pos):
    half = t.shape[-1] // 2
    inv = jnp.exp(-math.log(ROPE_THETA) * jnp.arange(half, dtype=F32) / half)
    ang = pos.astype(F32)[:, None] * inv[None, :]
    cos = jnp.cos(ang)[:, None, :]
    sin = jnp.sin(ang)[:, None, :]
    t1 = t[..., :half]
    t2 = t[..., half:]
    return jnp.concatenate([t1 * cos - t2 * sin, t2 * cos + t1 * sin], axis=-1)


def _j_sample_layer(xs, pos_s, win_pos, lp, past_len):
    db = xs.shape[0]
    proj = _inproj(xs[:, 0, :], lp['norm_mix'], lp['w_in_p'])
    b = db
    rq, rk, rv, rg = (proj[:, i * 512:(i + 1) * 512] for i in range(4))
    sz = proj[:, C_SZ:C_SZ + 1024]
    sxbc = proj[:, C_XBC:C_XBC + 1536]
    sdt = proj[:, C_SDT:C_SDT + 16]
    nz = proj[:, C_NZ:C_NZ + 512]
    ngate = proj[:, C_GATE:C_GATE + 24]
    q = _j_rope(rq.reshape(b, 1, RET_HEADS, RET_DK), pos_s)[:, 0]
    k = _j_rope(rk.reshape(b, 1, RET_HEADS, RET_DK), pos_s)[:, 0] * (RET_DK ** -0.5)
    v = rv.reshape(b, RET_HEADS, RET_DK)
    gamma = jnp.exp(jnp.log1p(-jnp.exp2(-5.0 - jnp.arange(RET_HEADS, dtype=F32))))
    s0 = lp['state_ret']
    ro = (jnp.einsum('bhd,bhd->bh', q, k)[..., None] * v
          + jnp.einsum('bhd,bhde->bhe', q, s0) * gamma[None, :, None])
    ret_s = s0 * gamma[None, :, None, None] + jnp.einsum('bhd,bhe->bhde', k, v)
    ret_out = _j_rms(ro).reshape(b, RET_WIDTH) * jax.nn.silu(rg)
    full = jnp.concatenate([lp['state_conv'], sxbc[:, None, :]], axis=1)
    conv = lp['ssd_conv_b'] + jnp.sum(full * lp['ssd_conv_w'][None], axis=1)
    conv_new = full[:, 1:]
    xbc = jax.nn.silu(conv)
    x_ = xbc[:, :SSD_WIDTH].reshape(b, SSD_HEADS, SSD_HEADDIM)
    bm = jnp.repeat(xbc[:, SSD_WIDTH:SSD_WIDTH + 256].reshape(b, SSD_GROUPS, SSD_STATE), 8, axis=1)
    cm = jnp.repeat(xbc[:, SSD_WIDTH + 256:].reshape(b, SSD_GROUPS, SSD_STATE), 8, axis=1)
    dt = jax.nn.softplus(sdt + lp['ssd_dt_bias'])
    a = dt * (-jnp.exp(lp['ssd_a_log']))
    h0 = lp['state_ssm']
    ea = jnp.exp(a)
    y = (jnp.einsum('bhn,bhn->bh', cm, bm) * dt)[..., None] * x_ + jnp.einsum('bhn,bhpn->bhp', cm, h0) * ea[..., None]
    ssm_h = h0 * ea[..., None, None] + jnp.einsum('bh,bhp,bhn->bhpn', dt, x_, bm)
    y = y + lp['ssd_d'][None, :, None] * x_
    yg = y.reshape(b, SSD_WIDTH) * jax.nn.silu(sz)
    ssd_out = _j_rms(yg) * lp['ssd_norm']
    cos, sin = _rope_tables(pos_s, NSA_HD, b)
    qb, cache_row, win_row, _ = _nsa_prep(proj, cos, sin, lp['nsa_q_norm'], lp['nsa_ks_norm'], lp['nsa_kw_norm'])
    q8 = qb.astype(F32).reshape(b, NSA_HEADS, 2, NSA_HD)
    q8 = jnp.concatenate([q8[:, :4, 0], q8[:, 4:, 1]], axis=1) * (NSA_HD ** 0.5)
    new_rows = cache_row.reshape(b, 1, 4, NSA_KV_HEADS, NSA_HD)
    past = lp['cache_nsa_kv'][lp['page_table']].reshape(b, past_len, 4, NSA_KV_HEADS, NSA_HD)
    fullkv = jnp.concatenate([past, new_rows], axis=1)
    kc_c, vc_c, kc_end = _j_compress(fullkv[:, :, 0], fullkv[:, :, 1], lp['nsa_cmp_pe'], lp['nsa_cmp_w1'],
                                     lp['nsa_cmp_w2'], lp['nsa_kc_norm'])
    win = jnp.concatenate([lp['cache_win_kv'], win_row.reshape(b, 1, 2, NSA_KV_HEADS, NSA_HD)], axis=1)
    gates = jax.nn.sigmoid(ngate).reshape(b, 1, NSA_HEADS, 3)
    o_s = _j_nsa_attend(q8[:, None], pos_s, kc_c, vc_c, kc_end, _j_slc_blocks(fullkv[:, :, 2]),
                        _j_slc_blocks(fullkv[:, :, 3]), win[:, :, 0], win[:, :, 1], win_pos, gates)
    nsa_out = o_s[:, 0] * jax.nn.silu(nz)
    cat = jnp.concatenate([ret_out, ssd_out, nsa_out], axis=-1)
    x1 = xs[:, 0] + cat @ lp['w_out']
    mkv = lp['cache_mem_kv']
    hq = _j_rms(x1) * lp['norm_cross']
    qm = _j_rms((hq @ lp['mem_wq']).reshape(b, MEM_HEADS, MEM_HD)) * lp['mem_q_norm']
    s = jnp.einsum('bhd,bmhd->bhm', qm, mkv[:, :, 0]) * (MEM_HD ** -0.5)
    p = jax.nn.softmax(s, axis=-1)
    o = jnp.einsum('bhm,bmhd->bhd', p, mkv[:, :, 1]).reshape(b, MEM_WIDTH)
    x2 = x1 + o @ lp['mem_wo']
    return x2[:, None], ret_s, ssm_h, conv_new, new_rows, win[:, 1:]


def kernel(x_prompt, x_sample, mem_prompt, state_ret, state_ssm, state_conv, cache_nsa_kv, cache_win_kv,
           cache_mem_kv, page_table, norm_mix, w_in, ssd_conv_w, ssd_conv_b, ssd_dt_bias, ssd_a_log, ssd_d,
           ssd_norm, nsa_q_norm, nsa_kc_norm, nsa_ks_norm, nsa_kw_norm, nsa_cmp_pe, nsa_cmp_w1, nsa_cmp_w2,
           w_out, norm_cross, norm_mem, mem_wq, mem_wk, mem_wv, mem_q_norm, mem_k_norm, mem_wo):
    b, l, _ = x_prompt.shape
    assert b == 1
    db, dl, _ = x_sample.shape
    assert dl == 1
    depth = w_in.shape[0]
    n_pages = page_table.shape[1]
    past_len = n_pages * cache_nsa_kv.shape[2]
    wbuf = cache_win_kv.shape[2]
    wp = min(WINDOW, l)
    pos_p = jnp.arange(l, dtype=jnp.int32)
    pos_s = past_len + jnp.arange(dl, dtype=jnp.int32)
    win_pos = past_len - wbuf + jnp.arange(wbuf + dl, dtype=jnp.int32)
    cos128, sin128 = _rope_tables(pos_p, RET_DK, l)
    cos64, sin64 = _rope_tables(pos_p, NSA_HD, l)

    xp = x_prompt[0]
    xs = x_sample
    outs = [[] for _ in range(11)]
    for li in range(depth):
        w_in_p = _prep_w_in(w_in[li])
        w_out_b = w_out[li].astype(BF16)
        wq_b = mem_wq[li].astype(BF16)
        wo_b = mem_wo[li].astype(BF16)
        proj = _inproj(xp, norm_mix[li], w_in_p)
        ret_out, ret_s = _ret_prompt(proj, cos128, sin128)
        ssd_out, ssm_h, conv8 = _ssd_prompt(proj, ssd_conv_w[li], ssd_conv_b[li], ssd_dt_bias[li], ssd_a_log[li],
                                            ssd_d[li], ssd_norm[li])
        qb, cache, win, kvb = _nsa_prep(proj, cos64, sin64, nsa_q_norm[li], nsa_ks_norm[li], nsa_kw_norm[li])
        w1x, pex, w2x = _compress_weights(nsa_cmp_pe[li], nsa_cmp_w1[li], nsa_cmp_w2[li])
        rk = cache[:, 0:128].reshape(l // CMP_STRIDE, CMP_STRIDE * LANES)
        rv = cache[:, 128:256].reshape(l // CMP_STRIDE, CMP_STRIDE * LANES)
        kc_b, vc2 = _compress(rk, rv, w1x, pex, w2x, nsa_kc_norm[li])
        nsa_out = _nsa_prompt(qb, kc_b, vc2, kvb, proj)
        mkv, mkv_b = _mem_kv(mem_prompt[0], norm_mem[li], mem_wk[li].astype(BF16), mem_wv[li].astype(BF16),
                             mem_k_norm[li])
        xp = _tail(xp, ret_out, ssd_out, nsa_out, w_out_b, norm_cross[li], wq_b, mem_q_norm[li], mkv_b, wo_b)
        outs[0].append(ret_s[None])
        outs[2].append(ssm_h[None])
        outs[4].append(conv8[None, 8 - (SSD_CONV - 1):])
        outs[6].append(cache.reshape(1, l, 4, NSA_KV_HEADS, NSA_HD))
        outs[8].append(win.reshape(1, l, 2, NSA_KV_HEADS, NSA_HD)[:, l - wp:])
        outs[10].append(mkv.reshape(1, -1, 2, MEM_HEADS, MEM_HD))
        lp = dict(norm_mix=norm_mix[li], w_in_p=w_in_p, ssd_conv_w=ssd_conv_w[li], ssd_conv_b=ssd_conv_b[li],
                  ssd_dt_bias=ssd_dt_bias[li], ssd_a_log=ssd_a_log[li], ssd_d=ssd_d[li], ssd_norm=ssd_norm[li],
                  nsa_q_norm=nsa_q_norm[li], nsa_kc_norm=nsa_kc_norm[li], nsa_ks_norm=nsa_ks_norm[li],
                  nsa_kw_norm=nsa_kw_norm[li], nsa_cmp_pe=nsa_cmp_pe[li], nsa_cmp_w1=nsa_cmp_w1[li],
                  nsa_cmp_w2=nsa_cmp_w2[li], w_out=w_out[li], norm_cross=norm_cross[li], mem_wq=mem_wq[li],
                  mem_q_norm=mem_q_norm[li], mem_wo=mem_wo[li], state_ret=state_ret[li], state_ssm=state_ssm[li],
                  state_conv=state_conv[li], cache_nsa_kv=cache_nsa_kv[li], cache_win_kv=cache_win_kv[li],
                  cache_mem_kv=cache_mem_kv[li], page_table=page_table)
        xs, s_ret, s_ssm, s_conv, s_rows, s_win = _j_sample_layer(xs, pos_s, win_pos, lp, past_len)
        outs[1].append(s_ret)
        outs[3].append(s_ssm)
        outs[5].append(s_conv)
        outs[7].append(s_rows)
        outs[9].append(s_win)
    st = [jnp.stack(o) for o in outs]
    return (xp[None], xs, st[0], st[1], st[2], st[3], st[4], st[5], st[6], st[7], st[8], st[9], st[10])
```

```python
import functools
import math

import numpy as np
import jax
import jax.numpy as jnp
from jax import lax
from jax.experimental import pallas as pl
from jax.experimental.pallas import tpu as pltpu

F32 = jnp.float32
BF16 = jnp.bfloat16

D_MODEL = 2048
RET_HEADS = 4
RET_DK = 128
RET_WIDTH = 512
CHUNK = 128
SSD_WIDTH = 1024
SSD_HEADDIM = 64
SSD_HEADS = 16
SSD_GROUPS = 2
SSD_STATE = 128
SSD_CONV = 4
SSD_CONV_DIM = 1536
NSA_WIDTH = 512
NSA_HEADS = 8
NSA_HD = 64
NSA_KV_HEADS = 2
CMP_BLOCK = 32
CMP_STRIDE = 16
SLC_BLOCK = 64
SLC_TOPN = 16
N_LOCAL = 2
WINDOW = 512
FORCE_SCORE = 1.0e6
MEM_HEADS = 4
MEM_HD = 128
MEM_WIDTH = 512
ROPE_THETA = 10000.0
EPS = 1e-6
NEG = -1.0e30

C_SZ = 2048
C_XBC = 3072
C_NZ = 4608
C_NQ = 5120
C_NKV = 6144
C_SDT = 6912
C_GATE = 7040
N_PROJ = 7168

LANES = 128
VMEM_LIMIT = 56 * 1024 * 1024


def _cparams(sem):
    return pltpu.CompilerParams(dimension_semantics=sem, vmem_limit_bytes=VMEM_LIMIT)


def _bf(x):
    return x.astype(BF16)


def _dot(a, b):
    return jnp.dot(a, b, preferred_element_type=F32)


def _dot_nt(a, b):
    return lax.dot_general(a, b, (((1,), (1,)), ((), ())), preferred_element_type=F32)


def _dot_tn(a, b):
    return lax.dot_general(a, b, (((0,), (0,)), ((), ())), preferred_element_type=F32)


def _split3(a):
    hi = a.astype(BF16)
    r = a - hi.astype(F32)
    mid = r.astype(BF16)
    lo = (r - mid.astype(F32)).astype(BF16)
    return hi, mid, lo


def _split2(a):
    hi = a.astype(BF16)
    lo = (a - hi.astype(F32)).astype(BF16)
    return hi, lo


def _dot_split(a, b_bf16):
    hi, mid, lo = _split3(a)
    return _dot(hi, b_bf16) + _dot(mid, b_bf16) + _dot(lo, b_bf16)


def _dot_tri(tri_bf16, a):
    hi, mid, lo = _split3(a)
    return _dot(tri_bf16, hi) + _dot(tri_bf16, mid) + _dot(tri_bf16, lo)


def _silu(x):
    return x * jax.nn.sigmoid(x)


def _softplus(x):
    return jnp.maximum(x, 0.0) + jnp.log1p(jnp.exp(-jnp.abs(x)))


def _rope128(x, cos, sin):
    return x * cos + pltpu.roll(x, 64, 1) * sin


def _rope64(x, cos, sin):
    lane = lax.broadcasted_iota(jnp.int32, x.shape, 1)
    first = (lane & 63) < 32
    partner = jnp.where(first, pltpu.roll(x, 96, 1), pltpu.roll(x, 32, 1))
    return x * cos + partner * sin


def _half_mean_mat():
    r = lax.broadcasted_iota(jnp.int32, (LANES, LANES), 0)
    c = lax.broadcasted_iota(jnp.int32, (LANES, LANES), 1)
    return jnp.where((r >> 6) == (c >> 6), 1.0 / 64.0, 0.0).astype(BF16)


def _rms64(x, w, bd):
    ms = _dot_split(x * x, bd)
    return x * lax.rsqrt(ms + EPS) * w


def _pad_lanes(v, n=LANES):
    v = v.reshape(1, -1)
    return jnp.pad(v, ((0, 0), (0, n - v.shape[1])))


def _row_only(x, b):
    rid = lax.broadcasted_iota(jnp.int32, (x.shape[0], 1), 0)
    return jnp.where(rid == b, x, jnp.zeros_like(x))


def _inproj_kernel(x_ref, nw_ref, w_ref, o_ref, h_ref):
    @pl.when(pl.program_id(1) == 0)
    def _():
        nw = nw_ref[...]
        rows = min(128, x_ref.shape[0])

        def body(i, c):
            r = pl.ds(pl.multiple_of(i * rows, rows), rows)
            x = x_ref[r, :]
            ms = jnp.mean(x * x, axis=-1, keepdims=True)
            h_ref[r, :] = (x * lax.rsqrt(ms + EPS) * nw).astype(BF16)
            return c

        lax.fori_loop(0, x_ref.shape[0] // rows, body, 0)

    o_ref[...] = _dot(h_ref[...], w_ref[...])


def _inproj(x, nw, w):
    m = x.shape[0]
    tm = min(m, 1024)
    tn = 512
    return pl.pallas_call(
        _inproj_kernel,
        grid=(m // tm, N_PROJ // tn),
        in_specs=[
            pl.BlockSpec((tm, D_MODEL), lambda i, j: (i, 0)),
            pl.BlockSpec((1, D_MODEL), lambda i, j: (0, 0)),
            pl.BlockSpec((D_MODEL, tn), lambda i, j: (0, j)),
        ],
        out_specs=pl.BlockSpec((tm, tn), lambda i, j: (i, j)),
        out_shape=jax.ShapeDtypeStruct((m, N_PROJ), F32),
        scratch_shapes=[pltpu.VMEM((tm, D_MODEL), BF16)],
        compiler_params=_cparams(("arbitrary", "arbitrary")),
        name="inproj",
    )(x, nw.reshape(1, D_MODEL), w)


RET_T = 512


def _ret_prompt_kernel(q_ref, k_ref, v_ref, g_ref, cos_ref, sin_ref, dec_ref, qd_ref, kd_ref, cd_ref,
                       o_ref, so_ref, s_ref):
    i = pl.program_id(0)

    @pl.when(i == 0)
    def _():
        s_ref[...] = jnp.zeros(s_ref.shape, F32)

    for c in range(RET_T // CHUNK):
        rows = slice(c * CHUNK, (c + 1) * CHUNK)
        cos = cos_ref[rows, :]
        sin = sin_ref[rows, :]
        for h in range(RET_HEADS):
            cols = slice(h * 128, (h + 1) * 128)
            q = _rope128(q_ref[rows, cols], cos, sin)
            k = _rope128(k_ref[rows, cols], cos, sin) * (RET_DK ** -0.5)
            v = v_ref[rows, cols]
            s = s_ref[h]
            qb = _bf(q)
            vb = _bf(v)
            att = _dot_nt(qb, _bf(k)) * dec_ref[h]
            o = _dot(_bf(att), vb) + _dot(qb, _bf(s)) * qd_ref[h]
            s_ref[h] = s * cd_ref[h] + _dot_tn(_bf(k * kd_ref[h]), vb)
            r = o * lax.rsqrt(jnp.mean(o * o, axis=-1, keepdims=True) + EPS)
            o_ref[rows, cols] = r * _silu(g_ref[rows, cols])

    @pl.when(i == pl.num_programs(0) - 1)
    def _():
        so_ref[...] = s_ref[...]


def _ret_gamma():
    return 1.0 - np.exp2(-5.0 - np.arange(RET_HEADS, dtype=np.float64))


def _ret_consts():
    lg = np.log(_ret_gamma())
    idx = np.arange(CHUNK, dtype=np.float64)
    diff = idx[:, None] - idx[None, :]
    dec = np.where(diff[None] >= 0, np.exp(lg[:, None, None] * np.maximum(diff, 0.0)[None]), 0.0)
    qd = np.exp(lg[:, None] * (idx + 1.0)[None])
    kd = np.exp(lg[:, None] * (CHUNK - 1.0 - idx)[None])
    cd = np.exp(lg * CHUNK)
    bc = lambda a: np.ascontiguousarray(np.broadcast_to(a[:, :, None], (RET_HEADS, CHUNK, LANES)))
    return (jnp.asarray(dec, F32), jnp.asarray(bc(qd), F32), jnp.asarray(bc(kd), F32),
            jnp.asarray(np.broadcast_to(cd[:, None, None], (RET_HEADS, 1, LANES)).copy(), F32))


def _ret_prompt(proj, cos, sin):
    l = proj.shape[0]
    dec, qd, kd, cd = _ret_consts()
    full3 = lambda shape: pl.BlockSpec(shape, lambda i: (0, 0, 0))
    return pl.pallas_call(
        _ret_prompt_kernel,
        grid=(l // RET_T,),
        in_specs=[
            pl.BlockSpec((RET_T, 512), lambda i: (i, 0)),
            pl.BlockSpec((RET_T, 512), lambda i: (i, 1)),
            pl.BlockSpec((RET_T, 512), lambda i: (i, 2)),
            pl.BlockSpec((RET_T, 512), lambda i: (i, 3)),
            pl.BlockSpec((RET_T, LANES), lambda i: (i, 0)),
            pl.BlockSpec((RET_T, LANES), lambda i: (i, 0)),
            full3((RET_HEADS, CHUNK, CHUNK)),
            full3((RET_HEADS, CHUNK, LANES)),
            full3((RET_HEADS, CHUNK, LANES)),
            full3((RET_HEADS, 1, LANES)),
        ],
        out_specs=[
            pl.BlockSpec((RET_T, 512), lambda i: (i, 0)),
            full3((RET_HEADS, RET_DK, RET_DK)),
        ],
        out_shape=[
            jax.ShapeDtypeStruct((l, RET_WIDTH), F32),
            jax.ShapeDtypeStruct((RET_HEADS, RET_DK, RET_DK), F32),
        ],
        scratch_shapes=[pltpu.VMEM((RET_HEADS, RET_DK, RET_DK), F32)],
        compiler_params=_cparams(("arbitrary",)),
        name="ret_prompt",
    )(proj, proj, proj, proj, cos, sin, dec, qd, kd, cd)


DEC_BB = 8


def _ret_decode_kernel(q_ref, k_ref, v_ref, g_ref, cos_ref, sin_ref, gam_ref, s_ref, o_ref, so_ref):
    cos = cos_ref[...]
    sin = sin_ref[...]
    for h in range(RET_HEADS):
        cols = slice(h * 128, (h + 1) * 128)
        qb = _bf(_rope128(q_ref[:, cols], cos, sin))
        kb = _bf(_rope128(k_ref[:, cols], cos, sin) * (RET_DK ** -0.5))
        vb = _bf(v_ref[:, cols])
        gam = gam_ref[h]
        qk = jnp.sum(qb.astype(F32) * kb.astype(F32), axis=-1, keepdims=True)
        o = _bf(qk).astype(F32) * vb.astype(F32)
        rows = []
        for b in range(DEC_BB):
            s = s_ref[b, h]
            rows.append(_dot(qb, _bf(s))[b:b + 1])
            so_ref[b, h] = s * gam + _dot_tn(_row_only(kb, b), vb)
        o = o + jnp.concatenate(rows, axis=0) * gam
        r = o * lax.rsqrt(jnp.mean(o * o, axis=-1, keepdims=True) + EPS)
        o_ref[:, cols] = r * _silu(g_ref[:, cols])


def _ret_decode(proj, cos, sin, state):
    db = proj.shape[0]
    gam = jnp.asarray(np.broadcast_to(_ret_gamma()[:, None, None], (RET_HEADS, 1, LANES)).copy(), F32)
    return pl.pallas_call(
        _ret_decode_kernel,
        grid=(db // DEC_BB,),
        in_specs=[
            pl.BlockSpec((DEC_BB, 512), lambda i: (i, 0)),
            pl.BlockSpec((DEC_BB, 512), lambda i: (i, 1)),
            pl.BlockSpec((DEC_BB, 512), lambda i: (i, 2)),
            pl.BlockSpec((DEC_BB, 512), lambda i: (i, 3)),
            pl.BlockSpec((DEC_BB, LANES), lambda i: (i, 0)),
            pl.BlockSpec((DEC_BB, LANES), lambda i: (i, 0)),
            pl.BlockSpec((RET_HEADS, 1, LANES), lambda i: (0, 0, 0)),
            pl.BlockSpec((DEC_BB, RET_HEADS, RET_DK, RET_DK), lambda i: (i, 0, 0, 0)),
        ],
        out_specs=[
            pl.BlockSpec((DEC_BB, 512), lambda i: (i, 0)),
            pl.BlockSpec((DEC_BB, RET_HEADS, RET_DK, RET_DK), lambda i: (i, 0, 0, 0)),
        ],
        out_shape=[
            jax.ShapeDtypeStruct((db, RET_WIDTH), F32),
            jax.ShapeDtypeStruct(state.shape, F32),
        ],
        compiler_params=_cparams(("arbitrary",)),
        name="ret_decode",
    )(proj, proj, proj, proj, cos, sin, gam, state)


def _head_expand_mat():
    r = lax.broadcasted_iota(jnp.int32, (LANES, SSD_WIDTH), 0)
    c = lax.broadcasted_iota(jnp.int32, (LANES, SSD_WIDTH), 1)
    return jnp.where(r == (c >> 6), 1.0, 0.0).astype(BF16)


def _ssd_prompt_kernel(z_ref, xbc_ref, dt_ref, cw_ref, cb_ref, dtb_ref, alog_ref, dexp_ref, nw_ref,
                       y_ref, ho_ref, co_ref, ext_ref, ht_ref):
    i = pl.program_id(0)

    @pl.when(i == 0)
    def _():
        ext_ref[0:8, :] = jnp.zeros((8, SSD_CONV_DIM), F32)
        ht_ref[...] = jnp.zeros(ht_ref.shape, F32)

    u = xbc_ref[...]
    ext_ref[8:8 + CHUNK, :] = u
    cw = cw_ref[...]
    conv = (cb_ref[...] + cw[3:4, :] * u + cw[2:3, :] * ext_ref[7:7 + CHUNK, :]
            + cw[1:2, :] * ext_ref[6:6 + CHUNK, :] + cw[0:1, :] * ext_ref[5:5 + CHUNK, :])
    ext_ref[0:8, :] = u[CHUNK - 8:CHUNK, :]
    xbc = _silu(conv)
    xs = xbc[:, 0:SSD_WIDTH]

    dt = _softplus(dt_ref[...] + dtb_ref[...])
    a = dt * (-jnp.exp(alog_ref[...]))
    ri = lax.broadcasted_iota(jnp.int32, (CHUNK, CHUNK), 0)
    ci = lax.broadcasted_iota(jnp.int32, (CHUNK, CHUNK), 1)
    causal = ri >= ci
    tri = jnp.where(causal, 1.0, 0.0).astype(BF16)
    cum = _dot_tri(tri, a)
    cum_t = cum.T
    dt_t = dt.T
    cum_last = cum[CHUNK - 1:CHUNK, :]
    eh = _head_expand_mat()
    ecum_x = _dot_split(jnp.exp(cum), eh)
    wgt_x = _dot_split(jnp.exp(cum_last - cum) * dt, eh)
    elast_x = _dot_split(jnp.broadcast_to(jnp.exp(cum_last), (8, LANES)), eh)[0:1, :]

    lane = lax.broadcasted_iota(jnp.int32, (CHUNK, LANES), 1)
    lo_half = lane < 64
    xw = _bf(xs * wgt_x)
    y_parts = []
    ch_parts = []
    for g in range(SSD_GROUPS):
        bg = xbc[:, SSD_WIDTH + g * 128:SSD_WIDTH + (g + 1) * 128]
        cg = xbc[:, SSD_WIDTH + 256 + g * 128:SSD_WIDTH + 256 + (g + 1) * 128]
        cgb = _bf(cg)
        cb = _dot_nt(cgb, _bf(bg))
        ht = ht_ref[g]
        ch_parts.append(_dot(cgb, _bf(ht)))
        for k in range(4):
            h0 = g * 8 + 2 * k
            xp = _bf(xs[:, h0 * 64:(h0 + 2) * 64])
            ys = []
            for hh in (h0, h0 + 1):
                seg = cum[:, hh:hh + 1] - cum_t[hh:hh + 1, :]
                lm = jnp.where(causal, jnp.exp(jnp.minimum(seg, 0.0)), 0.0)
                sc = cb * lm * dt_t[hh:hh + 1, :]
                ys.append(_dot(_bf(sc), xp))
            y_parts.append(jnp.where(lo_half, ys[0], ys[1]))
        bgt = _bf(bg.T)
        ht_ref[g] = ht * elast_x[:, g * 512:(g + 1) * 512] + _dot(bgt, xw[:, g * 512:(g + 1) * 512])
    y = jnp.concatenate(y_parts, axis=1) + jnp.concatenate(ch_parts, axis=1) * ecum_x + dexp_ref[...] * xs
    gated = y * _silu(z_ref[...])
    y_ref[...] = gated * lax.rsqrt(jnp.mean(gated * gated, axis=-1, keepdims=True) + EPS) * nw_ref[...]

    @pl.when(i == pl.num_programs(0) - 1)
    def _():
        co_ref[...] = u[CHUNK - 8:CHUNK, :]
        for g in range(SSD_GROUPS):
            htf = ht_ref[g]
            for k in range(4):
                h0 = g * 8 + 2 * k
                ho_ref[h0:h0 + 2] = htf[:, k * 128:(k + 1) * 128].T.reshape(2, SSD_HEADDIM, SSD_STATE)


def _ssd_prompt(proj, conv_w, conv_b, dt_bias, a_log, d, norm_w):
    l = proj.shape[0]
    full2 = lambda shape: pl.BlockSpec(shape, lambda i: (0, 0))
    return pl.pallas_call(
        _ssd_prompt_kernel,
        grid=(l // CHUNK,),
        in_specs=[
            pl.BlockSpec((CHUNK, SSD_WIDTH), lambda i: (i, C_SZ // SSD_WIDTH)),
            pl.BlockSpec((CHUNK, SSD_CONV_DIM), lambda i: (i, C_XBC // SSD_CONV_DIM)),
            pl.BlockSpec((CHUNK, LANES), lambda i: (i, C_SDT // LANES)),
            full2((SSD_CONV, SSD_CONV_DIM)),
            full2((1, SSD_CONV_DIM)),
            full2((1, LANES)),
            full2((1, LANES)),
            full2((1, SSD_WIDTH)),
            full2((1, SSD_WIDTH)),
        ],
        out_specs=[
            pl.BlockSpec((CHUNK, SSD_WIDTH), lambda i: (i, 0)),
            pl.BlockSpec((SSD_HEADS, SSD_HEADDIM, SSD_STATE), lambda i: (0, 0, 0)),
            full2((8, SSD_CONV_DIM)),
        ],
        out_shape=[
            jax.ShapeDtypeStruct((l, SSD_WIDTH), F32),
            jax.ShapeDtypeStruct((SSD_HEADS, SSD_HEADDIM, SSD_STATE), F32),
            jax.ShapeDtypeStruct((8, SSD_CONV_DIM), F32),
        ],
        scratch_shapes=[
            pltpu.VMEM((8 + CHUNK, SSD_CONV_DIM), F32),
            pltpu.VMEM((SSD_GROUPS, SSD_STATE, 512), F32),
        ],
        compiler_params=_cparams(("arbitrary",)),
        name="ssd_prompt",
    )(proj, proj, proj, conv_w, conv_b.reshape(1, -1), _pad_lanes(dt_bias), _pad_lanes(a_log),
      jnp.repeat(d, SSD_HEADDIM).reshape(1, -1), norm_w.reshape(1, -1))


def _ssd_decode_kernel(z_ref, xbc_ref, dt_ref, cs_ref, h_ref, cw_ref, cb_ref, dtb_ref, alog_ref, dexp_ref, nw_ref,
                       y_ref, co_ref, ho_ref):
    cd = SSD_CONV_DIM
    u = xbc_ref[...]
    c0 = cs_ref[:, 0:cd]
    c1 = cs_ref[:, cd:2 * cd]
    c2 = cs_ref[:, 2 * cd:3 * cd]
    cw = cw_ref[...]
    conv = cb_ref[...] + cw[3:4, :] * u + cw[2:3, :] * c2 + cw[1:2, :] * c1 + cw[0:1, :] * c0
    co_ref[:, 0:cd] = c1
    co_ref[:, cd:2 * cd] = c2
    co_ref[:, 2 * cd:3 * cd] = u
    xbc = _silu(conv)
    xs = xbc[:, 0:SSD_WIDTH]
    dt = _softplus(dt_ref[...] + dtb_ref[...])
    ea = jnp.exp(dt * (-jnp.exp(alog_ref[...])))
    eh = _head_expand_mat()
    dt_x = _dot_split(dt, eh)
    ea_x = _dot_split(ea, eh)
    dtx = dt_x * xs
    ones = jnp.ones((DEC_BB, LANES), BF16)
    ych = [[None, None] for _ in range(DEC_BB)]
    cbs = []
    for g in range(SSD_GROUPS):
        gc = slice(g * 512, (g + 1) * 512)
        bg = xbc[:, SSD_WIDTH + g * 128:SSD_WIDTH + (g + 1) * 128]
        cg = xbc[:, SSD_WIDTH + 256 + g * 128:SSD_WIDTH + 256 + (g + 1) * 128]
        cgb = _bf(cg)
        cbs.append(jnp.sum(cgb.astype(F32) * _bf(bg).astype(F32), axis=-1, keepdims=True))
        b_hi, b_lo = _split2(bg)
        for b in range(DEC_BB):
            hs = h_ref[b, g * 8:(g + 1) * 8].reshape(512, SSD_STATE)
            ych[b][g] = _dot_nt(cgb, _bf(hs))[b:b + 1]
            e_hi, e_lo = _split2(_row_only(ea_x[:, gc], b))
            decay = _dot_tn(e_hi, ones) + _dot_tn(e_lo, ones)
            x_hi, x_lo = _split2(_row_only(dtx[:, gc], b))
            upd = _dot_tn(x_hi, b_hi) + _dot_tn(x_hi, b_lo) + _dot_tn(x_lo, b_hi)
            ho_ref[b, g * 8:(g + 1) * 8] = (hs * decay + upd).reshape(8, SSD_HEADDIM, SSD_STATE)
    ych = jnp.concatenate([jnp.concatenate(r, axis=1) for r in ych], axis=0)
    lane = lax.broadcasted_iota(jnp.int32, (DEC_BB, SSD_WIDTH), 1)
    cbx = jnp.where(lane < 512, cbs[0], cbs[1])
    y = dt_x * cbx * xs + ych * ea_x + dexp_ref[...] * xs
    gated = y * _silu(z_ref[...])
    y_ref[...] = gated * lax.rsqrt(jnp.mean(gated * gated, axis=-1, keepdims=True) + EPS) * nw_ref[...]


def _ssd_decode(proj, conv_state, ssm_state, conv_w, conv_b, dt_bias, a_log, d, norm_w):
    db = proj.shape[0]
    full2 = lambda shape: pl.BlockSpec(shape, lambda i: (0, 0))
    cs = conv_state.reshape(db, (SSD_CONV - 1) * SSD_CONV_DIM)
    y, co, ho = pl.pallas_call(
        _ssd_decode_kernel,
        grid=(db // DEC_BB,),
        in_specs=[
            pl.BlockSpec((DEC_BB, SSD_WIDTH), lambda i: (i, C_SZ // SSD_WIDTH)),
            pl.BlockSpec((DEC_BB, SSD_CONV_DIM), lambda i: (i, C_XBC // SSD_CONV_DIM)),
            pl.BlockSpec((DEC_BB, LANES), lambda i: (i, C_SDT // LANES)),
            pl.BlockSpec((DEC_BB, 3 * SSD_CONV_DIM), lambda i: (i, 0)),
            pl.BlockSpec((DEC_BB, SSD_HEADS, SSD_HEADDIM, SSD_STATE), lambda i: (i, 0, 0, 0)),
            full2((SSD_CONV, SSD_CONV_DIM)),
            full2((1, SSD_CONV_DIM)),
            full2((1, LANES)),
            full2((1, LANES)),
            full2((1, SSD_WIDTH)),
            full2((1, SSD_WIDTH)),
        ],
        out_specs=[
            pl.BlockSpec((DEC_BB, SSD_WIDTH), lambda i: (i, 0)),
            pl.BlockSpec((DEC_BB, 3 * SSD_CONV_DIM), lambda i: (i, 0)),
            pl.BlockSpec((DEC_BB, SSD_HEADS, SSD_HEADDIM, SSD_STATE), lambda i: (i, 0, 0, 0)),
        ],
        out_shape=[
            jax.ShapeDtypeStruct((db, SSD_WIDTH), F32),
            jax.ShapeDtypeStruct((db, 3 * SSD_CONV_DIM), F32),
            jax.ShapeDtypeStruct(ssm_state.shape, F32),
        ],
        compiler_params=_cparams(("arbitrary",)),
        name="ssd_decode",
    )(proj, proj, proj, cs, ssm_state, conv_w, conv_b.reshape(1, -1), _pad_lanes(dt_bias), _pad_lanes(a_log),
      jnp.repeat(d, SSD_HEADDIM).reshape(1, -1), norm_w.reshape(1, -1))
    return y, co.reshape(db, SSD_CONV - 1, SSD_CONV_DIM), ho


def _value_variants(v):
    lane = lax.broadcasted_iota(jnp.int32, v.shape, 1)
    lo = lane < 64
    sw = pltpu.roll(v, 64, 1)
    one = jnp.ones_like(v)
    return [jnp.where(lo, v, one), jnp.where(lo, one, sw), jnp.where(lo, sw, one), jnp.where(lo, one, v)]


def _nsa_prep_kernel(nq_ref, nkv_ref, cos_ref, sin_ref, qn_ref, ksn_ref, kwn_ref,
                     qb_ref, cache_ref, win_ref, kvb_ref):
    cos = cos_ref[...]
    sin = sin_ref[...]
    bd = _half_mean_mat()
    qn = qn_ref[...]
    for h in range(NSA_HEADS):
        cols = slice(h * 128, (h + 1) * 128)
        x = nq_ref[:, cols]
        ms = jnp.sum(x * x, axis=-1, keepdims=True) * (1.0 / NSA_HD)
        qh = _rope64(x * lax.rsqrt(ms + EPS) * qn, cos, sin)
        qb_ref[:, cols] = _bf(qh * (NSA_HD ** -0.5))
    kc = _rope64(nkv_ref[:, 0:128], cos, sin)
    vc = nkv_ref[:, 128:256]
    ks = _rope64(_rms64(nkv_ref[:, 256:384], ksn_ref[...], bd), cos, sin)
    vs = nkv_ref[:, 384:512]
    kw = _rope64(_rms64(nkv_ref[:, 512:640], kwn_ref[...], bd), cos, sin)
    vw = nkv_ref[:, 640:768]
    cache_ref[:, 0:128] = kc
    cache_ref[:, 128:256] = vc
    cache_ref[:, 256:384] = ks
    cache_ref[:, 384:512] = vs
    win_ref[:, 0:128] = kw
    win_ref[:, 128:256] = vw
    kvb_ref[0] = _bf(ks)
    kvb_ref[1] = _bf(kw)
    for i, v in enumerate(_value_variants(vs)):
        kvb_ref[2 + i] = _bf(v)
    for i, v in enumerate(_value_variants(vw)):
        kvb_ref[6 + i] = _bf(v)


def _nsa_prep(proj, cos, sin, qn, ksn, kwn):
    m = proj.shape[0]
    t = min(m, 256)
    two = lambda w: jnp.concatenate([w, w]).reshape(1, LANES)
    full2 = lambda shape: pl.BlockSpec(shape, lambda i: (0, 0))
    return pl.pallas_call(
        _nsa_prep_kernel,
        grid=(m // t,),
        in_specs=[
            pl.BlockSpec((t, 1024), lambda i: (i, C_NQ // 1024)),
            pl.BlockSpec((t, 768), lambda i: (i, C_NKV // 768)),
            pl.BlockSpec((t, LANES), lambda i: (i, 0)),
            pl.BlockSpec((t, LANES), lambda i: (i, 0)),
            full2((1, LANES)), full2((1, LANES)), full2((1, LANES)),
        ],
        out_specs=[
            pl.BlockSpec((t, 1024), lambda i: (i, 0)),
            pl.BlockSpec((t, 512), lambda i: (i, 0)),
            pl.BlockSpec((t, 256), lambda i: (i, 0)),
            pl.BlockSpec((10, t, LANES), lambda i: (0, i, 0)),
        ],
        out_shape=[
            jax.ShapeDtypeStruct((m, 1024), BF16),
            jax.ShapeDtypeStruct((m, 512), F32),
            jax.ShapeDtypeStruct((m, 256), F32),
            jax.ShapeDtypeStruct((10, m, LANES), BF16),
        ],
        compiler_params=_cparams(("arbitrary",)),
        name="nsa_prep",
    )(proj, proj, cos, sin, two(qn), two(ksn), two(kwn))


def _compress_kernel(rk_ref, rv_ref, w1_ref, pe_ref, w2_ref, kn_ref, kc_ref, vc_ref):
    tr = rk_ref.shape[0]
    bd = _half_mean_mat()

    def mlp(r, j):
        ha = _dot(_bf(r + pe_ref[2 * j:2 * j + 1, :]), w1_ref[2 * j])
        hb = _dot(_bf(r + pe_ref[2 * j + 1:2 * j + 2, :]), w1_ref[2 * j + 1])
        hid = ha + pltpu.roll(hb, tr - 1, 0)
        return _dot(_bf(_silu(hid)), w2_ref[j])

    kc = _rms64(mlp(rk_ref[...], 0), kn_ref[...], bd)
    vc = mlp(rv_ref[...], 1)
    kc_ref[...] = _bf(kc)
    vc_ref[0] = _bf(vc)
    vc_ref[1] = _bf(pltpu.roll(vc, 64, 1))


def _compress_weights(pe, w1, w2):
    w1r = w1.reshape(2, 2, 16, 64, 64)
    z = jnp.zeros_like(w1r)
    top = jnp.concatenate([w1r, z], axis=-1)
    bot = jnp.concatenate([z, w1r], axis=-1)
    w1x = jnp.stack([top, bot], axis=3)
    w1x = w1x.reshape(4, 2048, LANES).astype(BF16)
    per = pe.reshape(2, 2, 16, 1, 64)
    pex = jnp.broadcast_to(per, (2, 2, 16, 2, 64)).reshape(4, 2048)
    z2 = jnp.zeros_like(w2)
    w2x = jnp.concatenate([jnp.concatenate([w2, z2], -1), jnp.concatenate([z2, w2], -1)], axis=1).astype(BF16)
    return w1x, pex, w2x


def _compress(rk, rv, w1x, pex, w2x, kn):
    nr = rk.shape[0]
    tr = min(nr, 512)
    two = jnp.concatenate([kn, kn]).reshape(1, LANES)
    return pl.pallas_call(
        _compress_kernel,
        grid=(nr // tr,),
        in_specs=[
            pl.BlockSpec((tr, 2048), lambda i: (i, 0)),
            pl.BlockSpec((tr, 2048), lambda i: (i, 0)),
            pl.BlockSpec((4, 2048, LANES), lambda i: (0, 0, 0)),
            pl.BlockSpec((4, 2048), lambda i: (0, 0)),
            pl.BlockSpec((2, LANES, LANES), lambda i: (0, 0, 0)),
            pl.BlockSpec((1, LANES), lambda i: (0, 0)),
        ],
        out_specs=[
            pl.BlockSpec((tr, LANES), lambda i: (i, 0)),
            pl.BlockSpec((2, tr, LANES), lambda i: (0, i, 0)),
        ],
        out_shape=[
            jax.ShapeDtypeStruct((nr, LANES), BF16),
            jax.ShapeDtypeStruct((2, nr, LANES), BF16),
        ],
        compiler_params=_cparams(("arbitrary",)),
        name="nsa_compress",
    )(rk, rv, w1x, pex, w2x, two)


QB = 128
SEL_TK = 512
WIN_TK = 128


def _nsa_prompt_kernel(q_ref, kc_ref, vc_ref, kvb_ref, ov_ref, gate_ref, eg_ref, nz_ref, o_ref,
                       m_ref, acc_ref, st_ref):
    g = pl.program_id(0)
    qi = pl.program_id(1)
    t0 = qi * QB
    nc = kc_ref.shape[0]
    q = q_ref[...]
    qs = jnp.concatenate([q[:, 0:128], q[:, 256:384], q[:, 128:256], q[:, 384:512]], axis=0)
    trow = t0 + (lax.broadcasted_iota(jnp.int32, (4 * QB, 1), 0) & (QB - 1))
    tq = t0 + lax.broadcasted_iota(jnp.int32, (QB, 1), 0)

    def pv(pb, ve, vo):
        return jnp.concatenate([_dot(pb[0:2 * QB], ve), _dot(pb[2 * QB:4 * QB], vo)], axis=0)

    sc = _dot_nt(qs, kc_ref[...])
    n_io = lax.broadcasted_iota(jnp.int32, (1, nc), 1)
    mask_c = (n_io * CMP_STRIDE + (CMP_BLOCK - 1)) <= trow
    sc = jnp.where(mask_c, sc, NEG)
    mc = jnp.max(sc, axis=1, keepdims=True)
    pc = jnp.where(mask_c, jnp.exp(sc - mc), 0.0)
    zc = jnp.sum(pc, axis=1, keepdims=True)
    pc = pc / jnp.where(zc > 0, zc, 1.0)
    o_c = pv(_bf(pc), vc_ref[g], vc_ref[1 - g])

    p4 = pc[0:QB] + pc[QB:2 * QB] + pc[2 * QB:3 * QB] + pc[3 * QB:4 * QB]
    imp = _dot_split(p4, ov_ref[...])
    j_io = lax.broadcasted_iota(jnp.int32, (1, LANES), 1)
    cur = tq >> 6
    forced = (j_io == 0) | ((j_io <= cur) & (j_io > cur - N_LOCAL))
    valid = j_io <= cur
    score = jnp.where(forced, FORCE_SCORE, imp)
    score = jnp.where(valid, score, -FORCE_SCORE)
    score_t = score.T
    st_ref[...] = score_t
    jb = lax.broadcasted_iota(jnp.int32, (LANES, QB), 0)
    n_blk = ((t0 + QB - 1) >> 6) + 1

    def rank_body(i, rank):
        row = st_ref[pl.ds(i, 1), :]
        beats = (row > score_t) | ((row == score_t) & (i < jb))
        return rank + jnp.where(beats, 1.0, 0.0)

    rank_t = lax.fori_loop(0, n_blk, rank_body, jnp.zeros((LANES, QB), F32))
    sel = jnp.where((rank_t.T < SLC_TOPN) & valid, 1.0, 0.0).astype(BF16)

    def reset():
        m_ref[...] = jnp.full(m_ref.shape, NEG, F32)
        acc_ref[...] = jnp.zeros(acc_ref.shape, F32)

    def update(kidx, ve, vo, k0, tk, bias, causal):
        s = _dot_nt(qs, kvb_ref[kidx, pl.ds(k0, tk), :])
        if bias is not None:
            s = s + bias
        if causal:
            kpos = k0 + lax.broadcasted_iota(jnp.int32, (1, tk), 1)
            s = jnp.where(kpos <= trow, s, NEG)
        m_old = m_ref[...]
        m_new = jnp.maximum(m_old, jnp.max(s, axis=1, keepdims=True))
        p = _bf(jnp.exp(s - m_new))
        acc_ref[...] = jnp.exp(m_old - m_new) * acc_ref[...] + pv(
            p, kvb_ref[ve, pl.ds(k0, tk), :], kvb_ref[vo, pl.ds(k0, tk), :])
        m_ref[...] = m_new

    def result():
        a = acc_ref[...]
        den = pltpu.roll(a, 64, 1)
        return a / jnp.where(den > 0, den, 1.0)

    jrow = lax.broadcasted_iota(jnp.int32, (LANES, 1), 0)

    def sel_bias(k0):
        kpos = k0 + lax.broadcasted_iota(jnp.int32, (1, SEL_TK), 1)
        et = jnp.where(jrow == (kpos >> 6), 1.0, 0.0).astype(BF16)
        b = (_dot(sel, et) - 1.0) * (-NEG)
        return jnp.concatenate([b, b, b, b], axis=0)

    reset()
    vse = 2 + 2 * g
    n_full = t0 // SEL_TK

    def sel_body(kt, c):
        k0 = pl.multiple_of(kt * SEL_TK, SEL_TK)
        update(0, vse, vse + 1, k0, SEL_TK, sel_bias(k0), False)
        return c

    lax.fori_loop(0, n_full, sel_body, 0)
    k_diag = pl.multiple_of(n_full * SEL_TK, SEL_TK)
    update(0, vse, vse + 1, k_diag, SEL_TK, sel_bias(k_diag), True)
    o_s = result()

    reset()
    vwe = 6 + 2 * g
    n_old = WINDOW // WIN_TK

    @pl.when(qi >= n_old)
    def _():
        k0 = pl.multiple_of((qi - n_old) * WIN_TK, WIN_TK)
        kpos = k0 + lax.broadcasted_iota(jnp.int32, (1, WIN_TK), 1)
        bias = jnp.where(trow - kpos <= WINDOW, 0.0, NEG)
        update(1, vwe, vwe + 1, k0, WIN_TK, bias, False)

    def win_body(kt, c):
        update(1, vwe, vwe + 1, pl.multiple_of(kt * WIN_TK, WIN_TK), WIN_TK, None, False)
        return c

    lax.fori_loop(jnp.maximum(qi - n_old + 1, 0), qi, win_body, 0)
    update(1, vwe, vwe + 1, pl.multiple_of(qi * WIN_TK, WIN_TK), WIN_TK, None, True)
    o_w = result()

    gx = _dot_split(jax.nn.sigmoid(gate_ref[...]), eg_ref[0])
    lane = lax.broadcasted_iota(jnp.int32, (QB, LANES), 1)
    lo_half = lane < 64
    for k in range(2):
        ra = slice(k * QB, (k + 1) * QB)
        rb = slice(2 * QB + k * QB, 2 * QB + (k + 1) * QB)
        cols = slice(k * 128, (k + 1) * 128)
        tile = lambda o: jnp.where(lo_half, o[ra], o[rb])
        o = (gx[:, k * 128:(k + 1) * 128] * tile(o_c)
             + gx[:, 256 + k * 128:256 + (k + 1) * 128] * tile(o_s)
             + gx[:, 512 + k * 128:512 + (k + 1) * 128] * tile(o_w))
        o_ref[:, cols] = o * _silu(nz_ref[:, cols])


def _overlap_mat(nc):
    n = np.arange(nc)[:, None]
    j = np.arange(LANES)[None, :]
    ov = ((n * CMP_STRIDE < (j + 1) * SLC_BLOCK) & (n * CMP_STRIDE + CMP_BLOCK - 1 >= j * SLC_BLOCK))
    return jnp.asarray(ov.astype(np.float32), BF16)


def _gate_expand_mat():
    eg = np.zeros((NSA_KV_HEADS, LANES, 3 * 256), np.float32)
    for g in range(NSA_KV_HEADS):
        for hh in range(4):
            for c in range(3):
                eg[g, (g * 4 + hh) * 3 + c, c * 256 + hh * 64:c * 256 + (hh + 1) * 64] = 1.0
    return jnp.asarray(eg, BF16)


def _nsa_prompt(qb, kc, vc2, kvb, proj):
    l = qb.shape[0]
    nc = kc.shape[0]
    assert l // SLC_BLOCK <= LANES and l % SEL_TK == 0
    return pl.pallas_call(
        _nsa_prompt_kernel,
        grid=(NSA_KV_HEADS, l // QB),
        in_specs=[
            pl.BlockSpec((QB, 512), lambda g, i: (i, g)),
            pl.BlockSpec((nc, LANES), lambda g, i: (0, 0)),
            pl.BlockSpec((2, nc, LANES), lambda g, i: (0, 0, 0)),
            pl.BlockSpec((10, l, LANES), lambda g, i: (0, 0, 0)),
            pl.BlockSpec((nc, LANES), lambda g, i: (0, 0)),
            pl.BlockSpec((QB, LANES), lambda g, i: (i, C_GATE // LANES)),
            pl.BlockSpec((1, LANES, 768), lambda g, i: (g, 0, 0)),
            pl.BlockSpec((QB, 256), lambda g, i: (i, C_NZ // 256 + g)),
        ],
        out_specs=pl.BlockSpec((QB, 256), lambda g, i: (i, g)),
        out_shape=jax.ShapeDtypeStruct((l, NSA_WIDTH), F32),
        scratch_shapes=[
            pltpu.VMEM((4 * QB, 1), F32),
            pltpu.VMEM((4 * QB, LANES), F32),
            pltpu.VMEM((LANES, QB), F32),
        ],
        compiler_params=_cparams(("arbitrary", "arbitrary")),
        name="nsa_prompt",
    )(qb, kc, vc2, kvb, _overlap_mat(nc), proj, _gate_expand_mat(), proj)


def _softmax_with_new(s, valid, s_new):
    s = jnp.where(valid, s, NEG)
    m = jnp.maximum(jnp.max(s, axis=1, keepdims=True), s_new)
    p = jnp.where(valid, jnp.exp(s - m), 0.0)
    pn = jnp.exp(s_new - m)
    z = jnp.sum(p, axis=1, keepdims=True) + pn
    return p / z, pn / z


def _nsa_decode_kernel(q_ref, kc_ref, vc_ref, ks_ref, vs_ref, crow_ref, win_ref, wrow_ref, gate_ref, nz_ref,
                       w1_ref, pe_ref, w2_ref, kn_ref, ov_ref, ek_ref, o_ref, wo_ref, *, qpos):
    t = kc_ref.shape[1]
    nr = t // CMP_STRIDE
    wb = win_ref.shape[1]
    n_cmp = (t + 1 - CMP_BLOCK) // CMP_STRIDE + 1
    q8 = q_ref[0]
    q8f = q8.astype(F32)
    bd = _half_mean_mat()

    def mlp(src_ref, j):
        acc_a = jnp.zeros((nr, LANES), F32)
        acc_b = jnp.zeros((nr, LANES), F32)
        for l2 in range(CMP_STRIDE // 2):
            r = jnp.concatenate([src_ref[0, pl.ds(2 * l2, nr, stride=CMP_STRIDE), :],
                                 src_ref[0, pl.ds(2 * l2 + 1, nr, stride=CMP_STRIDE), :]], axis=1)
            cols = slice(l2 * 256, (l2 + 1) * 256)
            acc_a = acc_a + _dot(_bf(r + pe_ref[2 * j:2 * j + 1, cols]), w1_ref[2 * j, cols, :])
            acc_b = acc_b + _dot(_bf(r + pe_ref[2 * j + 1:2 * j + 2, cols]), w1_ref[2 * j + 1, cols, :])
        hid = acc_a + pltpu.roll(acc_b, nr - 1, 0)
        return _dot(_bf(_silu(hid)), w2_ref[j])

    kcc = _rms64(mlp(kc_ref, 0), kn_ref[...], bd)
    vcc = mlp(vc_ref, 1)

    n_io = lax.broadcasted_iota(jnp.int32, (1, nr), 1)
    mask_c = ((n_io * CMP_STRIDE + (CMP_BLOCK - 1)) <= qpos) & (n_io < n_cmp)
    sc = jnp.where(mask_c, _dot_nt(q8, _bf(kcc)), NEG)
    mc = jnp.max(sc, axis=1, keepdims=True)
    pc = jnp.where(mask_c, jnp.exp(sc - mc), 0.0)
    zc = jnp.sum(pc, axis=1, keepdims=True)
    pc = pc / jnp.where(zc > 0, zc, 1.0)
    o_c = _dot(_bf(pc), _bf(vcc))

    r8 = lax.broadcasted_iota(jnp.int32, (8, 8), 0)
    c8 = lax.broadcasted_iota(jnp.int32, (8, 8), 1)
    gsum = jnp.where((r8 >> 2) == (c8 >> 2), 1.0, 0.0).astype(BF16)
    imp = _dot_split(_dot_tri(gsum, pc), ov_ref[...])
    j_io = lax.broadcasted_iota(jnp.int32, (1, LANES), 1)
    cur = qpos // SLC_BLOCK
    forced = (j_io == 0) | ((j_io <= cur) & (j_io > cur - N_LOCAL))
    valid = j_io <= cur
    score = jnp.where(forced, FORCE_SCORE, imp)
    score = jnp.where(valid, score, -FORCE_SCORE)
    rank = jnp.zeros((8, LANES), F32)
    for i in range(cur + 1):
        col = score[:, i:i + 1]
        rank = rank + jnp.where((col > score) | ((col == score) & (i < j_io)), 1.0, 0.0)
    sel = jnp.where((rank < SLC_TOPN) & valid, 1.0, 0.0)

    crow = crow_ref[0]
    ks_new = _bf(crow[:, 256:384]).astype(F32)
    vs_new = _bf(crow[:, 384:512]).astype(F32)
    sel_past = _dot(_bf(sel), ek_ref[...]) > 0.5
    s_s = _dot_nt(q8, _bf(ks_ref[0]))
    s_new = jnp.where(sel[:, cur:cur + 1] > 0.5, jnp.sum(q8f * ks_new, axis=-1, keepdims=True), NEG)
    p_s, pn_s = _softmax_with_new(s_s, sel_past, s_new)
    o_s = _dot(_bf(p_s), _bf(vs_ref[0])) + _bf(pn_s).astype(F32) * vs_new

    wrow = wrow_ref[0]
    kw_new = _bf(wrow[:, 0:128]).astype(F32)
    vw_new = _bf(wrow[:, 128:256]).astype(F32)
    i_io = lax.broadcasted_iota(jnp.int32, (1, wb), 1)
    valid_w = ((wb - i_io) <= WINDOW) & ((qpos - wb + i_io) >= 0)
    s_w = _dot_nt(q8, _bf(win_ref[0, :, 0:128]))
    p_w, pn_w = _softmax_with_new(s_w, valid_w, jnp.sum(q8f * kw_new, axis=-1, keepdims=True))
    o_w = _dot(_bf(p_w), _bf(win_ref[0, :, 128:256])) + _bf(pn_w).astype(F32) * vw_new

    sig = jnp.broadcast_to(jax.nn.sigmoid(gate_ref[0]), (8, LANES))
    lane8 = lax.broadcasted_iota(jnp.int32, (8, LANES), 1)
    row8 = lax.broadcasted_iota(jnp.int32, (8, LANES), 0)
    gate = lambda c: jnp.sum(jnp.where(lane8 == row8 * 3 + c, sig, 0.0), axis=-1, keepdims=True)
    o8 = gate(0) * o_c + gate(1) * o_s + gate(2) * o_w

    sw = pltpu.roll(o8, 64, 1)
    lo = lax.broadcasted_iota(jnp.int32, (1, LANES), 1) < 64
    flat = jnp.concatenate([
        jnp.where(lo, o8[0:1], sw[1:2]), jnp.where(lo, o8[2:3], sw[3:4]),
        jnp.where(lo, sw[4:5], o8[5:6]), jnp.where(lo, sw[6:7], o8[7:8])], axis=1)
    o_ref[0] = flat * _silu(nz_ref[0])

    wo_ref[0, 0:wb - 1, :] = win_ref[0, 1:wb, :]
    wo_ref[0, wb - 1:wb, :] = wrow


def _nsa_decode(q8, past, crow, win, wrow, gate, nz, w1x, pex, w2x, kn, qpos):
    db, t, _ = past.shape
    wb = win.shape[1]
    nr = t // CMP_STRIDE
    assert nr == LANES and t % SLC_BLOCK == 0 and qpos // SLC_BLOCK < LANES
    ek = (np.arange(LANES)[:, None] == (np.arange(t)[None, :] // SLC_BLOCK))
    ek = jnp.asarray(ek.astype(np.float32), BF16)
    ov = _overlap_mat(nr)
    two = jnp.concatenate([kn, kn]).reshape(1, LANES)
    row3 = lambda w: pl.BlockSpec((1, 1, w), lambda b: (b, 0, 0))
    full = lambda a: pl.BlockSpec(a.shape, lambda b: (0,) * a.ndim)
    return pl.pallas_call(
        functools.partial(_nsa_decode_kernel, qpos=qpos),
        grid=(db,),
        in_specs=[
            pl.BlockSpec((1, 8, LANES), lambda b: (b, 0, 0)),
            pl.BlockSpec((1, t, LANES), lambda b: (b, 0, 0)),
            pl.BlockSpec((1, t, LANES), lambda b: (b, 0, 1)),
            pl.BlockSpec((1, t, LANES), lambda b: (b, 0, 2)),
            pl.BlockSpec((1, t, LANES), lambda b: (b, 0, 3)),
            row3(512),
            pl.BlockSpec((1, wb, 256), lambda b: (b, 0, 0)),
            row3(256), row3(LANES), row3(512),
            full(w1x), full(pex), full(w2x), full(two), full(ov), full(ek),
        ],
        out_specs=[row3(512), pl.BlockSpec((1, wb, 256), lambda b: (b, 0, 0))],
        out_shape=[jax.ShapeDtypeStruct((db, 1, NSA_WIDTH), F32), jax.ShapeDtypeStruct((db, wb, 256), F32)],
        compiler_params=_cparams(("arbitrary",)),
        name="nsa_decode",
    )(q8, past, past, past, past, crow, win, wrow, gate, nz, w1x, pex, w2x, two, ov, ek)


def _mem_kv_kernel(mem_ref, nw_ref, wk_ref, wv_ref, kn_ref, kv_ref, kvb_ref):
    x = mem_ref[...]
    m = _bf(x * lax.rsqrt(jnp.mean(x * x, axis=-1, keepdims=True) + EPS) * nw_ref[...])
    k = _dot(m, wk_ref[...])
    v = _dot(m, wv_ref[...])
    for h in range(MEM_HEADS):
        cols = slice(h * 128, (h + 1) * 128)
        kh = k[:, cols]
        kh = kh * lax.rsqrt(jnp.mean(kh * kh, axis=-1, keepdims=True) + EPS) * kn_ref[...]
        kv_ref[:, cols] = kh
        kvb_ref[:, cols] = _bf(kh)
    kv_ref[:, MEM_WIDTH:2 * MEM_WIDTH] = v
    kvb_ref[:, MEM_WIDTH:2 * MEM_WIDTH] = _bf(v)


def _mem_kv(mem, nw, wk, wv, kn):
    ml = mem.shape[0]
    return pl.pallas_call(
        _mem_kv_kernel,
        out_shape=[jax.ShapeDtypeStruct((ml, 2 * MEM_WIDTH), F32),
                   jax.ShapeDtypeStruct((ml, 2 * MEM_WIDTH), BF16)],
        compiler_params=pltpu.CompilerParams(vmem_limit_bytes=VMEM_LIMIT),
        name="mem_kv",
    )(mem, nw.reshape(1, -1), wk, wv, kn.reshape(1, -1))


TAIL_T = 256


def _out_proj(x_ref, ret_ref, ssd_ref, nsa_ref, wout_ref):
    return (x_ref[...] + _dot(_bf(ret_ref[...]), wout_ref[0:512, :])
            + _dot(_bf(ssd_ref[...]), wout_ref[512:1536, :])
            + _dot(_bf(nsa_ref[...]), wout_ref[1536:2048, :]))


def _cross_q(x1, ncw_ref, wq_ref, qn_ref):
    h = _bf(x1 * lax.rsqrt(jnp.mean(x1 * x1, axis=-1, keepdims=True) + EPS) * ncw_ref[...])
    q = _dot(h, wq_ref[...])
    out = []
    for hd in range(MEM_HEADS):
        qh = q[:, hd * 128:(hd + 1) * 128]
        out.append(_bf(qh * lax.rsqrt(jnp.mean(qh * qh, axis=-1, keepdims=True) + EPS) * qn_ref[...]))
    return out


def _tail_kernel(x_ref, ret_ref, ssd_ref, nsa_ref, wout_ref, ncw_ref, wq_ref, qn_ref, kvb_ref, wo_ref, y_ref):
    x1 = _out_proj(x_ref, ret_ref, ssd_ref, nsa_ref, wout_ref)
    outs = []
    for hd, qh in enumerate(_cross_q(x1, ncw_ref, wq_ref, qn_ref)):
        s = _dot_nt(qh, kvb_ref[:, hd * 128:(hd + 1) * 128]) * (MEM_HD ** -0.5)
        s = s - jnp.max(s, axis=-1, keepdims=True)
        p = jnp.exp(s)
        p = p / jnp.sum(p, axis=-1, keepdims=True)
        outs.append(_dot(_bf(p), kvb_ref[:, MEM_WIDTH + hd * 128:MEM_WIDTH + (hd + 1) * 128]))
    y_ref[...] = x1 + _dot(_bf(jnp.concatenate(outs, axis=1)), wo_ref[...])


def _tail(x, ret, ssd, nsa, wout, ncw, wq, qn, kvb, wo):
    m = x.shape[0]
    t = min(m, TAIL_T)
    ml = kvb.shape[0]
    full2 = lambda shape: pl.BlockSpec(shape, lambda i: (0, 0))
    return pl.pallas_call(
        _tail_kernel,
        grid=(m // t,),
        in_specs=[
            pl.BlockSpec((t, D_MODEL), lambda i: (i, 0)),
            pl.BlockSpec((t, RET_WIDTH), lambda i: (i, 0)),
            pl.BlockSpec((t, SSD_WIDTH), lambda i: (i, 0)),
            pl.BlockSpec((t, NSA_WIDTH), lambda i: (i, 0)),
            full2((D_MODEL, D_MODEL)),
            full2((1, D_MODEL)),
            full2((D_MODEL, MEM_WIDTH)),
            full2((1, MEM_HD)),
            full2((ml, 2 * MEM_WIDTH)),
            full2((MEM_WIDTH, D_MODEL)),
        ],
        out_specs=pl.BlockSpec((t, D_MODEL), lambda i: (i, 0)),
        out_shape=jax.ShapeDtypeStruct((m, D_MODEL), F32),
        compiler_params=_cparams(("arbitrary",)),
        name="layer_tail",
    )(x, ret, ssd, nsa, wout, ncw.reshape(1, -1), wq, qn.reshape(1, -1), kvb, wo)


def _dec_tail_a_kernel(x_ref, ret_ref, ssd_ref, nsa_ref, wout_ref, ncw_ref, wq_ref, qn_ref, x1_ref, q_ref):
    x1 = _out_proj(x_ref, ret_ref, ssd_ref, nsa_ref, wout_ref)
    x1_ref[...] = x1
    for hd, qh in enumerate(_cross_q(x1, ncw_ref, wq_ref, qn_ref)):
        q_ref[:, hd * 128:(hd + 1) * 128] = qh


def _dec_tail_a(x, ret, ssd, nsa, wout, ncw, wq, qn):
    m = x.shape[0]
    return pl.pallas_call(
        _dec_tail_a_kernel,
        out_shape=[jax.ShapeDtypeStruct((m, D_MODEL), F32), jax.ShapeDtypeStruct((m, MEM_WIDTH), BF16)],
        compiler_params=pltpu.CompilerParams(vmem_limit_bytes=VMEM_LIMIT),
        name="dec_tail_a",
    )(x, ret, ssd, nsa, wout, ncw.reshape(1, -1), wq, qn.reshape(1, -1))


def _dec_xattn_kernel(q_ref, mem_ref, x1_ref, wo_ref, y_ref):
    lane = lax.broadcasted_iota(jnp.int32, (8, MEM_WIDTH), 1)
    row = lax.broadcasted_iota(jnp.int32, (8, MEM_WIDTH), 0)
    own = (lane >> 7) == row
    outs = []
    for b in range(DEC_BB):
        q4 = _bf(jnp.where(own, jnp.broadcast_to(q_ref[b:b + 1, :].astype(F32), (8, MEM_WIDTH)), 0.0))
        s = _dot_nt(q4, _bf(mem_ref[b, :, 0:MEM_WIDTH])) * (MEM_HD ** -0.5)
        s = s - jnp.max(s, axis=-1, keepdims=True)
        p = jnp.exp(s)
        p = p / jnp.sum(p, axis=-1, keepdims=True)
        o4 = _dot(_bf(p), _bf(mem_ref[b, :, MEM_WIDTH:2 * MEM_WIDTH]))
        outs.append(jnp.sum(jnp.where(own, o4, 0.0), axis=0, keepdims=True))
    y_ref[...] = x1_ref[...] + _dot(_bf(jnp.concatenate(outs, axis=0)), wo_ref[...])


def _dec_xattn(q, mem, x1, wo):
    db = q.shape[0]
    ml = mem.shape[1]
    return pl.pallas_call(
        _dec_xattn_kernel,
        grid=(db // DEC_BB,),
        in_specs=[
            pl.BlockSpec((DEC_BB, MEM_WIDTH), lambda i: (i, 0)),
            pl.BlockSpec((DEC_BB, ml, 2 * MEM_WIDTH), lambda i: (i, 0, 0)),
            pl.BlockSpec((DEC_BB, D_MODEL), lambda i: (i, 0)),
            pl.BlockSpec((MEM_WIDTH, D_MODEL), lambda i: (0, 0)),
        ],
        out_specs=pl.BlockSpec((DEC_BB, D_MODEL), lambda i: (i, 0)),
        out_shape=jax.ShapeDtypeStruct((db, D_MODEL), F32),
        compiler_params=_cparams(("arbitrary",)),
        name="dec_xattn",
    )(q, mem, x1, wo)


def _prep_w_in(w):
    k = w.shape[0]
    nq = w[:, 4624:5136].reshape(k, NSA_HEADS, NSA_HD)
    z = jnp.zeros_like(nq)
    nq_pad = jnp.concatenate([
        jnp.concatenate([nq[:, :4], z[:, :4]], axis=-1),
        jnp.concatenate([z[:, 4:], nq[:, 4:]], axis=-1)], axis=1).reshape(k, NSA_HEADS * LANES)
    padc = lambda a: jnp.pad(a, ((0, 0), (0, LANES - a.shape[1])))
    return jnp.concatenate([
        w[:, 0:4608],
        w[:, 5928:6440],
        nq_pad,
        w[:, 5136:5904],
        padc(w[:, 4608:4624]),
        padc(w[:, 5904:5928]),
    ], axis=1).astype(BF16)


def _rope_tables(pos, head_dim, rows):
    half = head_dim // 2
    inv = jnp.exp(-math.log(ROPE_THETA) * jnp.arange(half, dtype=F32) / half)
    ang = pos.astype(F32)[:, None] * inv[None, :]
    cos = jnp.cos(ang)
    sin = jnp.sin(ang)
    reps = LANES // head_dim
    cos_t = jnp.tile(jnp.concatenate([cos, cos], axis=-1), (1, reps))
    sin_t = jnp.tile(jnp.concatenate([-sin, sin], axis=-1), (1, reps))
    if cos_t.shape[0] != rows:
        cos_t = jnp.broadcast_to(cos_t, (rows, LANES))
        sin_t = jnp.broadcast_to(sin_t, (rows, LANES))
    return cos_t, sin_t


def kernel(x_prompt, x_sample, mem_prompt, state_ret, state_ssm, state_conv, cache_nsa_kv, cache_win_kv,
           cache_mem_kv, page_table, norm_mix, w_in, ssd_conv_w, ssd_conv_b, ssd_dt_bias, ssd_a_log, ssd_d,
           ssd_norm, nsa_q_norm, nsa_kc_norm, nsa_ks_norm, nsa_kw_norm, nsa_cmp_pe, nsa_cmp_w1, nsa_cmp_w2,
           w_out, norm_cross, norm_mem, mem_wq, mem_wk, mem_wv, mem_q_norm, mem_k_norm, mem_wo):
    b, l, _ = x_prompt.shape
    assert b == 1
    db, dl, _ = x_sample.shape
    assert dl == 1 and db % DEC_BB == 0
    depth = w_in.shape[0]
    n_pages = page_table.shape[1]
    page = cache_nsa_kv.shape[2]
    past_len = n_pages * page
    wbuf = cache_win_kv.shape[2]
    ml = cache_mem_kv.shape[2]
    wp = min(WINDOW, l)
    pos_p = jnp.arange(l, dtype=jnp.int32)
    pos_s = jnp.full((1,), past_len, dtype=jnp.int32)
    cos128, sin128 = _rope_tables(pos_p, RET_DK, l)
    cos64, sin64 = _rope_tables(pos_p, NSA_HD, l)
    cos128s, sin128s = _rope_tables(pos_s, RET_DK, db)
    cos64s, sin64s = _rope_tables(pos_s, NSA_HD, db)

    xp = x_prompt[0]
    xs = x_sample[:, 0, :]
    outs = [[] for _ in range(11)]
    for li in range(depth):
        w_in_p = _prep_w_in(w_in[li])
        w_out_b = w_out[li].astype(BF16)
        wq_b = mem_wq[li].astype(BF16)
        wo_b = mem_wo[li].astype(BF16)
        w1x, pex, w2x = _compress_weights(nsa_cmp_pe[li], nsa_cmp_w1[li], nsa_cmp_w2[li])
        ssd_w = (ssd_conv_w[li], ssd_conv_b[li], ssd_dt_bias[li], ssd_a_log[li], ssd_d[li], ssd_norm[li])
        nsa_n = (nsa_q_norm[li], nsa_ks_norm[li], nsa_kw_norm[li])
        proj = _inproj(xp, norm_mix[li], w_in_p)
        ret_out, ret_s = _ret_prompt(proj, cos128, sin128)
        ssd_out, ssm_h, conv8 = _ssd_prompt(proj, *ssd_w)
        qb, cache, win, kvb = _nsa_prep(proj, cos64, sin64, *nsa_n)
        rk = cache[:, 0:128].reshape(l // CMP_STRIDE, CMP_STRIDE * LANES)
        rv = cache[:, 128:256].reshape(l // CMP_STRIDE, CMP_STRIDE * LANES)
        kc_b, vc2 = _compress(rk, rv, w1x, pex, w2x, nsa_kc_norm[li])
        nsa_out = _nsa_prompt(qb, kc_b, vc2, kvb, proj)
        mkv, mkv_b = _mem_kv(mem_prompt[0], norm_mem[li], mem_wk[li].astype(BF16), mem_wv[li].astype(BF16),
                             mem_k_norm[li])
        xp = _tail(xp, ret_out, ssd_out, nsa_out, w_out_b, norm_cross[li], wq_b, mem_q_norm[li], mkv_b, wo_b)
        outs[0].append(ret_s[None])
        outs[2].append(ssm_h[None])
        outs[4].append(conv8[None, 8 - (SSD_CONV - 1):])
        outs[6].append(cache.reshape(1, l, 4, NSA_KV_HEADS, NSA_HD))
        outs[8].append(win.reshape(1, l, 2, NSA_KV_HEADS, NSA_HD)[:, l - wp:])
        outs[10].append(mkv.reshape(1, -1, 2, MEM_HEADS, MEM_HD))
        sproj = _inproj(xs, norm_mix[li], w_in_p)
        s_ret_out, s_ret = _ret_decode(sproj, cos128s, sin128s, state_ret[li])
        s_ssd_out, s_conv, s_ssm = _ssd_decode(sproj, state_conv[li], state_ssm[li], *ssd_w)
        s_qb, s_cache, s_wrow, _ = _nsa_prep(sproj, cos64s, sin64s, *nsa_n)
        past = cache_nsa_kv[li][page_table].reshape(db, past_len, 4 * LANES)
        s_nsa, s_win = _nsa_decode(
            s_qb.reshape(db, NSA_HEADS, LANES), past, s_cache[:, None, :],
            cache_win_kv[li].reshape(db, wbuf, 2 * LANES), s_wrow[:, None, :],
            sproj[:, None, C_GATE:C_GATE + LANES], sproj[:, None, C_NZ:C_NZ + NSA_WIDTH],
            w1x, pex, w2x, nsa_kc_norm[li], past_len)
        x1, s_q = _dec_tail_a(xs, s_ret_out, s_ssd_out, s_nsa[:, 0, :], w_out_b, norm_cross[li], wq_b,
                              mem_q_norm[li])
        xs = _dec_xattn(s_q, cache_mem_kv[li].reshape(db, ml, 2 * MEM_WIDTH), x1, wo_b)
        outs[1].append(s_ret)
        outs[3].append(s_ssm)
        outs[5].append(s_conv)
        outs[7].append(s_cache.reshape(db, 1, 4, NSA_KV_HEADS, NSA_HD))
        outs[9].append(s_win.reshape(db, wbuf, 2, NSA_KV_HEADS, NSA_HD))
    st = [jnp.stack(o) for o in outs]
    return (xp[None], xs[:, None, :], st[0], st[1], st[2], st[3], st[4], st[5], st[6], st[7], st[8], st[9], st[10])
```

```python
import functools
import math

import numpy as np
import jax
import jax.numpy as jnp
from jax import lax
from jax.experimental import pallas as pl
from jax.experimental.pallas import tpu as pltpu

F32 = jnp.float32
BF16 = jnp.bfloat16

D_MODEL = 2048
RET_HEADS = 4
RET_DK = 128
RET_WIDTH = 512
CHUNK = 128
SSD_WIDTH = 1024
SSD_HEADDIM = 64
SSD_HEADS = 16
SSD_GROUPS = 2
SSD_STATE = 128
SSD_CONV = 4
SSD_CONV_DIM = 1536
NSA_WIDTH = 512
NSA_HEADS = 8
NSA_HD = 64
NSA_KV_HEADS = 2
CMP_BLOCK = 32
CMP_STRIDE = 16
SLC_BLOCK = 64
SLC_TOPN = 16
N_LOCAL = 2
WINDOW = 512
FORCE_SCORE = 1.0e6
MEM_HEADS = 4
MEM_HD = 128
MEM_WIDTH = 512
ROPE_THETA = 10000.0
EPS = 1e-6
NEG = -1.0e30

C_SZ = 2048
C_XBC = 3072
C_NZ = 4608
C_NQ = 5120
C_NKV = 6144
C_SDT = 6912
C_GATE = 7040
N_PROJ = 7168

LANES = 128
VMEM_LIMIT = 56 * 1024 * 1024


def _cparams(sem):
    return pltpu.CompilerParams(dimension_semantics=sem, vmem_limit_bytes=VMEM_LIMIT)


def _bf(x):
    return x.astype(BF16)


def _dot(a, b):
    return jnp.dot(a, b, preferred_element_type=F32)


def _dot_nt(a, b):
    return lax.dot_general(a, b, (((1,), (1,)), ((), ())), preferred_element_type=F32)


def _dot_tn(a, b):
    return lax.dot_general(a, b, (((0,), (0,)), ((), ())), preferred_element_type=F32)


def _split3(a):
    hi = a.astype(BF16)
    r = a - hi.astype(F32)
    mid = r.astype(BF16)
    lo = (r - mid.astype(F32)).astype(BF16)
    return hi, mid, lo


def _split2(a):
    hi = a.astype(BF16)
    lo = (a - hi.astype(F32)).astype(BF16)
    return hi, lo


def _dot_split(a, b_bf16):
    hi, mid, lo = _split3(a)
    return _dot(hi, b_bf16) + _dot(mid, b_bf16) + _dot(lo, b_bf16)


def _dot_tri(tri_bf16, a):
    hi, mid, lo = _split3(a)
    return _dot(tri_bf16, hi) + _dot(tri_bf16, mid) + _dot(tri_bf16, lo)


def _silu(x):
    return x * jax.nn.sigmoid(x)


def _softplus(x):
    return jnp.maximum(x, 0.0) + jnp.log1p(jnp.exp(-jnp.abs(x)))


def _rope128(x, cos, sin):
    return x * cos + pltpu.roll(x, 64, 1) * sin


def _rope64(x, cos, sin):
    lane = lax.broadcasted_iota(jnp.int32, x.shape, 1)
    first = (lane & 63) < 32
    partner = jnp.where(first, pltpu.roll(x, 96, 1), pltpu.roll(x, 32, 1))
    return x * cos + partner * sin


def _half_mean_mat():
    r = lax.broadcasted_iota(jnp.int32, (LANES, LANES), 0)
    c = lax.broadcasted_iota(jnp.int32, (LANES, LANES), 1)
    return jnp.where((r >> 6) == (c >> 6), 1.0 / 64.0, 0.0).astype(BF16)


def _rms64(x, w, bd):
    ms = _dot_split(x * x, bd)
    return x * lax.rsqrt(ms + EPS) * w


def _pad_lanes(v, n=LANES):
    v = v.reshape(1, -1)
    return jnp.pad(v, ((0, 0), (0, n - v.shape[1])))


def _layer_out(prev):
    if prev is None:
        return [], [], None
    return [prev], [pl.BlockSpec(memory_space=pl.ANY)], prev


def _layer_spec(prev, depth, li, block, index):
    if prev is None:
        return pl.BlockSpec((depth,) + block, lambda *a: (0,) + index(*a))
    return pl.BlockSpec((None,) + block, lambda *a: (li,) + index(*a))


def _layer_view(ref, li, fill_depth):
    if not fill_depth:
        return ref
    for d in range(fill_depth):
        if d != li:
            ref[d] = jnp.zeros(ref.shape[1:], ref.dtype)
    return ref.at[li]


def _row_only(x, b):
    rid = lax.broadcasted_iota(jnp.int32, (x.shape[0], 1), 0)
    return jnp.where(rid == b, x, jnp.zeros_like(x))


N_MAIN = 4608
PROJ_TN = 512


def _inproj_kernel(x_ref, nw_ref, wa_ref, wb_ref, o_ref, h_ref):
    j = pl.program_id(1)

    @pl.when(j == 0)
    def _():
        nw = nw_ref[...]
        rows = min(128, x_ref.shape[0])

        def body(i, c):
            r = pl.ds(pl.multiple_of(i * rows, rows), rows)
            x = x_ref[r, :]
            ms = jnp.mean(x * x, axis=-1, keepdims=True)
            h_ref[r, :] = (x * lax.rsqrt(ms + EPS) * nw).astype(BF16)
            return c

        lax.fori_loop(0, x_ref.shape[0] // rows, body, 0)

    @pl.when(j < N_MAIN // PROJ_TN)
    def _():
        o_ref[...] = _dot_nt(h_ref[...], _bf(wa_ref[...]))

    @pl.when(j >= N_MAIN // PROJ_TN)
    def _():
        o_ref[...] = _dot_nt(h_ref[...], _bf(wb_ref[...]))


def _inproj(x, nw, w_t, w_tail, li):
    m = x.shape[0]
    tm = min(m, 1024)
    nja = N_MAIN // PROJ_TN
    return pl.pallas_call(
        _inproj_kernel,
        grid=(m // tm, N_PROJ // PROJ_TN),
        in_specs=[
            pl.BlockSpec((tm, D_MODEL), lambda i, j: (i, 0)),
            pl.BlockSpec((1, D_MODEL), lambda i, j: (0, 0)),
            pl.BlockSpec((None, PROJ_TN, D_MODEL), lambda i, j: (li, jnp.minimum(j, nja - 1), 0)),
            pl.BlockSpec((PROJ_TN, D_MODEL), lambda i, j: (jnp.maximum(j - nja, 0), 0)),
        ],
        out_specs=pl.BlockSpec((tm, PROJ_TN), lambda i, j: (i, j)),
        out_shape=jax.ShapeDtypeStruct((m, N_PROJ), F32),
        scratch_shapes=[pltpu.VMEM((tm, D_MODEL), BF16)],
        compiler_params=_cparams(("arbitrary", "arbitrary")),
        name="inproj",
    )(x, nw.reshape(1, D_MODEL), w_t, w_tail)


RET_T = 512


def _ret_prompt_kernel(q_ref, k_ref, v_ref, g_ref, cos_ref, sin_ref, dec_ref, qd_ref, kd_ref, cd_ref,
                       o_ref, so_ref, s_ref):
    i = pl.program_id(0)

    @pl.when(i == 0)
    def _():
        s_ref[...] = jnp.zeros(s_ref.shape, F32)

    for c in range(RET_T // CHUNK):
        rows = slice(c * CHUNK, (c + 1) * CHUNK)
        cos = cos_ref[rows, :]
        sin = sin_ref[rows, :]
        for h in range(RET_HEADS):
            cols = slice(h * 128, (h + 1) * 128)
            q = _rope128(q_ref[rows, cols], cos, sin)
            k = _rope128(k_ref[rows, cols], cos, sin) * (RET_DK ** -0.5)
            v = v_ref[rows, cols]
            s = s_ref[h]
            qb = _bf(q)
            vb = _bf(v)
            att = _dot_nt(qb, _bf(k)) * dec_ref[h]
            o = _dot(_bf(att), vb) + _dot(qb, _bf(s)) * qd_ref[h]
            s_ref[h] = s * cd_ref[h] + _dot_tn(_bf(k * kd_ref[h]), vb)
            r = o * lax.rsqrt(jnp.mean(o * o, axis=-1, keepdims=True) + EPS)
            o_ref[rows, cols] = r * _silu(g_ref[rows, cols])

    @pl.when(i == pl.num_programs(0) - 1)
    def _():
        so_ref[...] = s_ref[...]


def _ret_gamma():
    return 1.0 - np.exp2(-5.0 - np.arange(RET_HEADS, dtype=np.float64))


def _ret_consts():
    lg = np.log(_ret_gamma())
    idx = np.arange(CHUNK, dtype=np.float64)
    diff = idx[:, None] - idx[None, :]
    dec = np.where(diff[None] >= 0, np.exp(lg[:, None, None] * np.maximum(diff, 0.0)[None]), 0.0)
    qd = np.exp(lg[:, None] * (idx + 1.0)[None])
    kd = np.exp(lg[:, None] * (CHUNK - 1.0 - idx)[None])
    cd = np.exp(lg * CHUNK)
    bc = lambda a: np.ascontiguousarray(np.broadcast_to(a[:, :, None], (RET_HEADS, CHUNK, LANES)))
    return (jnp.asarray(dec, F32), jnp.asarray(bc(qd), F32), jnp.asarray(bc(kd), F32),
            jnp.asarray(np.broadcast_to(cd[:, None, None], (RET_HEADS, 1, LANES)).copy(), F32))


def _ret_prompt(proj, cos, sin):
    l = proj.shape[0]
    dec, qd, kd, cd = _ret_consts()
    full3 = lambda shape: pl.BlockSpec(shape, lambda i: (0, 0, 0))
    return pl.pallas_call(
        _ret_prompt_kernel,
        grid=(l // RET_T,),
        in_specs=[
            pl.BlockSpec((RET_T, 512), lambda i: (i, 0)),
            pl.BlockSpec((RET_T, 512), lambda i: (i, 1)),
            pl.BlockSpec((RET_T, 512), lambda i: (i, 2)),
            pl.BlockSpec((RET_T, 512), lambda i: (i, 3)),
            pl.BlockSpec((RET_T, LANES), lambda i: (i, 0)),
            pl.BlockSpec((RET_T, LANES), lambda i: (i, 0)),
            full3((RET_HEADS, CHUNK, CHUNK)),
            full3((RET_HEADS, CHUNK, LANES)),
            full3((RET_HEADS, CHUNK, LANES)),
            full3((RET_HEADS, 1, LANES)),
        ],
        out_specs=[
            pl.BlockSpec((RET_T, 512), lambda i: (i, 0)),
            full3((RET_HEADS, RET_DK, RET_DK)),
        ],
        out_shape=[
            jax.ShapeDtypeStruct((l, RET_WIDTH), F32),
            jax.ShapeDtypeStruct((RET_HEADS, RET_DK, RET_DK), F32),
        ],
        scratch_shapes=[pltpu.VMEM((RET_HEADS, RET_DK, RET_DK), F32)],
        compiler_params=_cparams(("arbitrary",)),
        name="ret_prompt",
    )(proj, proj, proj, proj, cos, sin, dec, qd, kd, cd)


DEC_BB = 8


def _ret_decode_kernel(q_ref, k_ref, v_ref, g_ref, cos_ref, sin_ref, gam_ref, s_ref, *rest, li, fill):
    o_ref, so_ref = rest[-2:]
    so_ref = _layer_view(so_ref, li, fill)
    cos = cos_ref[...]
    sin = sin_ref[...]
    for h in range(RET_HEADS):
        cols = slice(h * 128, (h + 1) * 128)
        qb = _bf(_rope128(q_ref[:, cols], cos, sin))
        kb = _bf(_rope128(k_ref[:, cols], cos, sin) * (RET_DK ** -0.5))
        vb = _bf(v_ref[:, cols])
        gam = gam_ref[h]
        qk = jnp.sum(qb.astype(F32) * kb.astype(F32), axis=-1, keepdims=True)
        o = _bf(qk).astype(F32) * vb.astype(F32)
        rows = []
        for b in range(DEC_BB):
            s = s_ref[b, h]
            rows.append(_dot(qb, _bf(s))[b:b + 1])
            so_ref[b, h] = s * gam + _dot_tn(_row_only(kb, b), vb)
        o = o + jnp.concatenate(rows, axis=0) * gam
        r = o * lax.rsqrt(jnp.mean(o * o, axis=-1, keepdims=True) + EPS)
        o_ref[:, cols] = r * _silu(g_ref[:, cols])


def _ret_decode(proj, cos, sin, state, li, prev):
    db = proj.shape[0]
    gam = jnp.asarray(np.broadcast_to(_ret_gamma()[:, None, None], (RET_HEADS, 1, LANES)).copy(), F32)
    extra, extra_specs, alias = _layer_out(prev)
    depth = state.shape[0]
    blk = (DEC_BB, RET_HEADS, RET_DK, RET_DK)
    st_spec = pl.BlockSpec((None,) + blk, lambda i: (li, i, 0, 0, 0))
    return pl.pallas_call(
        functools.partial(_ret_decode_kernel, li=li, fill=depth if prev is None else 0),
        grid=(db // DEC_BB,),
        in_specs=[
            pl.BlockSpec((DEC_BB, 512), lambda i: (i, 0)),
            pl.BlockSpec((DEC_BB, 512), lambda i: (i, 1)),
            pl.BlockSpec((DEC_BB, 512), lambda i: (i, 2)),
            pl.BlockSpec((DEC_BB, 512), lambda i: (i, 3)),
            pl.BlockSpec((DEC_BB, LANES), lambda i: (i, 0)),
            pl.BlockSpec((DEC_BB, LANES), lambda i: (i, 0)),
            pl.BlockSpec((RET_HEADS, 1, LANES), lambda i: (0, 0, 0)),
            st_spec,
        ] + extra_specs,
        out_specs=[pl.BlockSpec((DEC_BB, 512), lambda i: (i, 0)),
                   _layer_spec(prev, depth, li, blk, lambda i: (i, 0, 0, 0))],
        out_shape=[
            jax.ShapeDtypeStruct((db, RET_WIDTH), F32),
            jax.ShapeDtypeStruct(state.shape, F32),
        ],
        input_output_aliases={} if alias is None else {8: 1},
        compiler_params=_cparams(("arbitrary",)),
        name="ret_decode",
    )(proj, proj, proj, proj, cos, sin, gam, state, *extra)


def _head_expand_mat():
    r = lax.broadcasted_iota(jnp.int32, (LANES, SSD_WIDTH), 0)
    c = lax.broadcasted_iota(jnp.int32, (LANES, SSD_WIDTH), 1)
    return jnp.where(r == (c >> 6), 1.0, 0.0).astype(BF16)


def _ssd_prompt_kernel(z_ref, xbc_ref, dt_ref, cw_ref, cb_ref, dtb_ref, alog_ref, dexp_ref, nw_ref,
                       y_ref, ho_ref, co_ref, ext_ref, ht_ref):
    i = pl.program_id(0)

    @pl.when(i == 0)
    def _():
        ext_ref[0:8, :] = jnp.zeros((8, SSD_CONV_DIM), F32)
        ht_ref[...] = jnp.zeros(ht_ref.shape, F32)

    u = xbc_ref[...]
    ext_ref[8:8 + CHUNK, :] = u
    cw = cw_ref[...]
    conv = (cb_ref[...] + cw[3:4, :] * u + cw[2:3, :] * ext_ref[7:7 + CHUNK, :]
            + cw[1:2, :] * ext_ref[6:6 + CHUNK, :] + cw[0:1, :] * ext_ref[5:5 + CHUNK, :])
    ext_ref[0:8, :] = u[CHUNK - 8:CHUNK, :]
    xbc = _silu(conv)
    xs = xbc[:, 0:SSD_WIDTH]

    dt = _softplus(dt_ref[...] + dtb_ref[...])
    a = dt * (-jnp.exp(alog_ref[...]))
    ri = lax.broadcasted_iota(jnp.int32, (CHUNK, CHUNK), 0)
    ci = lax.broadcasted_iota(jnp.int32, (CHUNK, CHUNK), 1)
    causal = ri >= ci
    tri = jnp.where(causal, 1.0, 0.0).astype(BF16)
    cum = _dot_tri(tri, a)
    cum_t = cum.T
    dt_t = dt.T
    cum_last = cum[CHUNK - 1:CHUNK, :]
    eh = _head_expand_mat()
    ecum_x = _dot_split(jnp.exp(cum), eh)
    wgt_x = _dot_split(jnp.exp(cum_last - cum) * dt, eh)
    elast_x = _dot_split(jnp.broadcast_to(jnp.exp(cum_last), (8, LANES)), eh)[0:1, :]

    lane = lax.broadcasted_iota(jnp.int32, (CHUNK, LANES), 1)
    lo_half = lane < 64
    xw = _bf(xs * wgt_x)
    y_parts = []
    ch_parts = []
    for g in range(SSD_GROUPS):
        bg = xbc[:, SSD_WIDTH + g * 128:SSD_WIDTH + (g + 1) * 128]
        cg = xbc[:, SSD_WIDTH + 256 + g * 128:SSD_WIDTH + 256 + (g + 1) * 128]
        cgb = _bf(cg)
        cb = _dot_nt(cgb, _bf(bg))
        ht = ht_ref[g]
        ch_parts.append(_dot(cgb, _bf(ht)))
        for k in range(4):
            h0 = g * 8 + 2 * k
            xp = _bf(xs[:, h0 * 64:(h0 + 2) * 64])
            ys = []
            for hh in (h0, h0 + 1):
                seg = cum[:, hh:hh + 1] - cum_t[hh:hh + 1, :]
                lm = jnp.where(causal, jnp.exp(jnp.minimum(seg, 0.0)), 0.0)
                sc = cb * lm * dt_t[hh:hh + 1, :]
                ys.append(_dot(_bf(sc), xp))
            y_parts.append(jnp.where(lo_half, ys[0], ys[1]))
        bgt = _bf(bg.T)
        ht_ref[g] = ht * elast_x[:, g * 512:(g + 1) * 512] + _dot(bgt, xw[:, g * 512:(g + 1) * 512])
    y = jnp.concatenate(y_parts, axis=1) + jnp.concatenate(ch_parts, axis=1) * ecum_x + dexp_ref[...] * xs
    gated = y * _silu(z_ref[...])
    y_ref[...] = gated * lax.rsqrt(jnp.mean(gated * gated, axis=-1, keepdims=True) + EPS) * nw_ref[...]

    @pl.when(i == pl.num_programs(0) - 1)
    def _():
        co_ref[...] = u[CHUNK - 8:CHUNK, :]
        for g in range(SSD_GROUPS):
            htf = ht_ref[g]
            for k in range(4):
                h0 = g * 8 + 2 * k
                ho_ref[h0:h0 + 2] = htf[:, k * 128:(k + 1) * 128].T.reshape(2, SSD_HEADDIM, SSD_STATE)


def _ssd_prompt(proj, conv_w, conv_b, dt_bias, a_log, d, norm_w):
    l = proj.shape[0]
    full2 = lambda shape: pl.BlockSpec(shape, lambda i: (0, 0))
    return pl.pallas_call(
        _ssd_prompt_kernel,
        grid=(l // CHUNK,),
        in_specs=[
            pl.BlockSpec((CHUNK, SSD_WIDTH), lambda i: (i, C_SZ // SSD_WIDTH)),
            pl.BlockSpec((CHUNK, SSD_CONV_DIM), lambda i: (i, C_XBC // SSD_CONV_DIM)),
            pl.BlockSpec((CHUNK, LANES), lambda i: (i, C_SDT // LANES)),
            full2((SSD_CONV, SSD_CONV_DIM)),
            full2((1, SSD_CONV_DIM)),
            full2((1, LANES)),
            full2((1, LANES)),
            full2((1, SSD_WIDTH)),
            full2((1, SSD_WIDTH)),
        ],
        out_specs=[
            pl.BlockSpec((CHUNK, SSD_WIDTH), lambda i: (i, 0)),
            pl.BlockSpec((SSD_HEADS, SSD_HEADDIM, SSD_STATE), lambda i: (0, 0, 0)),
            full2((8, SSD_CONV_DIM)),
        ],
        out_shape=[
            jax.ShapeDtypeStruct((l, SSD_WIDTH), F32),
            jax.ShapeDtypeStruct((SSD_HEADS, SSD_HEADDIM, SSD_STATE), F32),
            jax.ShapeDtypeStruct((8, SSD_CONV_DIM), F32),
        ],
        scratch_shapes=[
            pltpu.VMEM((8 + CHUNK, SSD_CONV_DIM), F32),
            pltpu.VMEM((SSD_GROUPS, SSD_STATE, 512), F32),
        ],
        compiler_params=_cparams(("arbitrary",)),
        name="ssd_prompt",
    )(proj, proj, proj, conv_w, conv_b.reshape(1, -1), _pad_lanes(dt_bias), _pad_lanes(a_log),
      jnp.repeat(d, SSD_HEADDIM).reshape(1, -1), norm_w.reshape(1, -1))


def _ssd_decode_kernel(z_ref, xbc_ref, dt_ref, cs_ref, h_ref, cw_ref, cb_ref, dtb_ref, alog_ref, dexp_ref, nw_ref,
                       *rest, li, fill):
    y_ref, co_ref, ho_ref = rest[-3:]
    ho_ref = _layer_view(ho_ref, li, fill)
    u = xbc_ref[...]
    c0 = cs_ref[0]
    c1 = cs_ref[1]
    c2 = cs_ref[2]
    cw = cw_ref[...]
    conv = cb_ref[...] + cw[3:4, :] * u + cw[2:3, :] * c2 + cw[1:2, :] * c1 + cw[0:1, :] * c0
    co_ref[0] = c1
    co_ref[1] = c2
    co_ref[2] = u
    xbc = _silu(conv)
    xs = xbc[:, 0:SSD_WIDTH]
    dt = _softplus(dt_ref[...] + dtb_ref[...])
    ea = jnp.exp(dt * (-jnp.exp(alog_ref[...])))
    eh = _head_expand_mat()
    dt_x = _dot_split(dt, eh)
    ea_x = _dot_split(ea, eh)
    dtx = dt_x * xs
    ones = jnp.ones((DEC_BB, LANES), BF16)
    ych = [[None, None] for _ in range(DEC_BB)]
    cbs = []
    for g in range(SSD_GROUPS):
        gc = slice(g * 512, (g + 1) * 512)
        bg = xbc[:, SSD_WIDTH + g * 128:SSD_WIDTH + (g + 1) * 128]
        cg = xbc[:, SSD_WIDTH + 256 + g * 128:SSD_WIDTH + 256 + (g + 1) * 128]
        cgb = _bf(cg)
        cbs.append(jnp.sum(cgb.astype(F32) * _bf(bg).astype(F32), axis=-1, keepdims=True))
        b_hi, b_lo = _split2(bg)
        for b in range(DEC_BB):
            hs = h_ref[b, g * 8:(g + 1) * 8].reshape(512, SSD_STATE)
            ych[b][g] = _dot_nt(cgb, _bf(hs))[b:b + 1]
            e_hi, e_lo = _split2(_row_only(ea_x[:, gc], b))
            decay = _dot_tn(e_hi, ones) + _dot_tn(e_lo, ones)
            x_hi, x_lo = _split2(_row_only(dtx[:, gc], b))
            upd = _dot_tn(x_hi, b_hi) + _dot_tn(x_hi, b_lo) + _dot_tn(x_lo, b_hi)
            ho_ref[b, g * 8:(g + 1) * 8] = (hs * decay + upd).reshape(8, SSD_HEADDIM, SSD_STATE)
    ych = jnp.concatenate([jnp.concatenate(r, axis=1) for r in ych], axis=0)
    lane = lax.broadcasted_iota(jnp.int32, (DEC_BB, SSD_WIDTH), 1)
    cbx = jnp.where(lane < 512, cbs[0], cbs[1])
    y = dt_x * cbx * xs + ych * ea_x + dexp_ref[...] * xs
    gated = y * _silu(z_ref[...])
    y_ref[...] = gated * lax.rsqrt(jnp.mean(gated * gated, axis=-1, keepdims=True) + EPS) * nw_ref[...]


def _ssd_decode(proj, conv_state_t, ssm_state, li, prev, conv_w, conv_b, dt_bias, a_log, d, norm_w):
    db = proj.shape[0]
    full2 = lambda shape: pl.BlockSpec(shape, lambda i: (0, 0))
    extra, extra_specs, alias = _layer_out(prev)
    depth = ssm_state.shape[0]
    blk = (DEC_BB, SSD_HEADS, SSD_HEADDIM, SSD_STATE)
    st_spec = pl.BlockSpec((None,) + blk, lambda i: (li, i, 0, 0, 0))
    return pl.pallas_call(
        functools.partial(_ssd_decode_kernel, li=li, fill=depth if prev is None else 0),
        grid=(db // DEC_BB,),
        in_specs=[
            pl.BlockSpec((DEC_BB, SSD_WIDTH), lambda i: (i, C_SZ // SSD_WIDTH)),
            pl.BlockSpec((DEC_BB, SSD_CONV_DIM), lambda i: (i, C_XBC // SSD_CONV_DIM)),
            pl.BlockSpec((DEC_BB, LANES), lambda i: (i, C_SDT // LANES)),
            pl.BlockSpec((None, SSD_CONV - 1, DEC_BB, SSD_CONV_DIM), lambda i: (li, 0, i, 0)),
            st_spec,
            full2((SSD_CONV, SSD_CONV_DIM)),
            full2((1, SSD_CONV_DIM)),
            full2((1, LANES)),
            full2((1, LANES)),
            full2((1, SSD_WIDTH)),
            full2((1, SSD_WIDTH)),
        ] + extra_specs,
        out_specs=[
            pl.BlockSpec((DEC_BB, SSD_WIDTH), lambda i: (i, 0)),
            pl.BlockSpec((SSD_CONV - 1, DEC_BB, SSD_CONV_DIM), lambda i: (0, i, 0)),
            _layer_spec(prev, depth, li, blk, lambda i: (i, 0, 0, 0)),
        ],
        out_shape=[
            jax.ShapeDtypeStruct((db, SSD_WIDTH), F32),
            jax.ShapeDtypeStruct((SSD_CONV - 1, db, SSD_CONV_DIM), F32),
            jax.ShapeDtypeStruct(ssm_state.shape, F32),
        ],
        input_output_aliases={} if alias is None else {11: 2},
        compiler_params=_cparams(("arbitrary",)),
        name="ssd_decode",
    )(proj, proj, proj, conv_state_t, ssm_state, conv_w, conv_b.reshape(1, -1), _pad_lanes(dt_bias),
      _pad_lanes(a_log), jnp.repeat(d, SSD_HEADDIM).reshape(1, -1), norm_w.reshape(1, -1), *extra)


def _value_variants(v):
    lane = lax.broadcasted_iota(jnp.int32, v.shape, 1)
    lo = lane < 64
    sw = pltpu.roll(v, 64, 1)
    one = jnp.ones_like(v)
    return [jnp.where(lo, v, one), jnp.where(lo, one, sw), jnp.where(lo, sw, one), jnp.where(lo, one, v)]


def _nsa_prep_kernel(nq_ref, nkv_ref, cos_ref, sin_ref, qn_ref, ksn_ref, kwn_ref, *rest, li, fill):
    qb_ref, cache_ref, win_ref, cache_t_ref, win_t_ref, kvb_ref = rest[-6:]
    cache_t_ref = _layer_view(cache_t_ref, li, fill)
    cos = cos_ref[...]
    sin = sin_ref[...]
    bd = _half_mean_mat()
    qn = qn_ref[...]
    for h in range(NSA_HEADS):
        cols = slice(h * 128, (h + 1) * 128)
        x = nq_ref[:, cols]
        ms = jnp.sum(x * x, axis=-1, keepdims=True) * (1.0 / NSA_HD)
        qh = _rope64(x * lax.rsqrt(ms + EPS) * qn, cos, sin)
        qb_ref[:, cols] = _bf(qh * (NSA_HD ** -0.5))
    kc = _rope64(nkv_ref[:, 0:128], cos, sin)
    vc = nkv_ref[:, 128:256]
    ks = _rope64(_rms64(nkv_ref[:, 256:384], ksn_ref[...], bd), cos, sin)
    vs = nkv_ref[:, 384:512]
    kw = _rope64(_rms64(nkv_ref[:, 512:640], kwn_ref[...], bd), cos, sin)
    vw = nkv_ref[:, 640:768]
    for i, v in enumerate((kc, vc, ks, vs)):
        cache_ref[:, i * 128:(i + 1) * 128] = v
        cache_t_ref[i] = v.T
    for i, v in enumerate((kw, vw)):
        win_ref[:, i * 128:(i + 1) * 128] = v
        win_t_ref[i] = v.T
    kvb_ref[0] = _bf(ks)
    kvb_ref[1] = _bf(kw)
    for i, v in enumerate(_value_variants(vs)):
        kvb_ref[2 + i] = _bf(v)
    for i, v in enumerate(_value_variants(vw)):
        kvb_ref[6 + i] = _bf(v)


def _nsa_prep(proj, cos, sin, li, depth, prev, qn, ksn, kwn):
    m = proj.shape[0]
    t = min(m, 256)
    extra, extra_specs, alias = _layer_out(prev)
    two = lambda w: jnp.concatenate([w, w]).reshape(1, LANES)
    full2 = lambda shape: pl.BlockSpec(shape, lambda i: (0, 0))
    return pl.pallas_call(
        functools.partial(_nsa_prep_kernel, li=li, fill=depth if prev is None else 0),
        grid=(m // t,),
        in_specs=[
            pl.BlockSpec((t, 1024), lambda i: (i, C_NQ // 1024)),
            pl.BlockSpec((t, 768), lambda i: (i, C_NKV // 768)),
            pl.BlockSpec((t, LANES), lambda i: (i, 0)),
            pl.BlockSpec((t, LANES), lambda i: (i, 0)),
            full2((1, LANES)), full2((1, LANES)), full2((1, LANES)),
        ] + extra_specs,
        out_specs=[
            pl.BlockSpec((t, 1024), lambda i: (i, 0)),
            pl.BlockSpec((t, 512), lambda i: (i, 0)),
            pl.BlockSpec((t, 256), lambda i: (i, 0)),
            _layer_spec(prev, depth, li, (4, LANES, t), lambda i: (0, 0, i)),
            pl.BlockSpec((2, LANES, t), lambda i: (0, 0, i)),
            pl.BlockSpec((10, t, LANES), lambda i: (0, i, 0)),
        ],
        out_shape=[
            jax.ShapeDtypeStruct((m, 1024), BF16),
            jax.ShapeDtypeStruct((m, 512), F32),
            jax.ShapeDtypeStruct((m, 256), F32),
            jax.ShapeDtypeStruct((depth, 4, LANES, m), F32),
            jax.ShapeDtypeStruct((2, LANES, m), F32),
            jax.ShapeDtypeStruct((10, m, LANES), BF16),
        ],
        input_output_aliases={} if alias is None else {7: 3},
        compiler_params=_cparams(("arbitrary",)),
        name="nsa_prep",
    )(proj, proj, cos, sin, two(qn), two(ksn), two(kwn), *extra)


def _compress_kernel(rk_ref, rv_ref, w1_ref, pe_ref, w2_ref, kn_ref, kc_ref, vc_ref):
    tr = rk_ref.shape[0]
    bd = _half_mean_mat()

    def mlp(r, j):
        ha = _dot(_bf(r + pe_ref[2 * j:2 * j + 1, :]), w1_ref[2 * j])
        hb = _dot(_bf(r + pe_ref[2 * j + 1:2 * j + 2, :]), w1_ref[2 * j + 1])
        hid = ha + pltpu.roll(hb, tr - 1, 0)
        return _dot(_bf(_silu(hid)), w2_ref[j])

    kc = _rms64(mlp(rk_ref[...], 0), kn_ref[...], bd)
    vc = mlp(rv_ref[...], 1)
    kc_ref[...] = _bf(kc)
    vc_ref[0] = _bf(vc)
    vc_ref[1] = _bf(pltpu.roll(vc, 64, 1))


def _compress_weights(pe, w1, w2):
    w1r = w1.reshape(2, 2, 16, 64, 64)
    z = jnp.zeros_like(w1r)
    top = jnp.concatenate([w1r, z], axis=-1)
    bot = jnp.concatenate([z, w1r], axis=-1)
    w1x = jnp.stack([top, bot], axis=3)
    w1x = w1x.reshape(4, 2048, LANES).astype(BF16)
    per = pe.reshape(2, 2, 16, 1, 64)
    pex = jnp.broadcast_to(per, (2, 2, 16, 2, 64)).reshape(4, 2048)
    z2 = jnp.zeros_like(w2)
    w2x = jnp.concatenate([jnp.concatenate([w2, z2], -1), jnp.concatenate([z2, w2], -1)], axis=1).astype(BF16)
    return w1x, pex, w2x


def _compress(rk, rv, w1x, pex, w2x, kn):
    nr = rk.shape[0]
    tr = min(nr, 512)
    two = jnp.concatenate([kn, kn]).reshape(1, LANES)
    return pl.pallas_call(
        _compress_kernel,
        grid=(nr // tr,),
        in_specs=[
            pl.BlockSpec((tr, 2048), lambda i: (i, 0)),
            pl.BlockSpec((tr, 2048), lambda i: (i, 0)),
            pl.BlockSpec((4, 2048, LANES), lambda i: (0, 0, 0)),
            pl.BlockSpec((4, 2048), lambda i: (0, 0)),
            pl.BlockSpec((2, LANES, LANES), lambda i: (0, 0, 0)),
            pl.BlockSpec((1, LANES), lambda i: (0, 0)),
        ],
        out_specs=[
            pl.BlockSpec((tr, LANES), lambda i: (i, 0)),
            pl.BlockSpec((2, tr, LANES), lambda i: (0, i, 0)),
        ],
        out_shape=[
            jax.ShapeDtypeStruct((nr, LANES), BF16),
            jax.ShapeDtypeStruct((2, nr, LANES), BF16),
        ],
        compiler_params=_cparams(("arbitrary",)),
        name="nsa_compress",
    )(rk, rv, w1x, pex, w2x, two)


QB = 128
SEL_TK = 512
WIN_TK = 128


def _nsa_prompt_kernel(q_ref, kc_ref, vc_ref, kvb_ref, ov_ref, gate_ref, eg_ref, nz_ref, o_ref,
                       m_ref, acc_ref, st_ref):
    g = pl.program_id(0)
    qi = pl.program_id(1)
    t0 = qi * QB
    nc = kc_ref.shape[0]
    q = q_ref[...]
    qs = jnp.concatenate([q[:, 0:128], q[:, 256:384], q[:, 128:256], q[:, 384:512]], axis=0)
    trow = t0 + (lax.broadcasted_iota(jnp.int32, (4 * QB, 1), 0) & (QB - 1))
    tq = t0 + lax.broadcasted_iota(jnp.int32, (QB, 1), 0)

    def pv(pb, ve, vo):
        return jnp.concatenate([_dot(pb[0:2 * QB], ve), _dot(pb[2 * QB:4 * QB], vo)], axis=0)

    sc = _dot_nt(qs, kc_ref[...])
    n_io = lax.broadcasted_iota(jnp.int32, (1, nc), 1)
    mask_c = (n_io * CMP_STRIDE + (CMP_BLOCK - 1)) <= trow
    sc = jnp.where(mask_c, sc, NEG)
    mc = jnp.max(sc, axis=1, keepdims=True)
    pc = jnp.where(mask_c, jnp.exp(sc - mc), 0.0)
    zc = jnp.sum(pc, axis=1, keepdims=True)
    pc = pc / jnp.where(zc > 0, zc, 1.0)
    o_c = pv(_bf(pc), vc_ref[g], vc_ref[1 - g])

    p4 = pc[0:QB] + pc[QB:2 * QB] + pc[2 * QB:3 * QB] + pc[3 * QB:4 * QB]
    imp = _dot_split(p4, ov_ref[...])
    j_io = lax.broadcasted_iota(jnp.int32, (1, LANES), 1)
    cur = tq >> 6
    forced = (j_io == 0) | ((j_io <= cur) & (j_io > cur - N_LOCAL))
    valid = j_io <= cur
    score = jnp.where(forced, FORCE_SCORE, imp)
    score = jnp.where(valid, score, -FORCE_SCORE)
    score_t = score.T
    st_ref[...] = score_t
    jb = lax.broadcasted_iota(jnp.int32, (LANES, QB), 0)
    n_blk = ((t0 + QB - 1) >> 6) + 1

    def rank_body(i, rank):
        row = st_ref[pl.ds(i, 1), :]
        beats = (row > score_t) | ((row == score_t) & (i < jb))
        return rank + jnp.where(beats, 1.0, 0.0)

    rank_t = lax.fori_loop(0, n_blk, rank_body, jnp.zeros((LANES, QB), F32))
    sel = jnp.where((rank_t.T < SLC_TOPN) & valid, 1.0, 0.0).astype(BF16)

    def reset():
        m_ref[...] = jnp.full(m_ref.shape, NEG, F32)
        acc_ref[...] = jnp.zeros(acc_ref.shape, F32)

    def update(kidx, ve, vo, k0, tk, bias, causal):
        s = _dot_nt(qs, kvb_ref[kidx, pl.ds(k0, tk), :])
        if bias is not None:
            s = s + bias
        if causal:
            kpos = k0 + lax.broadcasted_iota(jnp.int32, (1, tk), 1)
            s = jnp.where(kpos <= trow, s, NEG)
        m_old = m_ref[...]
        m_new = jnp.maximum(m_old, jnp.max(s, axis=1, keepdims=True))
        p = _bf(jnp.exp(s - m_new))
        acc_ref[...] = jnp.exp(m_old - m_new) * acc_ref[...] + pv(
            p, kvb_ref[ve, pl.ds(k0, tk), :], kvb_ref[vo, pl.ds(k0, tk), :])
        m_ref[...] = m_new

    def result():
        a = acc_ref[...]
        den = pltpu.roll(a, 64, 1)
        return a / jnp.where(den > 0, den, 1.0)

    jrow = lax.broadcasted_iota(jnp.int32, (LANES, 1), 0)

    def sel_bias(k0):
        kpos = k0 + lax.broadcasted_iota(jnp.int32, (1, SEL_TK), 1)
        et = jnp.where(jrow == (kpos >> 6), 1.0, 0.0).astype(BF16)
        b = (_dot(sel, et) - 1.0) * (-NEG)
        return jnp.concatenate([b, b, b, b], axis=0)

    reset()
    vse = 2 + 2 * g
    n_full = t0 // SEL_TK

    def sel_body(kt, c):
        k0 = pl.multiple_of(kt * SEL_TK, SEL_TK)
        update(0, vse, vse + 1, k0, SEL_TK, sel_bias(k0), False)
        return c

    lax.fori_loop(0, n_full, sel_body, 0)
    k_diag = pl.multiple_of(n_full * SEL_TK, SEL_TK)
    update(0, vse, vse + 1, k_diag, SEL_TK, sel_bias(k_diag), True)
    o_s = result()

    reset()
    vwe = 6 + 2 * g
    n_old = WINDOW // WIN_TK

    @pl.when(qi >= n_old)
    def _():
        k0 = pl.multiple_of((qi - n_old) * WIN_TK, WIN_TK)
        kpos = k0 + lax.broadcasted_iota(jnp.int32, (1, WIN_TK), 1)
        bias = jnp.where(trow - kpos <= WINDOW, 0.0, NEG)
        update(1, vwe, vwe + 1, k0, WIN_TK, bias, False)

    def win_body(kt, c):
        update(1, vwe, vwe + 1, pl.multiple_of(kt * WIN_TK, WIN_TK), WIN_TK, None, False)
        return c

    lax.fori_loop(jnp.maximum(qi - n_old + 1, 0), qi, win_body, 0)
    update(1, vwe, vwe + 1, pl.multiple_of(qi * WIN_TK, WIN_TK), WIN_TK, None, True)
    o_w = result()

    gx = _dot_split(jax.nn.sigmoid(gate_ref[...]), eg_ref[0])
    lane = lax.broadcasted_iota(jnp.int32, (QB, LANES), 1)
    lo_half = lane < 64
    for k in range(2):
        ra = slice(k * QB, (k + 1) * QB)
        rb = slice(2 * QB + k * QB, 2 * QB + (k + 1) * QB)
        cols = slice(k * 128, (k + 1) * 128)
        tile = lambda o: jnp.where(lo_half, o[ra], o[rb])
        o = (gx[:, k * 128:(k + 1) * 128] * tile(o_c)
             + gx[:, 256 + k * 128:256 + (k + 1) * 128] * tile(o_s)
             + gx[:, 512 + k * 128:512 + (k + 1) * 128] * tile(o_w))
        o_ref[:, cols] = o * _silu(nz_ref[:, cols])


def _overlap_mat(nc):
    n = np.arange(nc)[:, None]
    j = np.arange(LANES)[None, :]
    ov = ((n * CMP_STRIDE < (j + 1) * SLC_BLOCK) & (n * CMP_STRIDE + CMP_BLOCK - 1 >= j * SLC_BLOCK))
    return jnp.asarray(ov.astype(np.float32), BF16)


def _gate_expand_mat():
    eg = np.zeros((NSA_KV_HEADS, LANES, 3 * 256), np.float32)
    for g in range(NSA_KV_HEADS):
        for hh in range(4):
            for c in range(3):
                eg[g, (g * 4 + hh) * 3 + c, c * 256 + hh * 64:c * 256 + (hh + 1) * 64] = 1.0
    return jnp.asarray(eg, BF16)


def _nsa_prompt(qb, kc, vc2, kvb, proj):
    l = qb.shape[0]
    nc = kc.shape[0]
    assert l // SLC_BLOCK <= LANES and l % SEL_TK == 0
    return pl.pallas_call(
        _nsa_prompt_kernel,
        grid=(NSA_KV_HEADS, l // QB),
        in_specs=[
            pl.BlockSpec((QB, 512), lambda g, i: (i, g)),
            pl.BlockSpec((nc, LANES), lambda g, i: (0, 0)),
            pl.BlockSpec((2, nc, LANES), lambda g, i: (0, 0, 0)),
            pl.BlockSpec((10, l, LANES), lambda g, i: (0, 0, 0)),
            pl.BlockSpec((nc, LANES), lambda g, i: (0, 0)),
            pl.BlockSpec((QB, LANES), lambda g, i: (i, C_GATE // LANES)),
            pl.BlockSpec((1, LANES, 768), lambda g, i: (g, 0, 0)),
            pl.BlockSpec((QB, 256), lambda g, i: (i, C_NZ // 256 + g)),
        ],
        out_specs=pl.BlockSpec((QB, 256), lambda g, i: (i, g)),
        out_shape=jax.ShapeDtypeStruct((l, NSA_WIDTH), F32),
        scratch_shapes=[
            pltpu.VMEM((4 * QB, 1), F32),
            pltpu.VMEM((4 * QB, LANES), F32),
            pltpu.VMEM((LANES, QB), F32),
        ],
        compiler_params=_cparams(("arbitrary", "arbitrary")),
        name="nsa_prompt",
    )(qb, kc, vc2, kvb, _overlap_mat(nc), proj, _gate_expand_mat(), proj)


def _softmax_with_new(s, valid, s_new):
    s = jnp.where(valid, s, NEG)
    m = jnp.maximum(jnp.max(s, axis=1, keepdims=True), s_new)
    p = jnp.where(valid, jnp.exp(s - m), 0.0)
    pn = jnp.exp(s_new - m)
    z = jnp.sum(p, axis=1, keepdims=True) + pn
    return p / z, pn / z


def _nsa_decode_kernel(pt_ref, q_ref, cache_hbm, crow_ref, win_ref, wrow_ref, wcol_ref, gate_ref, nz_ref,
                       w1_ref, pe_ref, w2_ref, kn_ref, ov_ref, ek_ref, *rest, qpos, li, fill):
    o_ref, wo_ref, buf, tok_ref, sem = rest[-5:]
    wo_ref = _layer_view(wo_ref, li, fill)
    b = pl.program_id(0)
    slot = b % 2
    n_pages = buf.shape[1]
    page = buf.shape[4]
    t = n_pages * page
    nr = t // CMP_STRIDE
    wb = win_ref.shape[-1]
    n_cmp = (t + 1 - CMP_BLOCK) // CMP_STRIDE + 1

    def page_copies(row, s):
        return [pltpu.make_async_copy(cache_hbm.at[li, pt_ref[row, p]], buf.at[s, p], sem.at[s])
                for p in range(n_pages)]

    @pl.when(b == 0)
    def _():
        for c in page_copies(0, 0):
            c.start()

    @pl.when(b + 1 < pl.num_programs(0))
    def _():
        for c in page_copies(b + 1, 1 - slot):
            c.start()

    for c in page_copies(b, slot):
        c.wait()

    q8 = q_ref[0]
    q8f = q8.astype(F32)
    bd = _half_mean_mat()

    for p in range(n_pages):
        for j in range(2):
            tok_ref[j, p * page:(p + 1) * page, :] = buf[slot, p, j].T

    def mlp(j):
        acc_a = jnp.zeros((nr, LANES), F32)
        acc_b = jnp.zeros((nr, LANES), F32)
        for l2 in range(CMP_STRIDE // 2):
            r = jnp.concatenate([tok_ref[j, pl.ds(2 * l2, nr, stride=CMP_STRIDE), :],
                                 tok_ref[j, pl.ds(2 * l2 + 1, nr, stride=CMP_STRIDE), :]], axis=1)
            cols = slice(l2 * 256, (l2 + 1) * 256)
            acc_a = acc_a + _dot(_bf(r + pe_ref[2 * j:2 * j + 1, cols]), w1_ref[2 * j, cols, :])
            acc_b = acc_b + _dot(_bf(r + pe_ref[2 * j + 1:2 * j + 2, cols]), w1_ref[2 * j + 1, cols, :])
        hid = acc_a + pltpu.roll(acc_b, nr - 1, 0)
        return _dot(_bf(_silu(hid)), w2_ref[j])

    kcc = _rms64(mlp(0), kn_ref[...], bd)
    vcc = mlp(1)

    n_io = lax.broadcasted_iota(jnp.int32, (1, nr), 1)
    mask_c = ((n_io * CMP_STRIDE + (CMP_BLOCK - 1)) <= qpos) & (n_io < n_cmp)
    sc = jnp.where(mask_c, _dot_nt(q8, _bf(kcc)), NEG)
    mc = jnp.max(sc, axis=1, keepdims=True)
    pc = jnp.where(mask_c, jnp.exp(sc - mc), 0.0)
    zc = jnp.sum(pc, axis=1, keepdims=True)
    pc = pc / jnp.where(zc > 0, zc, 1.0)
    o_c = _dot(_bf(pc), _bf(vcc))

    r8 = lax.broadcasted_iota(jnp.int32, (8, 8), 0)
    c8 = lax.broadcasted_iota(jnp.int32, (8, 8), 1)
    gsum = jnp.where((r8 >> 2) == (c8 >> 2), 1.0, 0.0).astype(BF16)
    imp = _dot_split(_dot_tri(gsum, pc), ov_ref[...])
    j_io = lax.broadcasted_iota(jnp.int32, (1, LANES), 1)
    cur = qpos // SLC_BLOCK
    forced = (j_io == 0) | ((j_io <= cur) & (j_io > cur - N_LOCAL))
    valid = j_io <= cur
    score = jnp.where(forced, FORCE_SCORE, imp)
    score = jnp.where(valid, score, -FORCE_SCORE)
    rank = jnp.zeros((8, LANES), F32)
    for i in range(cur + 1):
        col = score[:, i:i + 1]
        rank = rank + jnp.where((col > score) | ((col == score) & (i < j_io)), 1.0, 0.0)
    sel = jnp.where((rank < SLC_TOPN) & valid, 1.0, 0.0)

    crow = crow_ref[0]
    ks_new = _bf(crow[:, 256:384]).astype(F32)
    vs_new = _bf(crow[:, 384:512]).astype(F32)
    sel_past = _dot(_bf(sel), ek_ref[...]) > 0.5
    s_s = jnp.concatenate([_dot(q8, _bf(buf[slot, p, 2])) for p in range(n_pages)], axis=1)
    s_new = jnp.where(sel[:, cur:cur + 1] > 0.5, jnp.sum(q8f * ks_new, axis=-1, keepdims=True), NEG)
    p_s, pn_s = _softmax_with_new(s_s, sel_past, s_new)
    p_sb = _bf(p_s)
    o_s = _bf(pn_s).astype(F32) * vs_new
    for p in range(n_pages):
        o_s = o_s + _dot_nt(p_sb[:, p * page:(p + 1) * page], _bf(buf[slot, p, 3]))

    wrow = wrow_ref[0]
    kw_new = _bf(wrow[:, 0:128]).astype(F32)
    vw_new = _bf(wrow[:, 128:256]).astype(F32)
    i_io = lax.broadcasted_iota(jnp.int32, (1, wb), 1)
    valid_w = ((wb - i_io) <= WINDOW) & ((qpos - wb + i_io) >= 0)
    s_w = _dot(q8, _bf(win_ref[0]))
    p_w, pn_w = _softmax_with_new(s_w, valid_w, jnp.sum(q8f * kw_new, axis=-1, keepdims=True))
    o_w = _dot_nt(_bf(p_w), _bf(win_ref[1])) + _bf(pn_w).astype(F32) * vw_new

    sig = jnp.broadcast_to(jax.nn.sigmoid(gate_ref[0]), (8, LANES))
    lane8 = lax.broadcasted_iota(jnp.int32, (8, LANES), 1)
    row8 = lax.broadcasted_iota(jnp.int32, (8, LANES), 0)
    gate = lambda c: jnp.sum(jnp.where(lane8 == row8 * 3 + c, sig, 0.0), axis=-1, keepdims=True)
    o8 = gate(0) * o_c + gate(1) * o_s + gate(2) * o_w

    sw = pltpu.roll(o8, 64, 1)
    lo = lax.broadcasted_iota(jnp.int32, (1, LANES), 1) < 64
    flat = jnp.concatenate([
        jnp.where(lo, o8[0:1], sw[1:2]), jnp.where(lo, o8[2:3], sw[3:4]),
        jnp.where(lo, sw[4:5], o8[5:6]), jnp.where(lo, sw[6:7], o8[7:8])], axis=1)
    o_ref[0] = flat * _silu(nz_ref[0])

    own_col = lax.broadcasted_iota(jnp.int32, (LANES, wcol_ref.shape[2]), 1) == (b % wcol_ref.shape[2])
    last = lax.broadcasted_iota(jnp.int32, (LANES, wb), 1) == wb - 1
    for j in range(2):
        col = jnp.sum(jnp.where(own_col, wcol_ref[j], 0.0), axis=1, keepdims=True)
        wo_ref[j] = jnp.where(last, col, pltpu.roll(win_ref[j], wb - 1, 1))


def _nsa_decode(pt, q8, cache_t, crow, win_t, wrow, wcol, gate, nz, w1x, pex, w2x, kn, qpos, li, prev):
    db, n_pages = pt.shape
    page = cache_t.shape[-1]
    t = n_pages * page
    wb = win_t.shape[-1]
    nr = t // CMP_STRIDE
    assert nr == LANES and page == LANES and t % SLC_BLOCK == 0 and qpos // SLC_BLOCK < LANES and db % LANES == 0
    ek = (np.arange(LANES)[:, None] == (np.arange(t)[None, :] // SLC_BLOCK))
    ek = jnp.asarray(ek.astype(np.float32), BF16)
    ov = _overlap_mat(nr)
    two = jnp.concatenate([kn, kn]).reshape(1, LANES)
    row3 = lambda w: pl.BlockSpec((1, 1, w), lambda b, pt: (b, 0, 0))
    full = lambda a: pl.BlockSpec(a.shape, lambda b, pt: (0,) * a.ndim)
    win_spec = pl.BlockSpec((None, None, 2, LANES, wb), lambda b, pt: (li, b, 0, 0, 0))
    extra, extra_specs, alias = _layer_out(prev)
    depth = win_t.shape[0]
    return pl.pallas_call(
        functools.partial(_nsa_decode_kernel, qpos=qpos, li=li, fill=depth if prev is None else 0),
        grid_spec=pltpu.PrefetchScalarGridSpec(
            num_scalar_prefetch=1,
            grid=(db,),
            in_specs=[
                pl.BlockSpec((1, 8, LANES), lambda b, pt: (b, 0, 0)),
                pl.BlockSpec(memory_space=pl.ANY),
                row3(512),
                win_spec,
                row3(256),
                pl.BlockSpec((2, LANES, LANES), lambda b, pt: (0, 0, b // LANES)),
                row3(LANES), row3(512),
                full(w1x), full(pex), full(w2x), full(two), full(ov), full(ek),
            ] + extra_specs,
            out_specs=[row3(512), _layer_spec(prev, depth, li, (None, 2, LANES, wb), lambda b, pt: (b, 0, 0, 0))],
            scratch_shapes=[
                pltpu.VMEM((2, n_pages, 4, LANES, page), F32),
                pltpu.VMEM((2, t, LANES), F32),
                pltpu.SemaphoreType.DMA((2,)),
            ],
        ),
        out_shape=[jax.ShapeDtypeStruct((db, 1, NSA_WIDTH), F32), jax.ShapeDtypeStruct(win_t.shape, F32)],
        input_output_aliases={} if alias is None else {15: 1},
        compiler_params=_cparams(("arbitrary",)),
        name="nsa_decode",
    )(pt, q8, cache_t, crow, win_t, wrow, wcol, gate, nz, w1x, pex, w2x, two, ov, ek, *extra)


def _mem_kv_kernel(mem_ref, nw_ref, wk_ref, wv_ref, kn_ref, kv_ref, kvb_ref):
    ml = mem_ref.shape[0]
    x = mem_ref[...]
    m = _bf(x * lax.rsqrt(jnp.mean(x * x, axis=-1, keepdims=True) + EPS) * nw_ref[...])
    k = _dot(m, wk_ref[...])
    v = _dot(m, wv_ref[...])
    for h in range(MEM_HEADS):
        cols = slice(h * 128, (h + 1) * 128)
        kh = k[:, cols]
        kh = kh * lax.rsqrt(jnp.mean(kh * kh, axis=-1, keepdims=True) + EPS) * kn_ref[...]
        kv_ref[pl.ds(h, ml, stride=2 * MEM_HEADS), :] = kh
        kv_ref[pl.ds(MEM_HEADS + h, ml, stride=2 * MEM_HEADS), :] = v[:, cols]
        kvb_ref[:, cols] = _bf(kh)
    kvb_ref[:, MEM_WIDTH:2 * MEM_WIDTH] = _bf(v)


def _mem_kv(mem, nw, wk, wv, kn):
    ml = mem.shape[0]
    return pl.pallas_call(
        _mem_kv_kernel,
        out_shape=[jax.ShapeDtypeStruct((ml * 2 * MEM_HEADS, MEM_HD), F32),
                   jax.ShapeDtypeStruct((ml, 2 * MEM_WIDTH), BF16)],
        compiler_params=pltpu.CompilerParams(vmem_limit_bytes=VMEM_LIMIT),
        name="mem_kv",
    )(mem, nw.reshape(1, -1), wk, wv, kn.reshape(1, -1))


TAIL_T = 256


def _out_proj(x_ref, ret_ref, ssd_ref, nsa_ref, wout_ref):
    return (x_ref[...] + _dot(_bf(ret_ref[...]), wout_ref[0:512, :])
            + _dot(_bf(ssd_ref[...]), wout_ref[512:1536, :])
            + _dot(_bf(nsa_ref[...]), wout_ref[1536:2048, :]))


def _cross_q(x1, ncw_ref, wq_ref, qn_ref):
    h = _bf(x1 * lax.rsqrt(jnp.mean(x1 * x1, axis=-1, keepdims=True) + EPS) * ncw_ref[...])
    q = _dot(h, wq_ref[...])
    out = []
    for hd in range(MEM_HEADS):
        qh = q[:, hd * 128:(hd + 1) * 128]
        out.append(_bf(qh * lax.rsqrt(jnp.mean(qh * qh, axis=-1, keepdims=True) + EPS) * qn_ref[...]))
    return out


def _tail_kernel(x_ref, ret_ref, ssd_ref, nsa_ref, wout_ref, ncw_ref, wq_ref, qn_ref, kvb_ref, wo_ref, y_ref):
    x1 = _out_proj(x_ref, ret_ref, ssd_ref, nsa_ref, wout_ref)
    outs = []
    for hd, qh in enumerate(_cross_q(x1, ncw_ref, wq_ref, qn_ref)):
        s = _dot_nt(qh, kvb_ref[:, hd * 128:(hd + 1) * 128]) * (MEM_HD ** -0.5)
        s = s - jnp.max(s, axis=-1, keepdims=True)
        p = jnp.exp(s)
        p = p / jnp.sum(p, axis=-1, keepdims=True)
        outs.append(_dot(_bf(p), kvb_ref[:, MEM_WIDTH + hd * 128:MEM_WIDTH + (hd + 1) * 128]))
    y_ref[...] = x1 + _dot(_bf(jnp.concatenate(outs, axis=1)), wo_ref[...])


def _tail(x, ret, ssd, nsa, wout, ncw, wq, qn, kvb, wo):
    m = x.shape[0]
    t = min(m, TAIL_T)
    ml = kvb.shape[0]
    full2 = lambda shape: pl.BlockSpec(shape, lambda i: (0, 0))
    return pl.pallas_call(
        _tail_kernel,
        grid=(m // t,),
        in_specs=[
            pl.BlockSpec((t, D_MODEL), lambda i: (i, 0)),
            pl.BlockSpec((t, RET_WIDTH), lambda i: (i, 0)),
            pl.BlockSpec((t, SSD_WIDTH), lambda i: (i, 0)),
            pl.BlockSpec((t, NSA_WIDTH), lambda i: (i, 0)),
            full2((D_MODEL, D_MODEL)),
            full2((1, D_MODEL)),
            full2((D_MODEL, MEM_WIDTH)),
            full2((1, MEM_HD)),
            full2((ml, 2 * MEM_WIDTH)),
            full2((MEM_WIDTH, D_MODEL)),
        ],
        out_specs=pl.BlockSpec((t, D_MODEL), lambda i: (i, 0)),
        out_shape=jax.ShapeDtypeStruct((m, D_MODEL), F32),
        compiler_params=_cparams(("arbitrary",)),
        name="layer_tail",
    )(x, ret, ssd, nsa, wout, ncw.reshape(1, -1), wq, qn.reshape(1, -1), kvb, wo)


def _dec_tail_a_kernel(x_ref, ret_ref, ssd_ref, nsa_ref, wout_ref, ncw_ref, wq_ref, qn_ref, x1_ref, q_ref):
    x1 = _out_proj(x_ref, ret_ref, ssd_ref, nsa_ref, wout_ref)
    x1_ref[...] = x1
    for hd, qh in enumerate(_cross_q(x1, ncw_ref, wq_ref, qn_ref)):
        q_ref[:, hd * 128:(hd + 1) * 128] = qh


def _dec_tail_a(x, ret, ssd, nsa, wout, ncw, wq, qn):
    m = x.shape[0]
    return pl.pallas_call(
        _dec_tail_a_kernel,
        out_shape=[jax.ShapeDtypeStruct((m, D_MODEL), F32), jax.ShapeDtypeStruct((m, MEM_WIDTH), BF16)],
        compiler_params=pltpu.CompilerParams(vmem_limit_bytes=VMEM_LIMIT),
        name="dec_tail_a",
    )(x, ret, ssd, nsa, wout, ncw.reshape(1, -1), wq, qn.reshape(1, -1))


def _dec_xattn_kernel(q_ref, mem_ref, x1_ref, wo_ref, y_ref):
    ml = mem_ref.shape[-2] // (2 * MEM_HEADS)
    lane = lax.broadcasted_iota(jnp.int32, (8, MEM_WIDTH), 1)
    row = lax.broadcasted_iota(jnp.int32, (8, MEM_WIDTH), 0)
    own = (lane >> 7) == row
    outs = []
    for b in range(DEC_BB):
        q4 = _bf(jnp.where(own, jnp.broadcast_to(q_ref[b:b + 1, :].astype(F32), (8, MEM_WIDTH)), 0.0))
        kb = jnp.concatenate([_bf(mem_ref[b, pl.ds(h, ml, stride=2 * MEM_HEADS), :]) for h in range(MEM_HEADS)], axis=1)
        vb = jnp.concatenate([_bf(mem_ref[b, pl.ds(MEM_HEADS + h, ml, stride=2 * MEM_HEADS), :])
                              for h in range(MEM_HEADS)], axis=1)
        s = _dot_nt(q4, kb) * (MEM_HD ** -0.5)
        s = s - jnp.max(s, axis=-1, keepdims=True)
        p = jnp.exp(s)
        p = p / jnp.sum(p, axis=-1, keepdims=True)
        o4 = _dot(_bf(p), vb)
        outs.append(jnp.sum(jnp.where(own, o4, 0.0), axis=0, keepdims=True))
    y_ref[...] = x1_ref[...] + _dot(_bf(jnp.concatenate(outs, axis=0)), wo_ref[...])


def _dec_xattn(q, mem, x1, wo, li):
    db = q.shape[0]
    rows = mem.shape[2]
    return pl.pallas_call(
        _dec_xattn_kernel,
        grid=(db // DEC_BB,),
        in_specs=[
            pl.BlockSpec((DEC_BB, MEM_WIDTH), lambda i: (i, 0)),
            pl.BlockSpec((None, DEC_BB, rows, MEM_HD), lambda i: (li, i, 0, 0)),
            pl.BlockSpec((DEC_BB, D_MODEL), lambda i: (i, 0)),
            pl.BlockSpec((MEM_WIDTH, D_MODEL), lambda i: (0, 0)),
        ],
        out_specs=pl.BlockSpec((DEC_BB, D_MODEL), lambda i: (i, 0)),
        out_shape=jax.ShapeDtypeStruct((db, D_MODEL), F32),
        compiler_params=_cparams(("arbitrary",)),
        name="dec_xattn",
    )(q, mem, x1, wo)


def _prep_w_tail(wt):
    k = wt.shape[1]
    nq = wt[4624:5136].reshape(NSA_HEADS, NSA_HD, k)
    z = jnp.zeros_like(nq)
    nq_pad = jnp.concatenate([
        jnp.concatenate([nq[:4], z[:4]], axis=1),
        jnp.concatenate([z[4:], nq[4:]], axis=1)], axis=0).reshape(NSA_HEADS * LANES, k)
    padr = lambda a: jnp.pad(a, ((0, LANES - a.shape[0]), (0, 0)))
    return jnp.concatenate([
        wt[5928:6440],
        nq_pad,
        wt[5136:5904],
        padr(wt[4608:4624]),
        padr(wt[5904:5928]),
    ], axis=0)


def _rope_tables(pos, head_dim, rows):
    half = head_dim // 2
    inv = jnp.exp(-math.log(ROPE_THETA) * jnp.arange(half, dtype=F32) / half)
    ang = pos.astype(F32)[:, None] * inv[None, :]
    cos = jnp.cos(ang)
    sin = jnp.sin(ang)
    reps = LANES // head_dim
    cos_t = jnp.tile(jnp.concatenate([cos, cos], axis=-1), (1, reps))
    sin_t = jnp.tile(jnp.concatenate([-sin, sin], axis=-1), (1, reps))
    if cos_t.shape[0] != rows:
        cos_t = jnp.broadcast_to(cos_t, (rows, LANES))
        sin_t = jnp.broadcast_to(sin_t, (rows, LANES))
    return cos_t, sin_t


def kernel(x_prompt, x_sample, mem_prompt, state_ret, state_ssm, state_conv, cache_nsa_kv, cache_win_kv,
           cache_mem_kv, page_table, norm_mix, w_in, ssd_conv_w, ssd_conv_b, ssd_dt_bias, ssd_a_log, ssd_d,
           ssd_norm, nsa_q_norm, nsa_kc_norm, nsa_ks_norm, nsa_kw_norm, nsa_cmp_pe, nsa_cmp_w1, nsa_cmp_w2,
           w_out, norm_cross, norm_mem, mem_wq, mem_wk, mem_wv, mem_q_norm, mem_k_norm, mem_wo):
    b, l, _ = x_prompt.shape
    assert b == 1
    db, dl, _ = x_sample.shape
    assert dl == 1 and db % DEC_BB == 0
    depth = w_in.shape[0]
    n_pages = page_table.shape[1]
    page = cache_nsa_kv.shape[2]
    past_len = n_pages * page
    wbuf = cache_win_kv.shape[2]
    ml = cache_mem_kv.shape[2]
    wp = min(WINDOW, l)
    pos_p = jnp.arange(l, dtype=jnp.int32)
    pos_s = jnp.full((1,), past_len, dtype=jnp.int32)
    cos128, sin128 = _rope_tables(pos_p, RET_DK, l)
    cos64, sin64 = _rope_tables(pos_p, NSA_HD, l)
    cos128s, sin128s = _rope_tables(pos_s, RET_DK, db)
    cos64s, sin64s = _rope_tables(pos_s, NSA_HD, db)

    xp = x_prompt[0]
    xs = x_sample[:, 0, :]
    w_in_t = jnp.swapaxes(w_in, 1, 2)
    cache_t = jnp.transpose(cache_nsa_kv, (0, 1, 3, 4, 5, 2)).reshape(depth, -1, 4, LANES, page)
    win_t = jnp.transpose(cache_win_kv, (0, 1, 3, 4, 5, 2)).reshape(depth, db, 2, LANES, wbuf)
    conv_t = jnp.transpose(state_conv, (0, 2, 1, 3))
    mem_rows = cache_mem_kv.reshape(depth, db, ml * 2 * MEM_HEADS, MEM_HD)
    untok = lambda a: jnp.moveaxis(a.reshape(a.shape[:-2] + (NSA_KV_HEADS, NSA_HD, a.shape[-1])), -1, -4)
    ret_p, ssm_p, conv_p, win_p, mem_p, conv_s = [], [], [], [], [], []
    cache_p = cache_s = ret_s_all = ssm_s_all = win_s_all = None
    for li in range(depth):
        w_tail = _prep_w_tail(w_in_t[li])
        w_out_b = w_out[li].astype(BF16)
        wq_b = mem_wq[li].astype(BF16)
        wo_b = mem_wo[li].astype(BF16)
        w1x, pex, w2x = _compress_weights(nsa_cmp_pe[li], nsa_cmp_w1[li], nsa_cmp_w2[li])
        ssd_w = (ssd_conv_w[li], ssd_conv_b[li], ssd_dt_bias[li], ssd_a_log[li], ssd_d[li], ssd_norm[li])
        nsa_n = (nsa_q_norm[li], nsa_ks_norm[li], nsa_kw_norm[li])
        proj = _inproj(xp, norm_mix[li], w_in_t, w_tail, li)
        ret_out, ret_s = _ret_prompt(proj, cos128, sin128)
        ssd_out, ssm_h, conv8 = _ssd_prompt(proj, *ssd_w)
        qb, cache, _, cache_p, win_tp, kvb = _nsa_prep(proj, cos64, sin64, li, depth, cache_p, *nsa_n)
        rk = cache[:, 0:128].reshape(l // CMP_STRIDE, CMP_STRIDE * LANES)
        rv = cache[:, 128:256].reshape(l // CMP_STRIDE, CMP_STRIDE * LANES)
        kc_b, vc2 = _compress(rk, rv, w1x, pex, w2x, nsa_kc_norm[li])
        nsa_out = _nsa_prompt(qb, kc_b, vc2, kvb, proj)
        mkv, mkv_b = _mem_kv(mem_prompt[0], norm_mem[li], mem_wk[li].astype(BF16), mem_wv[li].astype(BF16),
                             mem_k_norm[li])
        xp = _tail(xp, ret_out, ssd_out, nsa_out, w_out_b, norm_cross[li], wq_b, mem_q_norm[li], mkv_b, wo_b)
        ret_p.append(ret_s[None])
        ssm_p.append(ssm_h[None])
        conv_p.append(conv8[None, 8 - (SSD_CONV - 1):])
        win_p.append(untok(win_tp[:, :, l - wp:])[None])
        mem_p.append(mkv.reshape(1, ml, 2, MEM_HEADS, MEM_HD))
        sproj = _inproj(xs, norm_mix[li], w_in_t, w_tail, li)
        s_ret_out, ret_s_all = _ret_decode(sproj, cos128s, sin128s, state_ret, li, ret_s_all)
        s_ssd_out, s_conv, ssm_s_all = _ssd_decode(sproj, conv_t, state_ssm, li, ssm_s_all, *ssd_w)
        s_qb, s_cache, s_wrow, cache_s, s_wcol, _ = _nsa_prep(sproj, cos64s, sin64s, li, depth, cache_s, *nsa_n)
        s_nsa, win_s_all = _nsa_decode(
            page_table, s_qb.reshape(db, NSA_HEADS, LANES), cache_t, s_cache[:, None, :], win_t,
            s_wrow[:, None, :], s_wcol, sproj[:, None, C_GATE:C_GATE + LANES],
            sproj[:, None, C_NZ:C_NZ + NSA_WIDTH], w1x, pex, w2x, nsa_kc_norm[li], past_len, li, win_s_all)
        x1, s_q = _dec_tail_a(xs, s_ret_out, s_ssd_out, s_nsa[:, 0, :], w_out_b, norm_cross[li], wq_b,
                              mem_q_norm[li])
        xs = _dec_xattn(s_q, mem_rows, x1, wo_b, li)
        conv_s.append(jnp.transpose(s_conv, (1, 0, 2)))
    stack = jnp.stack
    return (xp[None], xs[:, None, :], stack(ret_p), ret_s_all, stack(ssm_p), ssm_s_all, stack(conv_p), stack(conv_s),
            untok(cache_p)[:, None], untok(cache_s)[:, :, None], stack(win_p), untok(win_s_all), stack(mem_p))
```

```python
import functools
import math

import numpy as np
import jax
import jax.numpy as jnp
from jax import lax
from jax.experimental import pallas as pl
from jax.experimental.pallas import tpu as pltpu

F32 = jnp.float32
BF16 = jnp.bfloat16

D_MODEL = 2048
RET_HEADS = 4
RET_DK = 128
RET_WIDTH = 512
CHUNK = 128
SSD_WIDTH = 1024
SSD_HEADDIM = 64
SSD_HEADS = 16
SSD_GROUPS = 2
SSD_STATE = 128
SSD_CONV = 4
SSD_CONV_DIM = 1536
NSA_WIDTH = 512
NSA_HEADS = 8
NSA_HD = 64
NSA_KV_HEADS = 2
CMP_BLOCK = 32
CMP_STRIDE = 16
SLC_BLOCK = 64
SLC_TOPN = 16
N_LOCAL = 2
WINDOW = 512
FORCE_SCORE = 1.0e6
MEM_HEADS = 4
MEM_HD = 128
MEM_WIDTH = 512
ROPE_THETA = 10000.0
EPS = 1e-6
NEG = -1.0e30

C_SZ = 2048
C_XBC = 3072
C_NZ = 4608
C_NQ = 5120
C_NKV = 6144
C_SDT = 6912
C_GATE = 7040
N_PROJ = 7168

LANES = 128
VMEM_LIMIT = 56 * 1024 * 1024


def _cparams(sem):
    return pltpu.CompilerParams(dimension_semantics=sem, vmem_limit_bytes=VMEM_LIMIT)


def _bf(x):
    return x.astype(BF16)


def _dot(a, b):
    return jnp.dot(a, b, preferred_element_type=F32)


def _dot_nt(a, b):
    return lax.dot_general(a, b, (((1,), (1,)), ((), ())), preferred_element_type=F32)


def _dot_tn(a, b):
    return lax.dot_general(a, b, (((0,), (0,)), ((), ())), preferred_element_type=F32)


def _split3(a):
    hi = a.astype(BF16)
    r = a - hi.astype(F32)
    mid = r.astype(BF16)
    lo = (r - mid.astype(F32)).astype(BF16)
    return hi, mid, lo


def _split2(a):
    hi = a.astype(BF16)
    lo = (a - hi.astype(F32)).astype(BF16)
    return hi, lo


def _dot_split(a, b_bf16):
    hi, mid, lo = _split3(a)
    return _dot(hi, b_bf16) + _dot(mid, b_bf16) + _dot(lo, b_bf16)


def _dot_tri(tri_bf16, a):
    hi, mid, lo = _split3(a)
    return _dot(tri_bf16, hi) + _dot(tri_bf16, mid) + _dot(tri_bf16, lo)


def _silu(x):
    return x * jax.nn.sigmoid(x)


def _softplus(x):
    return jnp.maximum(x, 0.0) + jnp.log1p(jnp.exp(-jnp.abs(x)))


def _rope128(x, cos, sin):
    return x * cos + pltpu.roll(x, 64, 1) * sin


def _rope64(x, cos, sin):
    lane = lax.broadcasted_iota(jnp.int32, x.shape, 1)
    first = (lane & 63) < 32
    partner = jnp.where(first, pltpu.roll(x, 96, 1), pltpu.roll(x, 32, 1))
    return x * cos + partner * sin


def _half_mean_mat():
    r = lax.broadcasted_iota(jnp.int32, (LANES, LANES), 0)
    c = lax.broadcasted_iota(jnp.int32, (LANES, LANES), 1)
    return jnp.where((r >> 6) == (c >> 6), 1.0 / 64.0, 0.0).astype(BF16)


def _rms64(x, w, bd):
    ms = _dot_split(x * x, bd)
    return x * lax.rsqrt(ms + EPS) * w


def _pad_lanes(v, n=LANES):
    v = v.reshape(1, -1)
    return jnp.pad(v, ((0, 0), (0, n - v.shape[1])))


def _layer_out(prev):
    if prev is None:
        return [], [], None
    return [prev], [pl.BlockSpec(memory_space=pl.ANY)], prev


def _layer_spec(prev, depth, li, block, index):
    if prev is None:
        return pl.BlockSpec((depth,) + block, lambda *a: (0,) + index(*a))
    return pl.BlockSpec((None,) + block, lambda *a: (li,) + index(*a))


def _layer_view(ref, li, fill_depth):
    if not fill_depth:
        return ref
    for d in range(fill_depth):
        if d != li:
            ref[d] = jnp.zeros(ref.shape[1:], ref.dtype)
    return ref.at[li]


def _row_only(x, b):
    rid = lax.broadcasted_iota(jnp.int32, (x.shape[0], 1), 0)
    return jnp.where(rid == b, x, jnp.zeros_like(x))


N_MAIN = 4608
PROJ_TN = 512


def _inproj_kernel(x_ref, nw_ref, wa_ref, wb_ref, o_ref, h_ref):
    j = pl.program_id(1)

    @pl.when(j == 0)
    def _():
        nw = nw_ref[...]
        rows = min(128, x_ref.shape[0])

        def body(i, c):
            r = pl.ds(pl.multiple_of(i * rows, rows), rows)
            x = x_ref[r, :]
            ms = jnp.mean(x * x, axis=-1, keepdims=True)
            h_ref[r, :] = (x * lax.rsqrt(ms + EPS) * nw).astype(BF16)
            return c

        lax.fori_loop(0, x_ref.shape[0] // rows, body, 0)

    @pl.when(j < N_MAIN // PROJ_TN)
    def _():
        o_ref[...] = _dot_nt(h_ref[...], _bf(wa_ref[...]))

    @pl.when(j >= N_MAIN // PROJ_TN)
    def _():
        o_ref[...] = _dot_nt(h_ref[...], _bf(wb_ref[...]))


def _inproj(x, nw, w_t, w_tail, li):
    m = x.shape[0]
    tm = min(m, 1024)
    nja = N_MAIN // PROJ_TN
    return pl.pallas_call(
        _inproj_kernel,
        grid=(m // tm, N_PROJ // PROJ_TN),
        in_specs=[
            pl.BlockSpec((tm, D_MODEL), lambda i, j: (i, 0)),
            pl.BlockSpec((1, D_MODEL), lambda i, j: (0, 0)),
            pl.BlockSpec((None, PROJ_TN, D_MODEL), lambda i, j: (li, jnp.minimum(j, nja - 1), 0)),
            pl.BlockSpec((PROJ_TN, D_MODEL), lambda i, j: (jnp.maximum(j - nja, 0), 0)),
        ],
        out_specs=pl.BlockSpec((tm, PROJ_TN), lambda i, j: (i, j)),
        out_shape=jax.ShapeDtypeStruct((m, N_PROJ), F32),
        scratch_shapes=[pltpu.VMEM((tm, D_MODEL), BF16)],
        compiler_params=_cparams(("arbitrary", "arbitrary")),
        name="inproj",
    )(x, nw.reshape(1, D_MODEL), w_t, w_tail)


RET_T = 512


def _ret_prompt_kernel(q_ref, k_ref, v_ref, g_ref, cos_ref, sin_ref, dec_ref, qd_ref, kd_ref, cd_ref,
                       o_ref, so_ref, s_ref):
    i = pl.program_id(0)

    @pl.when(i == 0)
    def _():
        s_ref[...] = jnp.zeros(s_ref.shape, F32)

    for c in range(RET_T // CHUNK):
        rows = slice(c * CHUNK, (c + 1) * CHUNK)
        cos = cos_ref[rows, :]
        sin = sin_ref[rows, :]
        for h in range(RET_HEADS):
            cols = slice(h * 128, (h + 1) * 128)
            q = _rope128(q_ref[rows, cols], cos, sin)
            k = _rope128(k_ref[rows, cols], cos, sin) * (RET_DK ** -0.5)
            v = v_ref[rows, cols]
            s = s_ref[h]
            qb = _bf(q)
            vb = _bf(v)
            att = _dot_nt(qb, _bf(k)) * dec_ref[h]
            o = _dot(_bf(att), vb) + _dot(qb, _bf(s)) * qd_ref[h]
            s_ref[h] = s * cd_ref[h] + _dot_tn(_bf(k * kd_ref[h]), vb)
            r = o * lax.rsqrt(jnp.mean(o * o, axis=-1, keepdims=True) + EPS)
            o_ref[rows, cols] = r * _silu(g_ref[rows, cols])

    @pl.when(i == pl.num_programs(0) - 1)
    def _():
        so_ref[...] = s_ref[...]


def _ret_gamma():
    return 1.0 - np.exp2(-5.0 - np.arange(RET_HEADS, dtype=np.float64))


def _ret_consts():
    lg = np.log(_ret_gamma())
    idx = np.arange(CHUNK, dtype=np.float64)
    diff = idx[:, None] - idx[None, :]
    dec = np.where(diff[None] >= 0, np.exp(lg[:, None, None] * np.maximum(diff, 0.0)[None]), 0.0)
    qd = np.exp(lg[:, None] * (idx + 1.0)[None])
    kd = np.exp(lg[:, None] * (CHUNK - 1.0 - idx)[None])
    cd = np.exp(lg * CHUNK)
    bc = lambda a: np.ascontiguousarray(np.broadcast_to(a[:, :, None], (RET_HEADS, CHUNK, LANES)))
    return (jnp.asarray(dec, F32), jnp.asarray(bc(qd), F32), jnp.asarray(bc(kd), F32),
            jnp.asarray(np.broadcast_to(cd[:, None, None], (RET_HEADS, 1, LANES)).copy(), F32))


def _ret_prompt(proj, cos, sin):
    l = proj.shape[0]
    dec, qd, kd, cd = _ret_consts()
    full3 = lambda shape: pl.BlockSpec(shape, lambda i: (0, 0, 0))
    return pl.pallas_call(
        _ret_prompt_kernel,
        grid=(l // RET_T,),
        in_specs=[
            pl.BlockSpec((RET_T, 512), lambda i: (i, 0)),
            pl.BlockSpec((RET_T, 512), lambda i: (i, 1)),
            pl.BlockSpec((RET_T, 512), lambda i: (i, 2)),
            pl.BlockSpec((RET_T, 512), lambda i: (i, 3)),
            pl.BlockSpec((RET_T, LANES), lambda i: (i, 0)),
            pl.BlockSpec((RET_T, LANES), lambda i: (i, 0)),
            full3((RET_HEADS, CHUNK, CHUNK)),
            full3((RET_HEADS, CHUNK, LANES)),
            full3((RET_HEADS, CHUNK, LANES)),
            full3((RET_HEADS, 1, LANES)),
        ],
        out_specs=[
            pl.BlockSpec((RET_T, 512), lambda i: (i, 0)),
            full3((RET_HEADS, RET_DK, RET_DK)),
        ],
        out_shape=[
            jax.ShapeDtypeStruct((l, RET_WIDTH), F32),
            jax.ShapeDtypeStruct((RET_HEADS, RET_DK, RET_DK), F32),
        ],
        scratch_shapes=[pltpu.VMEM((RET_HEADS, RET_DK, RET_DK), F32)],
        compiler_params=_cparams(("arbitrary",)),
        name="ret_prompt",
    )(proj, proj, proj, proj, cos, sin, dec, qd, kd, cd)


DEC_BB = 8


def _ret_decode_kernel(q_ref, k_ref, v_ref, g_ref, cos_ref, sin_ref, gam_ref, s_ref, *rest, li, fill):
    o_ref, so_ref = rest[-2:]
    so_ref = _layer_view(so_ref, li, fill)
    cos = cos_ref[...]
    sin = sin_ref[...]
    for h in range(RET_HEADS):
        cols = slice(h * 128, (h + 1) * 128)
        qb = _bf(_rope128(q_ref[:, cols], cos, sin))
        kb = _bf(_rope128(k_ref[:, cols], cos, sin) * (RET_DK ** -0.5))
        vb = _bf(v_ref[:, cols])
        gam = gam_ref[h]
        qk = jnp.sum(qb.astype(F32) * kb.astype(F32), axis=-1, keepdims=True)
        o = _bf(qk).astype(F32) * vb.astype(F32)
        rows = []
        for b in range(DEC_BB):
            s = s_ref[b, h]
            rows.append(_dot(qb, _bf(s))[b:b + 1])
            so_ref[b, h] = s * gam + _dot_tn(_row_only(kb, b), vb)
        o = o + jnp.concatenate(rows, axis=0) * gam
        r = o * lax.rsqrt(jnp.mean(o * o, axis=-1, keepdims=True) + EPS)
        o_ref[:, cols] = r * _silu(g_ref[:, cols])


def _ret_decode(proj, cos, sin, state, li, prev):
    db = proj.shape[0]
    gam = jnp.asarray(np.broadcast_to(_ret_gamma()[:, None, None], (RET_HEADS, 1, LANES)).copy(), F32)
    extra, extra_specs, alias = _layer_out(prev)
    depth = state.shape[0]
    blk = (DEC_BB, RET_HEADS, RET_DK, RET_DK)
    st_spec = pl.BlockSpec((None,) + blk, lambda i: (li, i, 0, 0, 0))
    return pl.pallas_call(
        functools.partial(_ret_decode_kernel, li=li, fill=depth if prev is None else 0),
        grid=(db // DEC_BB,),
        in_specs=[
            pl.BlockSpec((DEC_BB, 512), lambda i: (i, 0)),
            pl.BlockSpec((DEC_BB, 512), lambda i: (i, 1)),
            pl.BlockSpec((DEC_BB, 512), lambda i: (i, 2)),
            pl.BlockSpec((DEC_BB, 512), lambda i: (i, 3)),
            pl.BlockSpec((DEC_BB, LANES), lambda i: (i, 0)),
            pl.BlockSpec((DEC_BB, LANES), lambda i: (i, 0)),
            pl.BlockSpec((RET_HEADS, 1, LANES), lambda i: (0, 0, 0)),
            st_spec,
        ] + extra_specs,
        out_specs=[pl.BlockSpec((DEC_BB, 512), lambda i: (i, 0)),
                   _layer_spec(prev, depth, li, blk, lambda i: (i, 0, 0, 0))],
        out_shape=[
            jax.ShapeDtypeStruct((db, RET_WIDTH), F32),
            jax.ShapeDtypeStruct(state.shape, F32),
        ],
        input_output_aliases={} if alias is None else {8: 1},
        compiler_params=_cparams(("arbitrary",)),
        name="ret_decode",
    )(proj, proj, proj, proj, cos, sin, gam, state, *extra)


def _head_expand_mat():
    r = lax.broadcasted_iota(jnp.int32, (LANES, SSD_WIDTH), 0)
    c = lax.broadcasted_iota(jnp.int32, (LANES, SSD_WIDTH), 1)
    return jnp.where(r == (c >> 6), 1.0, 0.0).astype(BF16)


def _ssd_prompt_kernel(z_ref, xbc_ref, dt_ref, cw_ref, cb_ref, dtb_ref, alog_ref, dexp_ref, nw_ref,
                       y_ref, ho_ref, co_ref, ext_ref, ht_ref):
    i = pl.program_id(0)

    @pl.when(i == 0)
    def _():
        ext_ref[0:8, :] = jnp.zeros((8, SSD_CONV_DIM), F32)
        ht_ref[...] = jnp.zeros(ht_ref.shape, F32)

    u = xbc_ref[...]
    ext_ref[8:8 + CHUNK, :] = u
    cw = cw_ref[...]
    conv = (cb_ref[...] + cw[3:4, :] * u + cw[2:3, :] * ext_ref[7:7 + CHUNK, :]
            + cw[1:2, :] * ext_ref[6:6 + CHUNK, :] + cw[0:1, :] * ext_ref[5:5 + CHUNK, :])
    ext_ref[0:8, :] = u[CHUNK - 8:CHUNK, :]
    xbc = _silu(conv)
    xs = xbc[:, 0:SSD_WIDTH]

    dt = _softplus(dt_ref[...] + dtb_ref[...])
    a = dt * (-jnp.exp(alog_ref[...]))
    ri = lax.broadcasted_iota(jnp.int32, (CHUNK, CHUNK), 0)
    ci = lax.broadcasted_iota(jnp.int32, (CHUNK, CHUNK), 1)
    causal = ri >= ci
    tri = jnp.where(causal, 1.0, 0.0).astype(BF16)
    cum = _dot_tri(tri, a)
    cum_t = cum.T
    dt_t = dt.T
    cum_last = cum[CHUNK - 1:CHUNK, :]
    eh = _head_expand_mat()
    ecum_x = _dot_split(jnp.exp(cum), eh)
    wgt_x = _dot_split(jnp.exp(cum_last - cum) * dt, eh)
    elast_x = _dot_split(jnp.broadcast_to(jnp.exp(cum_last), (8, LANES)), eh)[0:1, :]

    lane = lax.broadcasted_iota(jnp.int32, (CHUNK, LANES), 1)
    lo_half = lane < 64
    xw = _bf(xs * wgt_x)
    y_parts = []
    ch_parts = []
    for g in range(SSD_GROUPS):
        bg = xbc[:, SSD_WIDTH + g * 128:SSD_WIDTH + (g + 1) * 128]
        cg = xbc[:, SSD_WIDTH + 256 + g * 128:SSD_WIDTH + 256 + (g + 1) * 128]
        cgb = _bf(cg)
        cb = _dot_nt(cgb, _bf(bg))
        ht = ht_ref[g]
        ch_parts.append(_dot(cgb, _bf(ht)))
        for k in range(4):
            h0 = g * 8 + 2 * k
            xp = _bf(xs[:, h0 * 64:(h0 + 2) * 64])
            ys = []
            for hh in (h0, h0 + 1):
                seg = cum[:, hh:hh + 1] - cum_t[hh:hh + 1, :]
                lm = jnp.where(causal, jnp.exp(jnp.minimum(seg, 0.0)), 0.0)
                sc = cb * lm * dt_t[hh:hh + 1, :]
                ys.append(_dot(_bf(sc), xp))
            y_parts.append(jnp.where(lo_half, ys[0], ys[1]))
        bgt = _bf(bg.T)
        ht_ref[g] = ht * elast_x[:, g * 512:(g + 1) * 512] + _dot(bgt, xw[:, g * 512:(g + 1) * 512])
    y = jnp.concatenate(y_parts, axis=1) + jnp.concatenate(ch_parts, axis=1) * ecum_x + dexp_ref[...] * xs
    gated = y * _silu(z_ref[...])
    y_ref[...] = gated * lax.rsqrt(jnp.mean(gated * gated, axis=-1, keepdims=True) + EPS) * nw_ref[...]

    @pl.when(i == pl.num_programs(0) - 1)
    def _():
        co_ref[...] = u[CHUNK - 8:CHUNK, :]
        for g in range(SSD_GROUPS):
            htf = ht_ref[g]
            for k in range(4):
                h0 = g * 8 + 2 * k
                ho_ref[h0:h0 + 2] = htf[:, k * 128:(k + 1) * 128].T.reshape(2, SSD_HEADDIM, SSD_STATE)


def _ssd_prompt(proj, conv_w, conv_b, dt_bias, a_log, d, norm_w):
    l = proj.shape[0]
    full2 = lambda shape: pl.BlockSpec(shape, lambda i: (0, 0))
    return pl.pallas_call(
        _ssd_prompt_kernel,
        grid=(l // CHUNK,),
        in_specs=[
            pl.BlockSpec((CHUNK, SSD_WIDTH), lambda i: (i, C_SZ // SSD_WIDTH)),
            pl.BlockSpec((CHUNK, SSD_CONV_DIM), lambda i: (i, C_XBC // SSD_CONV_DIM)),
            pl.BlockSpec((CHUNK, LANES), lambda i: (i, C_SDT // LANES)),
            full2((SSD_CONV, SSD_CONV_DIM)),
            full2((1, SSD_CONV_DIM)),
            full2((1, LANES)),
            full2((1, LANES)),
            full2((1, SSD_WIDTH)),
            full2((1, SSD_WIDTH)),
        ],
        out_specs=[
            pl.BlockSpec((CHUNK, SSD_WIDTH), lambda i: (i, 0)),
            pl.BlockSpec((SSD_HEADS, SSD_HEADDIM, SSD_STATE), lambda i: (0, 0, 0)),
            full2((8, SSD_CONV_DIM)),
        ],
        out_shape=[
            jax.ShapeDtypeStruct((l, SSD_WIDTH), F32),
            jax.ShapeDtypeStruct((SSD_HEADS, SSD_HEADDIM, SSD_STATE), F32),
            jax.ShapeDtypeStruct((8, SSD_CONV_DIM), F32),
        ],
        scratch_shapes=[
            pltpu.VMEM((8 + CHUNK, SSD_CONV_DIM), F32),
            pltpu.VMEM((SSD_GROUPS, SSD_STATE, 512), F32),
        ],
        compiler_params=_cparams(("arbitrary",)),
        name="ssd_prompt",
    )(proj, proj, proj, conv_w, conv_b.reshape(1, -1), _pad_lanes(dt_bias), _pad_lanes(a_log),
      jnp.repeat(d, SSD_HEADDIM).reshape(1, -1), norm_w.reshape(1, -1))


def _ssd_decode_kernel(z_ref, xbc_ref, dt_ref, cs_ref, h_ref, cw_ref, cb_ref, dtb_ref, alog_ref, dexp_ref, nw_ref,
                       *rest, li, fill):
    y_ref, co_ref, ho_ref = rest[-3:]
    ho_ref = _layer_view(ho_ref, li, fill)
    u = xbc_ref[...]
    c0 = cs_ref[0]
    c1 = cs_ref[1]
    c2 = cs_ref[2]
    cw = cw_ref[...]
    conv = cb_ref[...] + cw[3:4, :] * u + cw[2:3, :] * c2 + cw[1:2, :] * c1 + cw[0:1, :] * c0
    co_ref[0] = c1
    co_ref[1] = c2
    co_ref[2] = u
    xbc = _silu(conv)
    xs = xbc[:, 0:SSD_WIDTH]
    dt = _softplus(dt_ref[...] + dtb_ref[...])
    ea = jnp.exp(dt * (-jnp.exp(alog_ref[...])))
    eh = _head_expand_mat()
    dt_x = _dot_split(dt, eh)
    ea_x = _dot_split(ea, eh)
    dtx = dt_x * xs
    ones = jnp.ones((DEC_BB, LANES), BF16)
    ych = [[None, None] for _ in range(DEC_BB)]
    cbs = []
    for g in range(SSD_GROUPS):
        gc = slice(g * 512, (g + 1) * 512)
        bg = xbc[:, SSD_WIDTH + g * 128:SSD_WIDTH + (g + 1) * 128]
        cg = xbc[:, SSD_WIDTH + 256 + g * 128:SSD_WIDTH + 256 + (g + 1) * 128]
        cgb = _bf(cg)
        cbs.append(jnp.sum(cgb.astype(F32) * _bf(bg).astype(F32), axis=-1, keepdims=True))
        b_hi, b_lo = _split2(bg)
        for b in range(DEC_BB):
            hs = h_ref[b, g * 8:(g + 1) * 8].reshape(512, SSD_STATE)
            ych[b][g] = _dot_nt(cgb, _bf(hs))[b:b + 1]
            e_hi, e_lo = _split2(_row_only(ea_x[:, gc], b))
            decay = _dot_tn(e_hi, ones) + _dot_tn(e_lo, ones)
            x_hi, x_lo = _split2(_row_only(dtx[:, gc], b))
            upd = _dot_tn(x_hi, b_hi) + _dot_tn(x_hi, b_lo) + _dot_tn(x_lo, b_hi)
            ho_ref[b, g * 8:(g + 1) * 8] = (hs * decay + upd).reshape(8, SSD_HEADDIM, SSD_STATE)
    ych = jnp.concatenate([jnp.concatenate(r, axis=1) for r in ych], axis=0)
    lane = lax.broadcasted_iota(jnp.int32, (DEC_BB, SSD_WIDTH), 1)
    cbx = jnp.where(lane < 512, cbs[0], cbs[1])
    y = dt_x * cbx * xs + ych * ea_x + dexp_ref[...] * xs
    gated = y * _silu(z_ref[...])
    y_ref[...] = gated * lax.rsqrt(jnp.mean(gated * gated, axis=-1, keepdims=True) + EPS) * nw_ref[...]


def _ssd_decode(proj, conv_state_t, ssm_state, li, prev, conv_w, conv_b, dt_bias, a_log, d, norm_w):
    db = proj.shape[0]
    full2 = lambda shape: pl.BlockSpec(shape, lambda i: (0, 0))
    extra, extra_specs, alias = _layer_out(prev)
    depth = ssm_state.shape[0]
    blk = (DEC_BB, SSD_HEADS, SSD_HEADDIM, SSD_STATE)
    st_spec = pl.BlockSpec((None,) + blk, lambda i: (li, i, 0, 0, 0))
    return pl.pallas_call(
        functools.partial(_ssd_decode_kernel, li=li, fill=depth if prev is None else 0),
        grid=(db // DEC_BB,),
        in_specs=[
            pl.BlockSpec((DEC_BB, SSD_WIDTH), lambda i: (i, C_SZ // SSD_WIDTH)),
            pl.BlockSpec((DEC_BB, SSD_CONV_DIM), lambda i: (i, C_XBC // SSD_CONV_DIM)),
            pl.BlockSpec((DEC_BB, LANES), lambda i: (i, C_SDT // LANES)),
            pl.BlockSpec((None, SSD_CONV - 1, DEC_BB, SSD_CONV_DIM), lambda i: (li, 0, i, 0)),
            st_spec,
            full2((SSD_CONV, SSD_CONV_DIM)),
            full2((1, SSD_CONV_DIM)),
            full2((1, LANES)),
            full2((1, LANES)),
            full2((1, SSD_WIDTH)),
            full2((1, SSD_WIDTH)),
        ] + extra_specs,
        out_specs=[
            pl.BlockSpec((DEC_BB, SSD_WIDTH), lambda i: (i, 0)),
            pl.BlockSpec((SSD_CONV - 1, DEC_BB, SSD_CONV_DIM), lambda i: (0, i, 0)),
            _layer_spec(prev, depth, li, blk, lambda i: (i, 0, 0, 0)),
        ],
        out_shape=[
            jax.ShapeDtypeStruct((db, SSD_WIDTH), F32),
            jax.ShapeDtypeStruct((SSD_CONV - 1, db, SSD_CONV_DIM), F32),
            jax.ShapeDtypeStruct(ssm_state.shape, F32),
        ],
        input_output_aliases={} if alias is None else {11: 2},
        compiler_params=_cparams(("arbitrary",)),
        name="ssd_decode",
    )(proj, proj, proj, conv_state_t, ssm_state, conv_w, conv_b.reshape(1, -1), _pad_lanes(dt_bias),
      _pad_lanes(a_log), jnp.repeat(d, SSD_HEADDIM).reshape(1, -1), norm_w.reshape(1, -1), *extra)


def _value_variants(v):
    lane = lax.broadcasted_iota(jnp.int32, v.shape, 1)
    lo = lane < 64
    sw = pltpu.roll(v, 64, 1)
    one = jnp.ones_like(v)
    return [jnp.where(lo, v, one), jnp.where(lo, one, sw), jnp.where(lo, sw, one), jnp.where(lo, one, v)]


def _nsa_prep_kernel(nq_ref, nkv_ref, cos_ref, sin_ref, qn_ref, ksn_ref, kwn_ref, *rest, li, fill):
    qb_ref, cache_ref, win_ref, cache_t_ref, win_t_ref, kb_ref, vt_ref = rest[-7:]
    cache_t_ref = _layer_view(cache_t_ref, li, fill)
    cos = cos_ref[...]
    sin = sin_ref[...]
    bd = _half_mean_mat()
    qn = qn_ref[...]
    for h in range(NSA_HEADS):
        cols = slice(h * 128, (h + 1) * 128)
        x = nq_ref[:, cols]
        ms = jnp.sum(x * x, axis=-1, keepdims=True) * (1.0 / NSA_HD)
        qh = _rope64(x * lax.rsqrt(ms + EPS) * qn, cos, sin)
        qb_ref[:, cols] = _bf(qh * (NSA_HD ** -0.5))
    kc = _rope64(nkv_ref[:, 0:128], cos, sin)
    vc = nkv_ref[:, 128:256]
    ks = _rope64(_rms64(nkv_ref[:, 256:384], ksn_ref[...], bd), cos, sin)
    vs = nkv_ref[:, 384:512]
    kw = _rope64(_rms64(nkv_ref[:, 512:640], kwn_ref[...], bd), cos, sin)
    vw = nkv_ref[:, 640:768]
    for i, v in enumerate((kc, vc, ks, vs)):
        cache_ref[:, i * 128:(i + 1) * 128] = v
        cache_t_ref[i] = v.T
    for i, v in enumerate((kw, vw)):
        win_ref[:, i * 128:(i + 1) * 128] = v
        win_t_ref[i] = v.T
    kb_ref[0] = _bf(ks)
    kb_ref[1] = _bf(kw)
    for i, v in enumerate(_value_variants(vs) + _value_variants(vw)):
        vt_ref[i] = _bf(v.T)


def _nsa_prep(proj, cos, sin, li, depth, prev, qn, ksn, kwn):
    m = proj.shape[0]
    t = min(m, 256)
    extra, extra_specs, alias = _layer_out(prev)
    two = lambda w: jnp.concatenate([w, w]).reshape(1, LANES)
    full2 = lambda shape: pl.BlockSpec(shape, lambda i: (0, 0))
    return pl.pallas_call(
        functools.partial(_nsa_prep_kernel, li=li, fill=depth if prev is None else 0),
        grid=(m // t,),
        in_specs=[
            pl.BlockSpec((t, 1024), lambda i: (i, C_NQ // 1024)),
            pl.BlockSpec((t, 768), lambda i: (i, C_NKV // 768)),
            pl.BlockSpec((t, LANES), lambda i: (i, 0)),
            pl.BlockSpec((t, LANES), lambda i: (i, 0)),
            full2((1, LANES)), full2((1, LANES)), full2((1, LANES)),
        ] + extra_specs,
        out_specs=[
            pl.BlockSpec((t, 1024), lambda i: (i, 0)),
            pl.BlockSpec((t, 512), lambda i: (i, 0)),
            pl.BlockSpec((t, 256), lambda i: (i, 0)),
            _layer_spec(prev, depth, li, (4, LANES, t), lambda i: (0, 0, i)),
            pl.BlockSpec((2, LANES, t), lambda i: (0, 0, i)),
            pl.BlockSpec((2, t, LANES), lambda i: (0, i, 0)),
            pl.BlockSpec((8, LANES, t), lambda i: (0, 0, i)),
        ],
        out_shape=[
            jax.ShapeDtypeStruct((m, 1024), BF16),
            jax.ShapeDtypeStruct((m, 512), F32),
            jax.ShapeDtypeStruct((m, 256), F32),
            jax.ShapeDtypeStruct((depth, 4, LANES, m), F32),
            jax.ShapeDtypeStruct((2, LANES, m), F32),
            jax.ShapeDtypeStruct((2, m, LANES), BF16),
            jax.ShapeDtypeStruct((8, LANES, m), BF16),
        ],
        input_output_aliases={} if alias is None else {7: 3},
        compiler_params=_cparams(("arbitrary",)),
        name="nsa_prep",
    )(proj, proj, cos, sin, two(qn), two(ksn), two(kwn), *extra)


def _compress_kernel(rk_ref, rv_ref, w1_ref, pe_ref, w2_ref, kn_ref, kc_ref, vc_ref):
    tr = rk_ref.shape[0]
    bd = _half_mean_mat()

    def mlp(r, j):
        ha = _dot(_bf(r + pe_ref[2 * j:2 * j + 1, :]), w1_ref[2 * j])
        hb = _dot(_bf(r + pe_ref[2 * j + 1:2 * j + 2, :]), w1_ref[2 * j + 1])
        hid = ha + pltpu.roll(hb, tr - 1, 0)
        return _dot(_bf(_silu(hid)), w2_ref[j])

    kc = _rms64(mlp(rk_ref[...], 0), kn_ref[...], bd)
    vc = mlp(rv_ref[...], 1)
    kc_ref[...] = _bf(kc)
    vct = vc.T
    vc_ref[0] = _bf(vct)
    vc_ref[1] = _bf(pltpu.roll(vct, 64, 0))


def _compress_weights(pe, w1, w2):
    w1r = w1.reshape(2, 2, 16, 64, 64)
    z = jnp.zeros_like(w1r)
    top = jnp.concatenate([w1r, z], axis=-1)
    bot = jnp.concatenate([z, w1r], axis=-1)
    w1x = jnp.stack([top, bot], axis=3)
    w1x = w1x.reshape(4, 2048, LANES).astype(BF16)
    per = pe.reshape(2, 2, 16, 1, 64)
    pex = jnp.broadcast_to(per, (2, 2, 16, 2, 64)).reshape(4, 2048)
    z2 = jnp.zeros_like(w2)
    w2x = jnp.concatenate([jnp.concatenate([w2, z2], -1), jnp.concatenate([z2, w2], -1)], axis=1).astype(BF16)
    return w1x, pex, w2x


def _compress(rk, rv, w1x, pex, w2x, kn):
    nr = rk.shape[0]
    tr = min(nr, 512)
    two = jnp.concatenate([kn, kn]).reshape(1, LANES)
    return pl.pallas_call(
        _compress_kernel,
        grid=(nr // tr,),
        in_specs=[
            pl.BlockSpec((tr, 2048), lambda i: (i, 0)),
            pl.BlockSpec((tr, 2048), lambda i: (i, 0)),
            pl.BlockSpec((4, 2048, LANES), lambda i: (0, 0, 0)),
            pl.BlockSpec((4, 2048), lambda i: (0, 0)),
            pl.BlockSpec((2, LANES, LANES), lambda i: (0, 0, 0)),
            pl.BlockSpec((1, LANES), lambda i: (0, 0)),
        ],
        out_specs=[
            pl.BlockSpec((tr, LANES), lambda i: (i, 0)),
            pl.BlockSpec((2, LANES, tr), lambda i: (0, 0, i)),
        ],
        out_shape=[
            jax.ShapeDtypeStruct((nr, LANES), BF16),
            jax.ShapeDtypeStruct((2, LANES, nr), BF16),
        ],
        compiler_params=_cparams(("arbitrary",)),
        name="nsa_compress",
    )(rk, rv, w1x, pex, w2x, two)


QB = 128
SEL_TK = 512
WIN_TK = 128


def _nsa_prompt_kernel(q_ref, kc_ref, vct_ref, kb_ref, vt_ref, ovt_ref, gate_ref, eg_ref, nz_ref, o_ref,
                       m_ref, acc_ref, st_ref, sb_ref):
    g = pl.program_id(0)
    qi = pl.program_id(1)
    t0 = qi * QB
    nc = kc_ref.shape[0]
    q = q_ref[...]
    qs = jnp.concatenate([q[:, 0:128], q[:, 256:384], q[:, 128:256], q[:, 384:512]], axis=0)
    tcol = t0 + (lax.broadcasted_iota(jnp.int32, (1, 4 * QB), 1) & (QB - 1))
    tq = t0 + lax.broadcasted_iota(jnp.int32, (1, QB), 1)

    def pv(vte, vto, pb):
        return jnp.concatenate([_dot(vte, pb[:, 0:2 * QB]), _dot(vto, pb[:, 2 * QB:4 * QB])], axis=1)

    sc = _dot_nt(kc_ref[...], qs)
    n_io = lax.broadcasted_iota(jnp.int32, (nc, 1), 0)
    mask_c = (n_io * CMP_STRIDE + (CMP_BLOCK - 1)) <= tcol
    sc = jnp.where(mask_c, sc, NEG)
    mc = jnp.max(sc, axis=0, keepdims=True)
    pc = jnp.where(mask_c, jnp.exp(sc - mc), 0.0)
    zc = jnp.sum(pc, axis=0, keepdims=True)
    pc = pc / jnp.where(zc > 0, zc, 1.0)
    o_c = pv(vct_ref[g], vct_ref[1 - g], _bf(pc))

    p4 = pc[:, 0:QB] + pc[:, QB:2 * QB] + pc[:, 2 * QB:3 * QB] + pc[:, 3 * QB:4 * QB]
    imp = _dot_tri(ovt_ref[...], p4)
    jb = lax.broadcasted_iota(jnp.int32, (LANES, QB), 0)
    cur = tq >> 6
    forced = (jb == 0) | ((jb <= cur) & (jb > cur - N_LOCAL))
    valid = jb <= cur
    score = jnp.where(forced, FORCE_SCORE, imp)
    score = jnp.where(valid, score, -FORCE_SCORE)
    st_ref[...] = score
    n_blk = ((t0 + QB - 1) >> 6) + 1

    def rank_body(i, rank):
        row = st_ref[pl.ds(i, 1), :]
        beats = (row > score) | ((row == score) & (i < jb))
        return rank + jnp.where(beats, 1.0, 0.0)

    rank = lax.fori_loop(0, n_blk, rank_body, jnp.zeros((LANES, QB), F32))
    sb_ref[...] = jnp.where((rank < SLC_TOPN) & valid, 0.0, NEG)

    def reset():
        m_ref[...] = jnp.full(m_ref.shape, NEG, F32)
        acc_ref[...] = jnp.zeros(acc_ref.shape, F32)

    def update(kidx, ve, vo, tk, tiles):
        m_old = m_ref[...]
        m_new = m_old
        ss = []
        for k0, bias, causal in tiles:
            s = _dot_nt(kb_ref[kidx, pl.ds(k0, tk), :], qs)
            if bias is not None:
                s = s + bias
            if causal:
                kpos = k0 + lax.broadcasted_iota(jnp.int32, (tk, 1), 0)
                s = jnp.where(kpos <= tcol, s, NEG)
            m_new = jnp.maximum(m_new, jnp.max(s, axis=0, keepdims=True))
            ss.append(s)
        acc = jnp.exp(m_old - m_new) * acc_ref[...]
        for (k0, _, _), s in zip(tiles, ss):
            acc = acc + pv(vt_ref[ve, :, pl.ds(k0, tk)], vt_ref[vo, :, pl.ds(k0, tk)], _bf(jnp.exp(s - m_new)))
        acc_ref[...] = acc
        m_ref[...] = m_new

    def result():
        a = acc_ref[...]
        den = pltpu.roll(a, 64, 0)
        return a / jnp.where(den > 0, den, 1.0)

    def sel_bias(k0):
        rows = sb_ref[pl.ds(pl.multiple_of(k0 // SLC_BLOCK, 8), SEL_TK // SLC_BLOCK), :]
        b = jnp.concatenate([jnp.broadcast_to(rows[r:r + 1, :], (SLC_BLOCK, QB))
                             for r in range(SEL_TK // SLC_BLOCK)], axis=0)
        return jnp.concatenate([b, b, b, b], axis=1)

    def sel_tile(k0, causal):
        return (k0, sel_bias(k0), causal)

    reset()
    vse = 2 * g
    n_full = t0 // SEL_TK

    def sel_body(kp, c):
        k0 = pl.multiple_of(kp * 2 * SEL_TK, 2 * SEL_TK)
        update(0, vse, vse + 1, SEL_TK, [sel_tile(k0, False), sel_tile(k0 + SEL_TK, False)])
        return c

    lax.fori_loop(0, n_full // 2, sel_body, 0)

    @pl.when(n_full % 2 == 1)
    def _():
        update(0, vse, vse + 1, SEL_TK, [sel_tile(pl.multiple_of((n_full - 1) * SEL_TK, SEL_TK), False)])

    update(0, vse, vse + 1, SEL_TK, [sel_tile(pl.multiple_of(n_full * SEL_TK, SEL_TK), True)])
    o_s = result()

    reset()
    vwe = 4 + 2 * g
    n_old = WINDOW // WIN_TK

    @pl.when(qi >= n_old)
    def _():
        k0 = pl.multiple_of((qi - n_old) * WIN_TK, WIN_TK)
        kpos = k0 + lax.broadcasted_iota(jnp.int32, (WIN_TK, 1), 0)
        update(1, vwe, vwe + 1, WIN_TK, [(k0, jnp.where(tcol - kpos <= WINDOW, 0.0, NEG), False)])

    def win_body(kt, c):
        update(1, vwe, vwe + 1, WIN_TK, [(pl.multiple_of(kt * WIN_TK, WIN_TK), None, False)])
        return c

    lax.fori_loop(jnp.maximum(qi - n_old + 1, 0), qi, win_body, 0)
    update(1, vwe, vwe + 1, WIN_TK, [(pl.multiple_of(qi * WIN_TK, WIN_TK), None, True)])
    o_w = result()

    gx = _dot_split(jax.nn.sigmoid(gate_ref[...]), eg_ref[0])
    lo_rows = lax.broadcasted_iota(jnp.int32, (LANES, QB), 0) < 64
    for k in range(2):
        ca = slice(k * QB, (k + 1) * QB)
        cb = slice(2 * QB + k * QB, 2 * QB + (k + 1) * QB)
        cols = slice(k * 128, (k + 1) * 128)
        tile = lambda o: jnp.where(lo_rows, o[:, ca], o[:, cb]).T
        o = (gx[:, k * 128:(k + 1) * 128] * tile(o_c)
             + gx[:, 256 + k * 128:256 + (k + 1) * 128] * tile(o_s)
             + gx[:, 512 + k * 128:512 + (k + 1) * 128] * tile(o_w))
        o_ref[:, cols] = o * _silu(nz_ref[:, cols])


def _overlap_mat(nc):
    n = np.arange(nc)[:, None]
    j = np.arange(LANES)[None, :]
    ov = ((n * CMP_STRIDE < (j + 1) * SLC_BLOCK) & (n * CMP_STRIDE + CMP_BLOCK - 1 >= j * SLC_BLOCK))
    return ov.astype(np.float32)


def _gate_expand_mat():
    eg = np.zeros((NSA_KV_HEADS, LANES, 3 * 256), np.float32)
    for g in range(NSA_KV_HEADS):
        for hh in range(4):
            for c in range(3):
                eg[g, (g * 4 + hh) * 3 + c, c * 256 + hh * 64:c * 256 + (hh + 1) * 64] = 1.0
    return jnp.asarray(eg, BF16)


def _nsa_prompt(qb, kc, vct, kb, vt, proj):
    l = qb.shape[0]
    nc = kc.shape[0]
    assert l // SLC_BLOCK <= LANES and l % SEL_TK == 0
    ovt = jnp.asarray(_overlap_mat(nc).T, BF16)
    return pl.pallas_call(
        _nsa_prompt_kernel,
        grid=(NSA_KV_HEADS, l // QB),
        in_specs=[
            pl.BlockSpec((QB, 512), lambda g, i: (i, g)),
            pl.BlockSpec((nc, LANES), lambda g, i: (0, 0)),
            pl.BlockSpec((2, LANES, nc), lambda g, i: (0, 0, 0)),
            pl.BlockSpec((2, l, LANES), lambda g, i: (0, 0, 0)),
            pl.BlockSpec((8, LANES, l), lambda g, i: (0, 0, 0)),
            pl.BlockSpec((LANES, nc), lambda g, i: (0, 0)),
            pl.BlockSpec((QB, LANES), lambda g, i: (i, C_GATE // LANES)),
            pl.BlockSpec((1, LANES, 768), lambda g, i: (g, 0, 0)),
            pl.BlockSpec((QB, 256), lambda g, i: (i, C_NZ // 256 + g)),
        ],
        out_specs=pl.BlockSpec((QB, 256), lambda g, i: (i, g)),
        out_shape=jax.ShapeDtypeStruct((l, NSA_WIDTH), F32),
        scratch_shapes=[
            pltpu.VMEM((1, 4 * QB), F32),
            pltpu.VMEM((LANES, 4 * QB), F32),
            pltpu.VMEM((LANES, QB), F32),
            pltpu.VMEM((LANES, QB), F32),
        ],
        compiler_params=_cparams(("arbitrary", "arbitrary")),
        name="nsa_prompt",
    )(qb, kc, vct, kb, vt, ovt, proj, _gate_expand_mat(), proj)


def _softmax_with_new(s, valid, s_new):
    s = jnp.where(valid, s, NEG)
    m = jnp.maximum(jnp.max(s, axis=1, keepdims=True), s_new)
    p = jnp.where(valid, jnp.exp(s - m), 0.0)
    pn = jnp.exp(s_new - m)
    z = jnp.sum(p, axis=1, keepdims=True) + pn
    return p / z, pn / z


def _nsa_decode_kernel(pt_ref, q_ref, cache_hbm, crow_ref, win_ref, wrow_ref, wcol_ref, gate_ref, nz_ref,
                       w1_ref, pe_ref, w2_ref, kn_ref, ov_ref, ek_ref, *rest, qpos, li, fill):
    o_ref, wo_ref, buf, tok_ref, sem = rest[-5:]
    wo_ref = _layer_view(wo_ref, li, fill)
    b = pl.program_id(0)
    slot = b % 2
    n_pages = buf.shape[1]
    page = buf.shape[4]
    t = n_pages * page
    nr = t // CMP_STRIDE
    wb = win_ref.shape[-1]
    n_cmp = (t + 1 - CMP_BLOCK) // CMP_STRIDE + 1

    def page_copies(row, s):
        return [pltpu.make_async_copy(cache_hbm.at[li, pt_ref[row, p]], buf.at[s, p], sem.at[s])
                for p in range(n_pages)]

    @pl.when(b == 0)
    def _():
        for c in page_copies(0, 0):
            c.start()

    @pl.when(b + 1 < pl.num_programs(0))
    def _():
        for c in page_copies(b + 1, 1 - slot):
            c.start()

    for c in page_copies(b, slot):
        c.wait()

    q8 = q_ref[0]
    q8f = q8.astype(F32)
    bd = _half_mean_mat()

    for p in range(n_pages):
        for j in range(2):
            tok_ref[j, p * page:(p + 1) * page, :] = buf[slot, p, j].T

    def mlp(j):
        acc_a = jnp.zeros((nr, LANES), F32)
        acc_b = jnp.zeros((nr, LANES), F32)
        for l2 in range(CMP_STRIDE // 2):
            r = jnp.concatenate([tok_ref[j, pl.ds(2 * l2, nr, stride=CMP_STRIDE), :],
                                 tok_ref[j, pl.ds(2 * l2 + 1, nr, stride=CMP_STRIDE), :]], axis=1)
            cols = slice(l2 * 256, (l2 + 1) * 256)
            acc_a = acc_a + _dot(_bf(r + pe_ref[2 * j:2 * j + 1, cols]), w1_ref[2 * j, cols, :])
            acc_b = acc_b + _dot(_bf(r + pe_ref[2 * j + 1:2 * j + 2, cols]), w1_ref[2 * j + 1, cols, :])
        hid = acc_a + pltpu.roll(acc_b, nr - 1, 0)
        return _dot(_bf(_silu(hid)), w2_ref[j])

    kcc = _rms64(mlp(0), kn_ref[...], bd)
    vcc = mlp(1)

    n_io = lax.broadcasted_iota(jnp.int32, (1, nr), 1)
    mask_c = ((n_io * CMP_STRIDE + (CMP_BLOCK - 1)) <= qpos) & (n_io < n_cmp)
    sc = jnp.where(mask_c, _dot_nt(q8, _bf(kcc)), NEG)
    mc = jnp.max(sc, axis=1, keepdims=True)
    pc = jnp.where(mask_c, jnp.exp(sc - mc), 0.0)
    zc = jnp.sum(pc, axis=1, keepdims=True)
    pc = pc / jnp.where(zc > 0, zc, 1.0)
    o_c = _dot(_bf(pc), _bf(vcc))

    r8 = lax.broadcasted_iota(jnp.int32, (8, 8), 0)
    c8 = lax.broadcasted_iota(jnp.int32, (8, 8), 1)
    gsum = jnp.where((r8 >> 2) == (c8 >> 2), 1.0, 0.0).astype(BF16)
    imp = _dot_split(_dot_tri(gsum, pc), ov_ref[...])
    j_io = lax.broadcasted_iota(jnp.int32, (1, LANES), 1)
    cur = qpos // SLC_BLOCK
    forced = (j_io == 0) | ((j_io <= cur) & (j_io > cur - N_LOCAL))
    valid = j_io <= cur
    score = jnp.where(forced, FORCE_SCORE, imp)
    score = jnp.where(valid, score, -FORCE_SCORE)
    rank = jnp.zeros((8, LANES), F32)
    for i in range(cur + 1):
        col = score[:, i:i + 1]
        rank = rank + jnp.where((col > score) | ((col == score) & (i < j_io)), 1.0, 0.0)
    sel = jnp.where((rank < SLC_TOPN) & valid, 1.0, 0.0)

    crow = crow_ref[0]
    ks_new = _bf(crow[:, 256:384]).astype(F32)
    vs_new = _bf(crow[:, 384:512]).astype(F32)
    sel_past = _dot(_bf(sel), ek_ref[...]) > 0.5
    s_s = jnp.concatenate([_dot(q8, _bf(buf[slot, p, 2])) for p in range(n_pages)], axis=1)
    s_new = jnp.where(sel[:, cur:cur + 1] > 0.5, jnp.sum(q8f * ks_new, axis=-1, keepdims=True), NEG)
    p_s, pn_s = _softmax_with_new(s_s, sel_past, s_new)
    p_sb = _bf(p_s)
    o_s = _bf(pn_s).astype(F32) * vs_new
    for p in range(n_pages):
        o_s = o_s + _dot_nt(p_sb[:, p * page:(p + 1) * page], _bf(buf[slot, p, 3]))

    wrow = wrow_ref[0]
    kw_new = _bf(wrow[:, 0:128]).astype(F32)
    vw_new = _bf(wrow[:, 128:256]).astype(F32)
    i_io = lax.broadcasted_iota(jnp.int32, (1, wb), 1)
    valid_w = ((wb - i_io) <= WINDOW) & ((qpos - wb + i_io) >= 0)
    s_w = _dot(q8, _bf(win_ref[0]))
    p_w, pn_w = _softmax_with_new(s_w, valid_w, jnp.sum(q8f * kw_new, axis=-1, keepdims=True))
    o_w = _dot_nt(_bf(p_w), _bf(win_ref[1])) + _bf(pn_w).astype(F32) * vw_new

    sig = jnp.broadcast_to(jax.nn.sigmoid(gate_ref[0]), (8, LANES))
    lane8 = lax.broadcasted_iota(jnp.int32, (8, LANES), 1)
    row8 = lax.broadcasted_iota(jnp.int32, (8, LANES), 0)
    gate = lambda c: jnp.sum(jnp.where(lane8 == row8 * 3 + c, sig, 0.0), axis=-1, keepdims=True)
    o8 = gate(0) * o_c + gate(1) * o_s + gate(2) * o_w

    sw = pltpu.roll(o8, 64, 1)
    lo = lax.broadcasted_iota(jnp.int32, (1, LANES), 1) < 64
    flat = jnp.concatenate([
        jnp.where(lo, o8[0:1], sw[1:2]), jnp.where(lo, o8[2:3], sw[3:4]),
        jnp.where(lo, sw[4:5], o8[5:6]), jnp.where(lo, sw[6:7], o8[7:8])], axis=1)
    o_ref[0] = flat * _silu(nz_ref[0])

    own_col = lax.broadcasted_iota(jnp.int32, (LANES, wcol_ref.shape[2]), 1) == (b % wcol_ref.shape[2])
    last = lax.broadcasted_iota(jnp.int32, (LANES, wb), 1) == wb - 1
    for j in range(2):
        col = jnp.sum(jnp.where(own_col, wcol_ref[j], 0.0), axis=1, keepdims=True)
        wo_ref[j] = jnp.where(last, col, pltpu.roll(win_ref[j], wb - 1, 1))


def _nsa_decode(pt, q8, cache_t, crow, win_t, wrow, wcol, gate, nz, w1x, pex, w2x, kn, qpos, li, prev):
    db, n_pages = pt.shape
    page = cache_t.shape[-1]
    t = n_pages * page
    wb = win_t.shape[-1]
    nr = t // CMP_STRIDE
    assert nr == LANES and page == LANES and t % SLC_BLOCK == 0 and qpos // SLC_BLOCK < LANES and db % LANES == 0
    ek = (np.arange(LANES)[:, None] == (np.arange(t)[None, :] // SLC_BLOCK))
    ek = jnp.asarray(ek.astype(np.float32), BF16)
    ov = jnp.asarray(_overlap_mat(nr), BF16)
    two = jnp.concatenate([kn, kn]).reshape(1, LANES)
    row3 = lambda w: pl.BlockSpec((1, 1, w), lambda b, pt: (b, 0, 0))
    full = lambda a: pl.BlockSpec(a.shape, lambda b, pt: (0,) * a.ndim)
    win_spec = pl.BlockSpec((None, None, 2, LANES, wb), lambda b, pt: (li, b, 0, 0, 0))
    extra, extra_specs, alias = _layer_out(prev)
    depth = win_t.shape[0]
    return pl.pallas_call(
        functools.partial(_nsa_decode_kernel, qpos=qpos, li=li, fill=depth if prev is None else 0),
        grid_spec=pltpu.PrefetchScalarGridSpec(
            num_scalar_prefetch=1,
            grid=(db,),
            in_specs=[
                pl.BlockSpec((1, 8, LANES), lambda b, pt: (b, 0, 0)),
                pl.BlockSpec(memory_space=pl.ANY),
                row3(512),
                win_spec,
                row3(256),
                pl.BlockSpec((2, LANES, LANES), lambda b, pt: (0, 0, b // LANES)),
                row3(LANES), row3(512),
                full(w1x), full(pex), full(w2x), full(two), full(ov), full(ek),
            ] + extra_specs,
            out_specs=[row3(512), _layer_spec(prev, depth, li, (None, 2, LANES, wb), lambda b, pt: (b, 0, 0, 0))],
            scratch_shapes=[
                pltpu.VMEM((2, n_pages, 4, LANES, page), F32),
                pltpu.VMEM((2, t, LANES), F32),
                pltpu.SemaphoreType.DMA((2,)),
            ],
        ),
        out_shape=[jax.ShapeDtypeStruct((db, 1, NSA_WIDTH), F32), jax.ShapeDtypeStruct(win_t.shape, F32)],
        input_output_aliases={} if alias is None else {15: 1},
        compiler_params=_cparams(("arbitrary",)),
        name="nsa_decode",
    )(pt, q8, cache_t, crow, win_t, wrow, wcol, gate, nz, w1x, pex, w2x, two, ov, ek, *extra)


def _mem_kv_kernel(mem_ref, nw_ref, wk_ref, wv_ref, kn_ref, kv_ref, kvb_ref):
    ml = mem_ref.shape[0]
    x = mem_ref[...]
    m = _bf(x * lax.rsqrt(jnp.mean(x * x, axis=-1, keepdims=True) + EPS) * nw_ref[...])
    k = _dot(m, wk_ref[...])
    v = _dot(m, wv_ref[...])
    for h in range(MEM_HEADS):
        cols = slice(h * 128, (h + 1) * 128)
        kh = k[:, cols]
        kh = kh * lax.rsqrt(jnp.mean(kh * kh, axis=-1, keepdims=True) + EPS) * kn_ref[...]
        kv_ref[pl.ds(h, ml, stride=2 * MEM_HEADS), :] = kh
        kv_ref[pl.ds(MEM_HEADS + h, ml, stride=2 * MEM_HEADS), :] = v[:, cols]
        kvb_ref[:, cols] = _bf(kh)
    kvb_ref[:, MEM_WIDTH:2 * MEM_WIDTH] = _bf(v)


def _mem_kv(mem, nw, wk, wv, kn):
    ml = mem.shape[0]
    return pl.pallas_call(
        _mem_kv_kernel,
        out_shape=[jax.ShapeDtypeStruct((ml * 2 * MEM_HEADS, MEM_HD), F32),
                   jax.ShapeDtypeStruct((ml, 2 * MEM_WIDTH), BF16)],
        compiler_params=pltpu.CompilerParams(vmem_limit_bytes=VMEM_LIMIT),
        name="mem_kv",
    )(mem, nw.reshape(1, -1), wk, wv, kn.reshape(1, -1))


TAIL_T = 256


def _out_proj(x_ref, ret_ref, ssd_ref, nsa_ref, wout_ref):
    return (x_ref[...] + _dot(_bf(ret_ref[...]), wout_ref[0:512, :])
            + _dot(_bf(ssd_ref[...]), wout_ref[512:1536, :])
            + _dot(_bf(nsa_ref[...]), wout_ref[1536:2048, :]))


def _cross_q(x1, ncw_ref, wq_ref, qn_ref):
    h = _bf(x1 * lax.rsqrt(jnp.mean(x1 * x1, axis=-1, keepdims=True) + EPS) * ncw_ref[...])
    q = _dot(h, wq_ref[...])
    out = []
    for hd in range(MEM_HEADS):
        qh = q[:, hd * 128:(hd + 1) * 128]
        out.append(_bf(qh * lax.rsqrt(jnp.mean(qh * qh, axis=-1, keepdims=True) + EPS) * qn_ref[...]))
    return out


def _tail_kernel(x_ref, ret_ref, ssd_ref, nsa_ref, wout_ref, ncw_ref, wq_ref, qn_ref, kvb_ref, wo_ref, y_ref):
    x1 = _out_proj(x_ref, ret_ref, ssd_ref, nsa_ref, wout_ref)
    outs = []
    for hd, qh in enumerate(_cross_q(x1, ncw_ref, wq_ref, qn_ref)):
        s = _dot_nt(qh, kvb_ref[:, hd * 128:(hd + 1) * 128]) * (MEM_HD ** -0.5)
        s = s - jnp.max(s, axis=-1, keepdims=True)
        p = jnp.exp(s)
        p = p / jnp.sum(p, axis=-1, keepdims=True)
        outs.append(_dot(_bf(p), kvb_ref[:, MEM_WIDTH + hd * 128:MEM_WIDTH + (hd + 1) * 128]))
    y_ref[...] = x1 + _dot(_bf(jnp.concatenate(outs, axis=1)), wo_ref[...])


def _tail(x, ret, ssd, nsa, wout, ncw, wq, qn, kvb, wo):
    m = x.shape[0]
    t = min(m, TAIL_T)
    ml = kvb.shape[0]
    full2 = lambda shape: pl.BlockSpec(shape, lambda i: (0, 0))
    return pl.pallas_call(
        _tail_kernel,
        grid=(m // t,),
        in_specs=[
            pl.BlockSpec((t, D_MODEL), lambda i: (i, 0)),
            pl.BlockSpec((t, RET_WIDTH), lambda i: (i, 0)),
            pl.BlockSpec((t, SSD_WIDTH), lambda i: (i, 0)),
            pl.BlockSpec((t, NSA_WIDTH), lambda i: (i, 0)),
            full2((D_MODEL, D_MODEL)),
            full2((1, D_MODEL)),
            full2((D_MODEL, MEM_WIDTH)),
            full2((1, MEM_HD)),
            full2((ml, 2 * MEM_WIDTH)),
            full2((MEM_WIDTH, D_MODEL)),
        ],
        out_specs=pl.BlockSpec((t, D_MODEL), lambda i: (i, 0)),
        out_shape=jax.ShapeDtypeStruct((m, D_MODEL), F32),
        compiler_params=_cparams(("arbitrary",)),
        name="layer_tail",
    )(x, ret, ssd, nsa, wout, ncw.reshape(1, -1), wq, qn.reshape(1, -1), kvb, wo)


def _dec_tail_a_kernel(x_ref, ret_ref, ssd_ref, nsa_ref, wout_ref, ncw_ref, wq_ref, qn_ref, x1_ref, q_ref):
    x1 = _out_proj(x_ref, ret_ref, ssd_ref, nsa_ref, wout_ref)
    x1_ref[...] = x1
    for hd, qh in enumerate(_cross_q(x1, ncw_ref, wq_ref, qn_ref)):
        q_ref[:, hd * 128:(hd + 1) * 128] = qh


def _dec_tail_a(x, ret, ssd, nsa, wout, ncw, wq, qn):
    m = x.shape[0]
    return pl.pallas_call(
        _dec_tail_a_kernel,
        out_shape=[jax.ShapeDtypeStruct((m, D_MODEL), F32), jax.ShapeDtypeStruct((m, MEM_WIDTH), BF16)],
        compiler_params=pltpu.CompilerParams(vmem_limit_bytes=VMEM_LIMIT),
        name="dec_tail_a",
    )(x, ret, ssd, nsa, wout, ncw.reshape(1, -1), wq, qn.reshape(1, -1))


def _dec_xattn_kernel(q_ref, mem_ref, x1_ref, wo_ref, y_ref):
    ml = mem_ref.shape[-2] // (2 * MEM_HEADS)
    lane = lax.broadcasted_iota(jnp.int32, (8, MEM_WIDTH), 1)
    row = lax.broadcasted_iota(jnp.int32, (8, MEM_WIDTH), 0)
    own = (lane >> 7) == row
    outs = []
    for b in range(DEC_BB):
        q4 = _bf(jnp.where(own, jnp.broadcast_to(q_ref[b:b + 1, :].astype(F32), (8, MEM_WIDTH)), 0.0))
        kb = jnp.concatenate([_bf(mem_ref[b, pl.ds(h, ml, stride=2 * MEM_HEADS), :]) for h in range(MEM_HEADS)], axis=1)
        vb = jnp.concatenate([_bf(mem_ref[b, pl.ds(MEM_HEADS + h, ml, stride=2 * MEM_HEADS), :])
                              for h in range(MEM_HEADS)], axis=1)
        s = _dot_nt(q4, kb) * (MEM_HD ** -0.5)
        s = s - jnp.max(s, axis=-1, keepdims=True)
        p = jnp.exp(s)
        p = p / jnp.sum(p, axis=-1, keepdims=True)
        o4 = _dot(_bf(p), vb)
        outs.append(jnp.sum(jnp.where(own, o4, 0.0), axis=0, keepdims=True))
    y_ref[...] = x1_ref[...] + _dot(_bf(jnp.concatenate(outs, axis=0)), wo_ref[...])


def _dec_xattn(q, mem, x1, wo, li):
    db = q.shape[0]
    rows = mem.shape[2]
    return pl.pallas_call(
        _dec_xattn_kernel,
        grid=(db // DEC_BB,),
        in_specs=[
            pl.BlockSpec((DEC_BB, MEM_WIDTH), lambda i: (i, 0)),
            pl.BlockSpec((None, DEC_BB, rows, MEM_HD), lambda i: (li, i, 0, 0)),
            pl.BlockSpec((DEC_BB, D_MODEL), lambda i: (i, 0)),
            pl.BlockSpec((MEM_WIDTH, D_MODEL), lambda i: (0, 0)),
        ],
        out_specs=pl.BlockSpec((DEC_BB, D_MODEL), lambda i: (i, 0)),
        out_shape=jax.ShapeDtypeStruct((db, D_MODEL), F32),
        compiler_params=_cparams(("arbitrary",)),
        name="dec_xattn",
    )(q, mem, x1, wo)


def _prep_w_tail(wt):
    k = wt.shape[1]
    nq = wt[4624:5136].reshape(NSA_HEADS, NSA_HD, k)
    z = jnp.zeros_like(nq)
    nq_pad = jnp.concatenate([
        jnp.concatenate([nq[:4], z[:4]], axis=1),
        jnp.concatenate([z[4:], nq[4:]], axis=1)], axis=0).reshape(NSA_HEADS * LANES, k)
    padr = lambda a: jnp.pad(a, ((0, LANES - a.shape[0]), (0, 0)))
    return jnp.concatenate([
        wt[5928:6440],
        nq_pad,
        wt[5136:5904],
        padr(wt[4608:4624]),
        padr(wt[5904:5928]),
    ], axis=0)


def _rope_tables(pos, head_dim, rows):
    half = head_dim // 2
    inv = jnp.exp(-math.log(ROPE_THETA) * jnp.arange(half, dtype=F32) / half)
    ang = pos.astype(F32)[:, None] * inv[None, :]
    cos = jnp.cos(ang)
    sin = jnp.sin(ang)
    reps = LANES // head_dim
    cos_t = jnp.tile(jnp.concatenate([cos, cos], axis=-1), (1, reps))
    sin_t = jnp.tile(jnp.concatenate([-sin, sin], axis=-1), (1, reps))
    if cos_t.shape[0] != rows:
        cos_t = jnp.broadcast_to(cos_t, (rows, LANES))
        sin_t = jnp.broadcast_to(sin_t, (rows, LANES))
    return cos_t, sin_t


def kernel(x_prompt, x_sample, mem_prompt, state_ret, state_ssm, state_conv, cache_nsa_kv, cache_win_kv,
           cache_mem_kv, page_table, norm_mix, w_in, ssd_conv_w, ssd_conv_b, ssd_dt_bias, ssd_a_log, ssd_d,
           ssd_norm, nsa_q_norm, nsa_kc_norm, nsa_ks_norm, nsa_kw_norm, nsa_cmp_pe, nsa_cmp_w1, nsa_cmp_w2,
           w_out, norm_cross, norm_mem, mem_wq, mem_wk, mem_wv, mem_q_norm, mem_k_norm, mem_wo):
    b, l, _ = x_prompt.shape
    assert b == 1
    db, dl, _ = x_sample.shape
    assert dl == 1 and db % DEC_BB == 0
    depth = w_in.shape[0]
    n_pages = page_table.shape[1]
    page = cache_nsa_kv.shape[2]
    past_len = n_pages * page
    wbuf = cache_win_kv.shape[2]
    ml = cache_mem_kv.shape[2]
    wp = min(WINDOW, l)
    pos_p = jnp.arange(l, dtype=jnp.int32)
    pos_s = jnp.full((1,), past_len, dtype=jnp.int32)
    cos128, sin128 = _rope_tables(pos_p, RET_DK, l)
    cos64, sin64 = _rope_tables(pos_p, NSA_HD, l)
    cos128s, sin128s = _rope_tables(pos_s, RET_DK, db)
    cos64s, sin64s = _rope_tables(pos_s, NSA_HD, db)

    xp = x_prompt[0]
    xs = x_sample[:, 0, :]
    w_in_t = jnp.swapaxes(w_in, 1, 2)
    cache_t = jnp.transpose(cache_nsa_kv, (0, 1, 3, 4, 5, 2)).reshape(depth, -1, 4, LANES, page)
    win_t = jnp.transpose(cache_win_kv, (0, 1, 3, 4, 5, 2)).reshape(depth, db, 2, LANES, wbuf)
    conv_t = jnp.transpose(state_conv, (0, 2, 1, 3))
    mem_rows = cache_mem_kv.reshape(depth, db, ml * 2 * MEM_HEADS, MEM_HD)
    untok = lambda a: jnp.moveaxis(a.reshape(a.shape[:-2] + (NSA_KV_HEADS, NSA_HD, a.shape[-1])), -1, -4)
    ret_p, ssm_p, conv_p, win_p, mem_p, conv_s = [], [], [], [], [], []
    cache_p = cache_s = ret_s_all = ssm_s_all = win_s_all = None
    for li in range(depth):
        w_tail = _prep_w_tail(w_in_t[li])
        w_out_b = w_out[li].astype(BF16)
        wq_b = mem_wq[li].astype(BF16)
        wo_b = mem_wo[li].astype(BF16)
        w1x, pex, w2x = _compress_weights(nsa_cmp_pe[li], nsa_cmp_w1[li], nsa_cmp_w2[li])
        ssd_w = (ssd_conv_w[li], ssd_conv_b[li], ssd_dt_bias[li], ssd_a_log[li], ssd_d[li], ssd_norm[li])
        nsa_n = (nsa_q_norm[li], nsa_ks_norm[li], nsa_kw_norm[li])
        proj = _inproj(xp, norm_mix[li], w_in_t, w_tail, li)
        ret_out, ret_s = _ret_prompt(proj, cos128, sin128)
        ssd_out, ssm_h, conv8 = _ssd_prompt(proj, *ssd_w)
        qb, cache, _, cache_p, win_tp, kb, vt = _nsa_prep(proj, cos64, sin64, li, depth, cache_p, *nsa_n)
        rk = cache[:, 0:128].reshape(l // CMP_STRIDE, CMP_STRIDE * LANES)
        rv = cache[:, 128:256].reshape(l // CMP_STRIDE, CMP_STRIDE * LANES)
        kc_b, vct = _compress(rk, rv, w1x, pex, w2x, nsa_kc_norm[li])
        nsa_out = _nsa_prompt(qb, kc_b, vct, kb, vt, proj)
        mkv, mkv_b = _mem_kv(mem_prompt[0], norm_mem[li], mem_wk[li].astype(BF16), mem_wv[li].astype(BF16),
                             mem_k_norm[li])
        xp = _tail(xp, ret_out, ssd_out, nsa_out, w_out_b, norm_cross[li], wq_b, mem_q_norm[li], mkv_b, wo_b)
        ret_p.append(ret_s[None])
        ssm_p.append(ssm_h[None])
        conv_p.append(conv8[None, 8 - (SSD_CONV - 1):])
        win_p.append(untok(win_tp[:, :, l - wp:])[None])
        mem_p.append(mkv.reshape(1, ml, 2, MEM_HEADS, MEM_HD))
        sproj = _inproj(xs, norm_mix[li], w_in_t, w_tail, li)
        s_ret_out, ret_s_all = _ret_decode(sproj, cos128s, sin128s, state_ret, li, ret_s_all)
        s_ssd_out, s_conv, ssm_s_all = _ssd_decode(sproj, conv_t, state_ssm, li, ssm_s_all, *ssd_w)
        s_qb, s_cache, s_wrow, cache_s, s_wcol, _, _ = _nsa_prep(sproj, cos64s, sin64s, li, depth, cache_s, *nsa_n)
        s_nsa, win_s_all = _nsa_decode(
            page_table, s_qb.reshape(db, NSA_HEADS, LANES), cache_t, s_cache[:, None, :], win_t,
            s_wrow[:, None, :], s_wcol, sproj[:, None, C_GATE:C_GATE + LANES],
            sproj[:, None, C_NZ:C_NZ + NSA_WIDTH], w1x, pex, w2x, nsa_kc_norm[li], past_len, li, win_s_all)
        x1, s_q = _dec_tail_a(xs, s_ret_out, s_ssd_out, s_nsa[:, 0, :], w_out_b, norm_cross[li], wq_b,
                              mem_q_norm[li])
        xs = _dec_xattn(s_q, mem_rows, x1, wo_b, li)
        conv_s.append(jnp.transpose(s_conv, (1, 0, 2)))
    stack = jnp.stack
    return (xp[None], xs[:, None, :], stack(ret_p), ret_s_all, stack(ssm_p), ssm_s_all, stack(conv_p), stack(conv_s),
            untok(cache_p)[:, None], untok(cache_s)[:, :, None], stack(win_p), untok(win_s_all), stack(mem_p))
```

```python
import functools
import math

import numpy as np
import jax
import jax.numpy as jnp
from jax import lax
from jax.experimental import pallas as pl
from jax.experimental.pallas import tpu as pltpu

F32 = jnp.float32
BF16 = jnp.bfloat16

D_MODEL = 2048
RET_HEADS = 4
RET_DK = 128
RET_WIDTH = 512
CHUNK = 128
SSD_WIDTH = 1024
SSD_HEADDIM = 64
SSD_HEADS = 16
SSD_GROUPS = 2
SSD_STATE = 128
SSD_CONV = 4
SSD_CONV_DIM = 1536
NSA_WIDTH = 512
NSA_HEADS = 8
NSA_HD = 64
NSA_KV_HEADS = 2
CMP_BLOCK = 32
CMP_STRIDE = 16
SLC_BLOCK = 64
SLC_TOPN = 16
N_LOCAL = 2
WINDOW = 512
FORCE_SCORE = 1.0e6
MEM_HEADS = 4
MEM_HD = 128
MEM_WIDTH = 512
ROPE_THETA = 10000.0
EPS = 1e-6
NEG = -1.0e30

C_SZ = 2048
C_XBC = 3072
C_NZ = 4608
C_NQ = 5120
C_NKV = 6144
C_SDT = 6912
C_GATE = 7040
N_PROJ = 7168

LANES = 128
VMEM_LIMIT = 56 * 1024 * 1024


def _cparams(sem):
    return pltpu.CompilerParams(dimension_semantics=sem, vmem_limit_bytes=VMEM_LIMIT)


def _bf(x):
    return x.astype(BF16)


def _dot(a, b):
    return jnp.dot(a, b, preferred_element_type=F32)


def _dot_nt(a, b):
    return lax.dot_general(a, b, (((1,), (1,)), ((), ())), preferred_element_type=F32)


def _dot_tn(a, b):
    return lax.dot_general(a, b, (((0,), (0,)), ((), ())), preferred_element_type=F32)


def _split3(a):
    hi = a.astype(BF16)
    r = a - hi.astype(F32)
    mid = r.astype(BF16)
    lo = (r - mid.astype(F32)).astype(BF16)
    return hi, mid, lo


def _split2(a):
    hi = a.astype(BF16)
    lo = (a - hi.astype(F32)).astype(BF16)
    return hi, lo


def _dot_split(a, b_bf16):
    hi, mid, lo = _split3(a)
    return _dot(hi, b_bf16) + _dot(mid, b_bf16) + _dot(lo, b_bf16)


def _dot_tri(tri_bf16, a):
    hi, mid, lo = _split3(a)
    return _dot(tri_bf16, hi) + _dot(tri_bf16, mid) + _dot(tri_bf16, lo)


def _silu(x):
    return x * jax.nn.sigmoid(x)


def _softplus(x):
    return jnp.maximum(x, 0.0) + jnp.log1p(jnp.exp(-jnp.abs(x)))


def _rope128(x, cos, sin):
    return x * cos + pltpu.roll(x, 64, 1) * sin


def _rope64(x, cos, sin):
    lane = lax.broadcasted_iota(jnp.int32, x.shape, 1)
    first = (lane & 63) < 32
    partner = jnp.where(first, pltpu.roll(x, 96, 1), pltpu.roll(x, 32, 1))
    return x * cos + partner * sin


def _half_mean_mat():
    r = lax.broadcasted_iota(jnp.int32, (LANES, LANES), 0)
    c = lax.broadcasted_iota(jnp.int32, (LANES, LANES), 1)
    return jnp.where((r >> 6) == (c >> 6), 1.0 / 64.0, 0.0).astype(BF16)


def _rms64(x, w, bd):
    ms = _dot_split(x * x, bd)
    return x * lax.rsqrt(ms + EPS) * w


def _pad_lanes(v, n=LANES):
    v = v.reshape(1, -1)
    return jnp.pad(v, ((0, 0), (0, n - v.shape[1])))


def _layer_out(prev):
    if prev is None:
        return [], [], None
    return [prev], [pl.BlockSpec(memory_space=pl.ANY)], prev


def _layer_spec(prev, depth, li, block, index):
    if prev is None:
        return pl.BlockSpec((depth,) + block, lambda *a: (0,) + index(*a))
    return pl.BlockSpec((None,) + block, lambda *a: (li,) + index(*a))


def _layer_view(ref, li, fill_depth):
    if not fill_depth:
        return ref
    for d in range(fill_depth):
        if d != li:
            ref[d] = jnp.zeros(ref.shape[1:], ref.dtype)
    return ref.at[li]


def _row_only(x, b):
    rid = lax.broadcasted_iota(jnp.int32, (x.shape[0], 1), 0)
    return jnp.where(rid == b, x, jnp.zeros_like(x))


N_MAIN = 4608
PROJ_TN = 512


def _inproj_kernel(x_ref, nw_ref, wa_ref, wb_ref, o_ref, h_ref):
    j = pl.program_id(1)

    @pl.when(j == 0)
    def _():
        nw = nw_ref[...]
        rows = min(128, x_ref.shape[0])

        def body(i, c):
            r = pl.ds(pl.multiple_of(i * rows, rows), rows)
            x = x_ref[r, :]
            ms = jnp.mean(x * x, axis=-1, keepdims=True)
            h_ref[r, :] = (x * lax.rsqrt(ms + EPS) * nw).astype(BF16)
            return c

        lax.fori_loop(0, x_ref.shape[0] // rows, body, 0)

    @pl.when(j < N_MAIN // PROJ_TN)
    def _():
        o_ref[...] = _dot_nt(h_ref[...], _bf(wa_ref[...]))

    @pl.when(j >= N_MAIN // PROJ_TN)
    def _():
        o_ref[...] = _dot_nt(h_ref[...], _bf(wb_ref[...]))


def _inproj(x, nw, w_t, w_tail, li):
    m = x.shape[0]
    tm = min(m, 1024)
    nja = N_MAIN // PROJ_TN
    return pl.pallas_call(
        _inproj_kernel,
        grid=(m // tm, N_PROJ // PROJ_TN),
        in_specs=[
            pl.BlockSpec((tm, D_MODEL), lambda i, j: (i, 0)),
            pl.BlockSpec((1, D_MODEL), lambda i, j: (0, 0)),
            pl.BlockSpec((None, PROJ_TN, D_MODEL), lambda i, j: (li, jnp.minimum(j, nja - 1), 0)),
            pl.BlockSpec((PROJ_TN, D_MODEL), lambda i, j: (jnp.maximum(j - nja, 0), 0)),
        ],
        out_specs=pl.BlockSpec((tm, PROJ_TN), lambda i, j: (i, j)),
        out_shape=jax.ShapeDtypeStruct((m, N_PROJ), F32),
        scratch_shapes=[pltpu.VMEM((tm, D_MODEL), BF16)],
        compiler_params=_cparams(("arbitrary", "arbitrary")),
        name="inproj",
    )(x, nw.reshape(1, D_MODEL), w_t, w_tail)


RET_T = 512


def _ret_prompt_kernel(q_ref, k_ref, v_ref, g_ref, cos_ref, sin_ref, dec_ref, qd_ref, kd_ref, cd_ref,
                       o_ref, so_ref, s_ref):
    i = pl.program_id(0)

    @pl.when(i == 0)
    def _():
        s_ref[...] = jnp.zeros(s_ref.shape, F32)

    for c in range(RET_T // CHUNK):
        rows = slice(c * CHUNK, (c + 1) * CHUNK)
        cos = cos_ref[rows, :]
        sin = sin_ref[rows, :]
        for h in range(RET_HEADS):
            cols = slice(h * 128, (h + 1) * 128)
            q = _rope128(q_ref[rows, cols], cos, sin)
            k = _rope128(k_ref[rows, cols], cos, sin) * (RET_DK ** -0.5)
            v = v_ref[rows, cols]
            s = s_ref[h]
            qb = _bf(q)
            vb = _bf(v)
            att = _dot_nt(qb, _bf(k)) * dec_ref[h]
            o = _dot(_bf(att), vb) + _dot(qb, _bf(s)) * qd_ref[h]
            s_ref[h] = s * cd_ref[h] + _dot_tn(_bf(k * kd_ref[h]), vb)
            r = o * lax.rsqrt(jnp.mean(o * o, axis=-1, keepdims=True) + EPS)
            o_ref[rows, cols] = r * _silu(g_ref[rows, cols])

    @pl.when(i == pl.num_programs(0) - 1)
    def _():
        so_ref[...] = s_ref[...]


def _ret_gamma():
    return 1.0 - np.exp2(-5.0 - np.arange(RET_HEADS, dtype=np.float64))


def _ret_consts():
    lg = np.log(_ret_gamma())
    idx = np.arange(CHUNK, dtype=np.float64)
    diff = idx[:, None] - idx[None, :]
    dec = np.where(diff[None] >= 0, np.exp(lg[:, None, None] * np.maximum(diff, 0.0)[None]), 0.0)
    qd = np.exp(lg[:, None] * (idx + 1.0)[None])
    kd = np.exp(lg[:, None] * (CHUNK - 1.0 - idx)[None])
    cd = np.exp(lg * CHUNK)
    bc = lambda a: np.ascontiguousarray(np.broadcast_to(a[:, :, None], (RET_HEADS, CHUNK, LANES)))
    return (jnp.asarray(dec, F32), jnp.asarray(bc(qd), F32), jnp.asarray(bc(kd), F32),
            jnp.asarray(np.broadcast_to(cd[:, None, None], (RET_HEADS, 1, LANES)).copy(), F32))


def _ret_prompt(proj, cos, sin):
    l = proj.shape[0]
    dec, qd, kd, cd = _ret_consts()
    full3 = lambda shape: pl.BlockSpec(shape, lambda i: (0, 0, 0))
    return pl.pallas_call(
        _ret_prompt_kernel,
        grid=(l // RET_T,),
        in_specs=[
            pl.BlockSpec((RET_T, 512), lambda i: (i, 0)),
            pl.BlockSpec((RET_T, 512), lambda i: (i, 1)),
            pl.BlockSpec((RET_T, 512), lambda i: (i, 2)),
            pl.BlockSpec((RET_T, 512), lambda i: (i, 3)),
            pl.BlockSpec((RET_T, LANES), lambda i: (i, 0)),
            pl.BlockSpec((RET_T, LANES), lambda i: (i, 0)),
            full3((RET_HEADS, CHUNK, CHUNK)),
            full3((RET_HEADS, CHUNK, LANES)),
            full3((RET_HEADS, CHUNK, LANES)),
            full3((RET_HEADS, 1, LANES)),
        ],
        out_specs=[
            pl.BlockSpec((RET_T, 512), lambda i: (i, 0)),
            full3((RET_HEADS, RET_DK, RET_DK)),
        ],
        out_shape=[
            jax.ShapeDtypeStruct((l, RET_WIDTH), F32),
            jax.ShapeDtypeStruct((RET_HEADS, RET_DK, RET_DK), F32),
        ],
        scratch_shapes=[pltpu.VMEM((RET_HEADS, RET_DK, RET_DK), F32)],
        compiler_params=_cparams(("arbitrary",)),
        name="ret_prompt",
    )(proj, proj, proj, proj, cos, sin, dec, qd, kd, cd)


DEC_BB = 8


def _ret_decode_kernel(q_ref, k_ref, v_ref, g_ref, cos_ref, sin_ref, gam_ref, s_ref, *rest, li, fill):
    o_ref, so_ref = rest[-2:]
    so_ref = _layer_view(so_ref, li, fill)
    cos = cos_ref[...]
    sin = sin_ref[...]
    for h in range(RET_HEADS):
        cols = slice(h * 128, (h + 1) * 128)
        qb = _bf(_rope128(q_ref[:, cols], cos, sin))
        kb = _bf(_rope128(k_ref[:, cols], cos, sin) * (RET_DK ** -0.5))
        vb = _bf(v_ref[:, cols])
        gam = gam_ref[h]
        qk = jnp.sum(qb.astype(F32) * kb.astype(F32), axis=-1, keepdims=True)
        o = _bf(qk).astype(F32) * vb.astype(F32)
        rows = []
        for b in range(DEC_BB):
            s = s_ref[b, h]
            rows.append(_dot(qb, _bf(s))[b:b + 1])
            so_ref[b, h] = s * gam + _dot_tn(_row_only(kb, b), vb)
        o = o + jnp.concatenate(rows, axis=0) * gam
        r = o * lax.rsqrt(jnp.mean(o * o, axis=-1, keepdims=True) + EPS)
        o_ref[:, cols] = r * _silu(g_ref[:, cols])


def _ret_decode(proj, cos, sin, state, li, prev):
    db = proj.shape[0]
    gam = jnp.asarray(np.broadcast_to(_ret_gamma()[:, None, None], (RET_HEADS, 1, LANES)).copy(), F32)
    extra, extra_specs, alias = _layer_out(prev)
    depth = state.shape[0]
    blk = (DEC_BB, RET_HEADS, RET_DK, RET_DK)
    st_spec = pl.BlockSpec((None,) + blk, lambda i: (li, i, 0, 0, 0))
    return pl.pallas_call(
        functools.partial(_ret_decode_kernel, li=li, fill=depth if prev is None else 0),
        grid=(db // DEC_BB,),
        in_specs=[
            pl.BlockSpec((DEC_BB, 512), lambda i: (i, 0)),
            pl.BlockSpec((DEC_BB, 512), lambda i: (i, 1)),
            pl.BlockSpec((DEC_BB, 512), lambda i: (i, 2)),
            pl.BlockSpec((DEC_BB, 512), lambda i: (i, 3)),
            pl.BlockSpec((DEC_BB, LANES), lambda i: (i, 0)),
            pl.BlockSpec((DEC_BB, LANES), lambda i: (i, 0)),
            pl.BlockSpec((RET_HEADS, 1, LANES), lambda i: (0, 0, 0)),
            st_spec,
        ] + extra_specs,
        out_specs=[pl.BlockSpec((DEC_BB, 512), lambda i: (i, 0)),
                   _layer_spec(prev, depth, li, blk, lambda i: (i, 0, 0, 0))],
        out_shape=[
            jax.ShapeDtypeStruct((db, RET_WIDTH), F32),
            jax.ShapeDtypeStruct(state.shape, F32),
        ],
        input_output_aliases={} if alias is None else {8: 1},
        compiler_params=_cparams(("arbitrary",)),
        name="ret_decode",
    )(proj, proj, proj, proj, cos, sin, gam, state, *extra)


def _head_expand_mat():
    r = lax.broadcasted_iota(jnp.int32, (LANES, SSD_WIDTH), 0)
    c = lax.broadcasted_iota(jnp.int32, (LANES, SSD_WIDTH), 1)
    return jnp.where(r == (c >> 6), 1.0, 0.0).astype(BF16)


def _ssd_prompt_kernel(z_ref, xbc_ref, dt_ref, cw_ref, cb_ref, dtb_ref, alog_ref, dexp_ref, nw_ref,
                       y_ref, ho_ref, co_ref, ext_ref, ht_ref):
    i = pl.program_id(0)

    @pl.when(i == 0)
    def _():
        ext_ref[0:8, :] = jnp.zeros((8, SSD_CONV_DIM), F32)
        ht_ref[...] = jnp.zeros(ht_ref.shape, F32)

    u = xbc_ref[...]
    ext_ref[8:8 + CHUNK, :] = u
    cw = cw_ref[...]
    conv = (cb_ref[...] + cw[3:4, :] * u + cw[2:3, :] * ext_ref[7:7 + CHUNK, :]
            + cw[1:2, :] * ext_ref[6:6 + CHUNK, :] + cw[0:1, :] * ext_ref[5:5 + CHUNK, :])
    ext_ref[0:8, :] = u[CHUNK - 8:CHUNK, :]
    xbc = _silu(conv)
    xs = xbc[:, 0:SSD_WIDTH]

    dt = _softplus(dt_ref[...] + dtb_ref[...])
    a = dt * (-jnp.exp(alog_ref[...]))
    ri = lax.broadcasted_iota(jnp.int32, (CHUNK, CHUNK), 0)
    ci = lax.broadcasted_iota(jnp.int32, (CHUNK, CHUNK), 1)
    causal = ri >= ci
    tri = jnp.where(causal, 1.0, 0.0).astype(BF16)
    cum = _dot_tri(tri, a)
    cum_t = cum.T
    dt_t = dt.T
    cum_last = cum[CHUNK - 1:CHUNK, :]
    eh = _head_expand_mat()
    ecum_x = _dot_split(jnp.exp(cum), eh)
    wgt_x = _dot_split(jnp.exp(cum_last - cum) * dt, eh)
    elast_x = _dot_split(jnp.broadcast_to(jnp.exp(cum_last), (8, LANES)), eh)[0:1, :]

    lane = lax.broadcasted_iota(jnp.int32, (CHUNK, LANES), 1)
    lo_half = lane < 64
    xw = _bf(xs * wgt_x)
    y_parts = []
    ch_parts = []
    for g in range(SSD_GROUPS):
        bg = xbc[:, SSD_WIDTH + g * 128:SSD_WIDTH + (g + 1) * 128]
        cg = xbc[:, SSD_WIDTH + 256 + g * 128:SSD_WIDTH + 256 + (g + 1) * 128]
        cgb = _bf(cg)
        cb = _dot_nt(cgb, _bf(bg))
        ht = ht_ref[g]
        ch_parts.append(_dot(cgb, _bf(ht)))
        for k in range(4):
            h0 = g * 8 + 2 * k
            xp = _bf(xs[:, h0 * 64:(h0 + 2) * 64])
            ys = []
            for hh in (h0, h0 + 1):
                seg = cum[:, hh:hh + 1] - cum_t[hh:hh + 1, :]
                lm = jnp.where(causal, jnp.exp(jnp.minimum(seg, 0.0)), 0.0)
                sc = cb * lm * dt_t[hh:hh + 1, :]
                ys.append(_dot(_bf(sc), xp))
            y_parts.append(jnp.where(lo_half, ys[0], ys[1]))
        bgt = _bf(bg.T)
        ht_ref[g] = ht * elast_x[:, g * 512:(g + 1) * 512] + _dot(bgt, xw[:, g * 512:(g + 1) * 512])
    y = jnp.concatenate(y_parts, axis=1) + jnp.concatenate(ch_parts, axis=1) * ecum_x + dexp_ref[...] * xs
    gated = y * _silu(z_ref[...])
    y_ref[...] = gated * lax.rsqrt(jnp.mean(gated * gated, axis=-1, keepdims=True) + EPS) * nw_ref[...]

    @pl.when(i == pl.num_programs(0) - 1)
    def _():
        co_ref[...] = u[CHUNK - 8:CHUNK, :]
        for g in range(SSD_GROUPS):
            htf = ht_ref[g]
            for k in range(4):
                h0 = g * 8 + 2 * k
                ho_ref[h0:h0 + 2] = htf[:, k * 128:(k + 1) * 128].T.reshape(2, SSD_HEADDIM, SSD_STATE)


def _ssd_prompt(proj, conv_w, conv_b, dt_bias, a_log, d, norm_w):
    l = proj.shape[0]
    full2 = lambda shape: pl.BlockSpec(shape, lambda i: (0, 0))
    return pl.pallas_call(
        _ssd_prompt_kernel,
        grid=(l // CHUNK,),
        in_specs=[
            pl.BlockSpec((CHUNK, SSD_WIDTH), lambda i: (i, C_SZ // SSD_WIDTH)),
            pl.BlockSpec((CHUNK, SSD_CONV_DIM), lambda i: (i, C_XBC // SSD_CONV_DIM)),
            pl.BlockSpec((CHUNK, LANES), lambda i: (i, C_SDT // LANES)),
            full2((SSD_CONV, SSD_CONV_DIM)),
            full2((1, SSD_CONV_DIM)),
            full2((1, LANES)),
            full2((1, LANES)),
            full2((1, SSD_WIDTH)),
            full2((1, SSD_WIDTH)),
        ],
        out_specs=[
            pl.BlockSpec((CHUNK, SSD_WIDTH), lambda i: (i, 0)),
            pl.BlockSpec((SSD_HEADS, SSD_HEADDIM, SSD_STATE), lambda i: (0, 0, 0)),
            full2((8, SSD_CONV_DIM)),
        ],
        out_shape=[
            jax.ShapeDtypeStruct((l, SSD_WIDTH), F32),
            jax.ShapeDtypeStruct((SSD_HEADS, SSD_HEADDIM, SSD_STATE), F32),
            jax.ShapeDtypeStruct((8, SSD_CONV_DIM), F32),
        ],
        scratch_shapes=[
            pltpu.VMEM((8 + CHUNK, SSD_CONV_DIM), F32),
            pltpu.VMEM((SSD_GROUPS, SSD_STATE, 512), F32),
        ],
        compiler_params=_cparams(("arbitrary",)),
        name="ssd_prompt",
    )(proj, proj, proj, conv_w, conv_b.reshape(1, -1), _pad_lanes(dt_bias), _pad_lanes(a_log),
      jnp.repeat(d, SSD_HEADDIM).reshape(1, -1), norm_w.reshape(1, -1))


def _ssd_decode_kernel(z_ref, xbc_ref, dt_ref, cs_ref, h_ref, cw_ref, cb_ref, dtb_ref, alog_ref, dexp_ref, nw_ref,
                       *rest, li, fill):
    y_ref, co_ref, ho_ref = rest[-3:]
    ho_ref = _layer_view(ho_ref, li, fill)
    u = xbc_ref[...]
    c0 = cs_ref[0]
    c1 = cs_ref[1]
    c2 = cs_ref[2]
    cw = cw_ref[...]
    conv = cb_ref[...] + cw[3:4, :] * u + cw[2:3, :] * c2 + cw[1:2, :] * c1 + cw[0:1, :] * c0
    co_ref[0] = c1
    co_ref[1] = c2
    co_ref[2] = u
    xbc = _silu(conv)
    xs = xbc[:, 0:SSD_WIDTH]
    dt = _softplus(dt_ref[...] + dtb_ref[...])
    ea = jnp.exp(dt * (-jnp.exp(alog_ref[...])))
    eh = _head_expand_mat()
    dt_x = _dot_split(dt, eh)
    ea_x = _dot_split(ea, eh)
    dtx = dt_x * xs
    ones = jnp.ones((DEC_BB, LANES), BF16)
    ych = [[None, None] for _ in range(DEC_BB)]
    cbs = []
    for g in range(SSD_GROUPS):
        gc = slice(g * 512, (g + 1) * 512)
        bg = xbc[:, SSD_WIDTH + g * 128:SSD_WIDTH + (g + 1) * 128]
        cg = xbc[:, SSD_WIDTH + 256 + g * 128:SSD_WIDTH + 256 + (g + 1) * 128]
        cgb = _bf(cg)
        cbs.append(jnp.sum(cgb.astype(F32) * _bf(bg).astype(F32), axis=-1, keepdims=True))
        b_hi, b_lo = _split2(bg)
        for b in range(DEC_BB):
            hs = h_ref[b, g * 8:(g + 1) * 8].reshape(512, SSD_STATE)
            ych[b][g] = _dot_nt(cgb, _bf(hs))[b:b + 1]
            e_hi, e_lo = _split2(_row_only(ea_x[:, gc], b))
            decay = _dot_tn(e_hi, ones) + _dot_tn(e_lo, ones)
            x_hi, x_lo = _split2(_row_only(dtx[:, gc], b))
            upd = _dot_tn(x_hi, b_hi) + _dot_tn(x_hi, b_lo) + _dot_tn(x_lo, b_hi)
            ho_ref[b, g * 8:(g + 1) * 8] = (hs * decay + upd).reshape(8, SSD_HEADDIM, SSD_STATE)
    ych = jnp.concatenate([jnp.concatenate(r, axis=1) for r in ych], axis=0)
    lane = lax.broadcasted_iota(jnp.int32, (DEC_BB, SSD_WIDTH), 1)
    cbx = jnp.where(lane < 512, cbs[0], cbs[1])
    y = dt_x * cbx * xs + ych * ea_x + dexp_ref[...] * xs
    gated = y * _silu(z_ref[...])
    y_ref[...] = gated * lax.rsqrt(jnp.mean(gated * gated, axis=-1, keepdims=True) + EPS) * nw_ref[...]


def _ssd_decode(proj, conv_state_t, ssm_state, li, prev, conv_w, conv_b, dt_bias, a_log, d, norm_w):
    db = proj.shape[0]
    full2 = lambda shape: pl.BlockSpec(shape, lambda i: (0, 0))
    extra, extra_specs, alias = _layer_out(prev)
    depth = ssm_state.shape[0]
    blk = (DEC_BB, SSD_HEADS, SSD_HEADDIM, SSD_STATE)
    st_spec = pl.BlockSpec((None,) + blk, lambda i: (li, i, 0, 0, 0))
    return pl.pallas_call(
        functools.partial(_ssd_decode_kernel, li=li, fill=depth if prev is None else 0),
        grid=(db // DEC_BB,),
        in_specs=[
            pl.BlockSpec((DEC_BB, SSD_WIDTH), lambda i: (i, C_SZ // SSD_WIDTH)),
            pl.BlockSpec((DEC_BB, SSD_CONV_DIM), lambda i: (i, C_XBC // SSD_CONV_DIM)),
            pl.BlockSpec((DEC_BB, LANES), lambda i: (i, C_SDT // LANES)),
            pl.BlockSpec((None, SSD_CONV - 1, DEC_BB, SSD_CONV_DIM), lambda i: (li, 0, i, 0)),
            st_spec,
            full2((SSD_CONV, SSD_CONV_DIM)),
            full2((1, SSD_CONV_DIM)),
            full2((1, LANES)),
            full2((1, LANES)),
            full2((1, SSD_WIDTH)),
            full2((1, SSD_WIDTH)),
        ] + extra_specs,
        out_specs=[
            pl.BlockSpec((DEC_BB, SSD_WIDTH), lambda i: (i, 0)),
            pl.BlockSpec((SSD_CONV - 1, DEC_BB, SSD_CONV_DIM), lambda i: (0, i, 0)),
            _layer_spec(prev, depth, li, blk, lambda i: (i, 0, 0, 0)),
        ],
        out_shape=[
            jax.ShapeDtypeStruct((db, SSD_WIDTH), F32),
            jax.ShapeDtypeStruct((SSD_CONV - 1, db, SSD_CONV_DIM), F32),
            jax.ShapeDtypeStruct(ssm_state.shape, F32),
        ],
        input_output_aliases={} if alias is None else {11: 2},
        compiler_params=_cparams(("arbitrary",)),
        name="ssd_decode",
    )(proj, proj, proj, conv_state_t, ssm_state, conv_w, conv_b.reshape(1, -1), _pad_lanes(dt_bias),
      _pad_lanes(a_log), jnp.repeat(d, SSD_HEADDIM).reshape(1, -1), norm_w.reshape(1, -1), *extra)


def _value_variants(v):
    lane = lax.broadcasted_iota(jnp.int32, v.shape, 1)
    lo = lane < 64
    sw = pltpu.roll(v, 64, 1)
    one = jnp.ones_like(v)
    return [jnp.where(lo, v, one), jnp.where(lo, one, sw), jnp.where(lo, sw, one), jnp.where(lo, one, v)]


def _nsa_prep_kernel(nq_ref, nkv_ref, cos_ref, sin_ref, qn_ref, ksn_ref, kwn_ref, *rest, li, fill):
    qb_ref, cache_ref, win_ref, cache_t_ref, win_t_ref, kb_ref, vt_ref = rest[-7:]
    cache_t_ref = _layer_view(cache_t_ref, li, fill)
    cos = cos_ref[...]
    sin = sin_ref[...]
    bd = _half_mean_mat()
    qn = qn_ref[...]
    for h in range(NSA_HEADS):
        cols = slice(h * 128, (h + 1) * 128)
        x = nq_ref[:, cols]
        ms = jnp.sum(x * x, axis=-1, keepdims=True) * (1.0 / NSA_HD)
        qh = _rope64(x * lax.rsqrt(ms + EPS) * qn, cos, sin)
        qb_ref[:, cols] = _bf(qh * (NSA_HD ** -0.5))
    kc = _rope64(nkv_ref[:, 0:128], cos, sin)
    vc = nkv_ref[:, 128:256]
    ks = _rope64(_rms64(nkv_ref[:, 256:384], ksn_ref[...], bd), cos, sin)
    vs = nkv_ref[:, 384:512]
    kw = _rope64(_rms64(nkv_ref[:, 512:640], kwn_ref[...], bd), cos, sin)
    vw = nkv_ref[:, 640:768]
    for i, v in enumerate((kc, vc, ks, vs)):
        cache_ref[:, i * 128:(i + 1) * 128] = v
        cache_t_ref[i] = v.T
    for i, v in enumerate((kw, vw)):
        win_ref[:, i * 128:(i + 1) * 128] = v
        win_t_ref[i] = v.T
    kb_ref[0] = _bf(ks)
    kb_ref[1] = _bf(kw)
    for i, v in enumerate(_value_variants(vs) + _value_variants(vw)):
        vt_ref[i] = _bf(v.T)


def _nsa_prep(proj, cos, sin, li, depth, prev, qn, ksn, kwn):
    m = proj.shape[0]
    t = min(m, 256)
    extra, extra_specs, alias = _layer_out(prev)
    two = lambda w: jnp.concatenate([w, w]).reshape(1, LANES)
    full2 = lambda shape: pl.BlockSpec(shape, lambda i: (0, 0))
    return pl.pallas_call(
        functools.partial(_nsa_prep_kernel, li=li, fill=depth if prev is None else 0),
        grid=(m // t,),
        in_specs=[
            pl.BlockSpec((t, 1024), lambda i: (i, C_NQ // 1024)),
            pl.BlockSpec((t, 768), lambda i: (i, C_NKV // 768)),
            pl.BlockSpec((t, LANES), lambda i: (i, 0)),
            pl.BlockSpec((t, LANES), lambda i: (i, 0)),
            full2((1, LANES)), full2((1, LANES)), full2((1, LANES)),
        ] + extra_specs,
        out_specs=[
            pl.BlockSpec((t, 1024), lambda i: (i, 0)),
            pl.BlockSpec((t, 512), lambda i: (i, 0)),
            pl.BlockSpec((t, 256), lambda i: (i, 0)),
            _layer_spec(prev, depth, li, (4, LANES, t), lambda i: (0, 0, i)),
            pl.BlockSpec((2, LANES, t), lambda i: (0, 0, i)),
            pl.BlockSpec((2, t, LANES), lambda i: (0, i, 0)),
            pl.BlockSpec((8, LANES, t), lambda i: (0, 0, i)),
        ],
        out_shape=[
            jax.ShapeDtypeStruct((m, 1024), BF16),
            jax.ShapeDtypeStruct((m, 512), F32),
            jax.ShapeDtypeStruct((m, 256), F32),
            jax.ShapeDtypeStruct((depth, 4, LANES, m), F32),
            jax.ShapeDtypeStruct((2, LANES, m), F32),
            jax.ShapeDtypeStruct((2, m, LANES), BF16),
            jax.ShapeDtypeStruct((8, LANES, m), BF16),
        ],
        input_output_aliases={} if alias is None else {7: 3},
        compiler_params=_cparams(("arbitrary",)),
        name="nsa_prep",
    )(proj, proj, cos, sin, two(qn), two(ksn), two(kwn), *extra)


def _compress_kernel(rk_ref, rv_ref, w1_ref, pe_ref, w2_ref, kn_ref, kc_ref, vc_ref):
    tr = rk_ref.shape[0]
    bd = _half_mean_mat()

    def mlp(r, j):
        ha = _dot(_bf(r + pe_ref[2 * j:2 * j + 1, :]), w1_ref[2 * j])
        hb = _dot(_bf(r + pe_ref[2 * j + 1:2 * j + 2, :]), w1_ref[2 * j + 1])
        hid = ha + pltpu.roll(hb, tr - 1, 0)
        return _dot(_bf(_silu(hid)), w2_ref[j])

    kc = _rms64(mlp(rk_ref[...], 0), kn_ref[...], bd)
    vc = mlp(rv_ref[...], 1)
    kc_ref[...] = _bf(kc)
    vct = vc.T
    vc_ref[0] = _bf(vct)
    vc_ref[1] = _bf(pltpu.roll(vct, 64, 0))


def _compress_weights(pe, w1, w2):
    w1r = w1.reshape(2, 2, 16, 64, 64)
    z = jnp.zeros_like(w1r)
    top = jnp.concatenate([w1r, z], axis=-1)
    bot = jnp.concatenate([z, w1r], axis=-1)
    w1x = jnp.stack([top, bot], axis=3)
    w1x = w1x.reshape(4, 2048, LANES).astype(BF16)
    per = pe.reshape(2, 2, 16, 1, 64)
    pex = jnp.broadcast_to(per, (2, 2, 16, 2, 64)).reshape(4, 2048)
    z2 = jnp.zeros_like(w2)
    w2x = jnp.concatenate([jnp.concatenate([w2, z2], -1), jnp.concatenate([z2, w2], -1)], axis=1).astype(BF16)
    return w1x, pex, w2x


def _compress(rk, rv, w1x, pex, w2x, kn):
    nr = rk.shape[0]
    tr = min(nr, 512)
    two = jnp.concatenate([kn, kn]).reshape(1, LANES)
    return pl.pallas_call(
        _compress_kernel,
        grid=(nr // tr,),
        in_specs=[
            pl.BlockSpec((tr, 2048), lambda i: (i, 0)),
            pl.BlockSpec((tr, 2048), lambda i: (i, 0)),
            pl.BlockSpec((4, 2048, LANES), lambda i: (0, 0, 0)),
            pl.BlockSpec((4, 2048), lambda i: (0, 0)),
            pl.BlockSpec((2, LANES, LANES), lambda i: (0, 0, 0)),
            pl.BlockSpec((1, LANES), lambda i: (0, 0)),
        ],
        out_specs=[
            pl.BlockSpec((tr, LANES), lambda i: (i, 0)),
            pl.BlockSpec((2, LANES, tr), lambda i: (0, 0, i)),
        ],
        out_shape=[
            jax.ShapeDtypeStruct((nr, LANES), BF16),
            jax.ShapeDtypeStruct((2, LANES, nr), BF16),
        ],
        compiler_params=_cparams(("arbitrary",)),
        name="nsa_compress",
    )(rk, rv, w1x, pex, w2x, two)


QB = 128
SEL_TK = 512
WIN_TK = 128
RANK_STEP = 32


def _nsa_prompt_kernel(q_ref, kc_ref, vct_ref, kb_ref, vt_ref, ovt_ref, gate_ref, eg_ref, nz_ref, o_ref,
                       m_ref, acc_ref, st_ref, sb_ref):
    g = pl.program_id(0)
    qi = pl.program_id(1)
    t0 = qi * QB
    nc = kc_ref.shape[0]
    q = q_ref[...]
    qs = jnp.concatenate([q[:, 0:128], q[:, 256:384], q[:, 128:256], q[:, 384:512]], axis=0)
    tcol = t0 + (lax.broadcasted_iota(jnp.int32, (1, 4 * QB), 1) & (QB - 1))
    tq = t0 + lax.broadcasted_iota(jnp.int32, (1, QB), 1)

    def pv(vte, vto, pb):
        return jnp.concatenate([_dot(vte, pb[:, 0:2 * QB]), _dot(vto, pb[:, 2 * QB:4 * QB])], axis=1)

    sc = _dot_nt(kc_ref[...], qs)
    n_io = lax.broadcasted_iota(jnp.int32, (nc, 1), 0)
    mask_c = (n_io * CMP_STRIDE + (CMP_BLOCK - 1)) <= tcol
    sc = jnp.where(mask_c, sc, NEG)
    mc = jnp.max(sc, axis=0, keepdims=True)
    pc = jnp.where(mask_c, jnp.exp(sc - mc), 0.0)
    zc = jnp.sum(pc, axis=0, keepdims=True)
    pc = pc / jnp.where(zc > 0, zc, 1.0)
    o_c = pv(vct_ref[g], vct_ref[1 - g], _bf(pc))

    p4 = pc[:, 0:QB] + pc[:, QB:2 * QB] + pc[:, 2 * QB:3 * QB] + pc[:, 3 * QB:4 * QB]
    imp = _dot_tri(ovt_ref[...], p4)
    jb = lax.broadcasted_iota(jnp.int32, (LANES, QB), 0)
    cur = tq >> 6
    forced = (jb == 0) | ((jb <= cur) & (jb > cur - N_LOCAL))
    valid = jb <= cur
    score = jnp.where(forced, FORCE_SCORE, imp)
    score = jnp.where(valid, score, -FORCE_SCORE)
    st_ref[...] = score
    n_blk = ((t0 + QB - 1) >> 6) + 1
    sb_ref[...] = jnp.full(sb_ref.shape, NEG, F32)

    def rank_rows(nrows):
        sc_n = score[0:nrows]
        jb_n = jb[0:nrows]

        def rank_body(i, rank):
            row = st_ref[pl.ds(i, 1), :]
            beats = (row > sc_n) | ((row == sc_n) & (i < jb_n))
            return rank + jnp.where(beats, 1.0, 0.0)

        rank = lax.fori_loop(0, n_blk, rank_body, jnp.zeros((nrows, QB), F32))
        sb_ref[0:nrows, :] = jnp.where((rank < SLC_TOPN) & valid[0:nrows], 0.0, NEG)

    for nrows in range(RANK_STEP, LANES + 1, RANK_STEP):
        pl.when((n_blk > nrows - RANK_STEP) & (n_blk <= nrows))(functools.partial(rank_rows, nrows))

    def reset():
        m_ref[...] = jnp.full(m_ref.shape, NEG, F32)
        acc_ref[...] = jnp.zeros(acc_ref.shape, F32)

    def update(kidx, ve, vo, tk, tiles):
        m_old = m_ref[...]
        m_new = m_old
        ss = []
        for k0, bias, causal in tiles:
            s = _dot_nt(kb_ref[kidx, pl.ds(k0, tk), :], qs)
            if bias is not None:
                s = s + bias
            if causal:
                kpos = k0 + lax.broadcasted_iota(jnp.int32, (tk, 1), 0)
                s = jnp.where(kpos <= tcol, s, NEG)
            m_new = jnp.maximum(m_new, jnp.max(s, axis=0, keepdims=True))
            ss.append(s)
        acc = jnp.exp(m_old - m_new) * acc_ref[...]
        for (k0, _, _), s in zip(tiles, ss):
            acc = acc + pv(vt_ref[ve, :, pl.ds(k0, tk)], vt_ref[vo, :, pl.ds(k0, tk)], _bf(jnp.exp(s - m_new)))
        acc_ref[...] = acc
        m_ref[...] = m_new

    def result():
        a = acc_ref[...]
        den = pltpu.roll(a, 64, 0)
        return a / jnp.where(den > 0, den, 1.0)

    def sel_bias(k0):
        rows = sb_ref[pl.ds(pl.multiple_of(k0 // SLC_BLOCK, 8), SEL_TK // SLC_BLOCK), :]
        b = jnp.concatenate([jnp.broadcast_to(rows[r:r + 1, :], (SLC_BLOCK, QB))
                             for r in range(SEL_TK // SLC_BLOCK)], axis=0)
        return jnp.concatenate([b, b, b, b], axis=1)

    def sel_tile(k0, causal):
        return (k0, sel_bias(k0), causal)

    reset()
    vse = 2 * g
    n_full = t0 // SEL_TK

    def sel_body(kp, c):
        k0 = pl.multiple_of(kp * 2 * SEL_TK, 2 * SEL_TK)
        update(0, vse, vse + 1, SEL_TK, [sel_tile(k0, False), sel_tile(k0 + SEL_TK, False)])
        return c

    lax.fori_loop(0, n_full // 2, sel_body, 0)

    @pl.when(n_full % 2 == 1)
    def _():
        update(0, vse, vse + 1, SEL_TK, [sel_tile(pl.multiple_of((n_full - 1) * SEL_TK, SEL_TK), False)])

    update(0, vse, vse + 1, SEL_TK, [sel_tile(pl.multiple_of(n_full * SEL_TK, SEL_TK), True)])
    o_s = result()

    reset()
    vwe = 4 + 2 * g
    n_old = WINDOW // WIN_TK

    @pl.when(qi >= n_old)
    def _():
        k0 = pl.multiple_of((qi - n_old) * WIN_TK, WIN_TK)
        kpos = k0 + lax.broadcasted_iota(jnp.int32, (WIN_TK, 1), 0)
        update(1, vwe, vwe + 1, WIN_TK, [(k0, jnp.where(tcol - kpos <= WINDOW, 0.0, NEG), False)])

    def win_body(kt, c):
        update(1, vwe, vwe + 1, WIN_TK, [(pl.multiple_of(kt * WIN_TK, WIN_TK), None, False)])
        return c

    lax.fori_loop(jnp.maximum(qi - n_old + 1, 0), qi, win_body, 0)
    update(1, vwe, vwe + 1, WIN_TK, [(pl.multiple_of(qi * WIN_TK, WIN_TK), None, True)])
    o_w = result()

    gx = _dot_split(jax.nn.sigmoid(gate_ref[...]), eg_ref[0])
    lo_rows = lax.broadcasted_iota(jnp.int32, (LANES, QB), 0) < 64
    for k in range(2):
        ca = slice(k * QB, (k + 1) * QB)
        cb = slice(2 * QB + k * QB, 2 * QB + (k + 1) * QB)
        cols = slice(k * 128, (k + 1) * 128)
        tile = lambda o: jnp.where(lo_rows, o[:, ca], o[:, cb]).T
        o = (gx[:, k * 128:(k + 1) * 128] * tile(o_c)
             + gx[:, 256 + k * 128:256 + (k + 1) * 128] * tile(o_s)
             + gx[:, 512 + k * 128:512 + (k + 1) * 128] * tile(o_w))
        o_ref[:, cols] = o * _silu(nz_ref[:, cols])


def _overlap_mat(nc):
    n = np.arange(nc)[:, None]
    j = np.arange(LANES)[None, :]
    ov = ((n * CMP_STRIDE < (j + 1) * SLC_BLOCK) & (n * CMP_STRIDE + CMP_BLOCK - 1 >= j * SLC_BLOCK))
    return ov.astype(np.float32)


def _gate_expand_mat():
    eg = np.zeros((NSA_KV_HEADS, LANES, 3 * 256), np.float32)
    for g in range(NSA_KV_HEADS):
        for hh in range(4):
            for c in range(3):
                eg[g, (g * 4 + hh) * 3 + c, c * 256 + hh * 64:c * 256 + (hh + 1) * 64] = 1.0
    return jnp.asarray(eg, BF16)


def _nsa_prompt(qb, kc, vct, kb, vt, proj):
    l = qb.shape[0]
    nc = kc.shape[0]
    assert l // SLC_BLOCK <= LANES and l % SEL_TK == 0
    ovt = jnp.asarray(_overlap_mat(nc).T, BF16)
    return pl.pallas_call(
        _nsa_prompt_kernel,
        grid=(NSA_KV_HEADS, l // QB),
        in_specs=[
            pl.BlockSpec((QB, 512), lambda g, i: (i, g)),
            pl.BlockSpec((nc, LANES), lambda g, i: (0, 0)),
            pl.BlockSpec((2, LANES, nc), lambda g, i: (0, 0, 0)),
            pl.BlockSpec((2, l, LANES), lambda g, i: (0, 0, 0)),
            pl.BlockSpec((8, LANES, l), lambda g, i: (0, 0, 0)),
            pl.BlockSpec((LANES, nc), lambda g, i: (0, 0)),
            pl.BlockSpec((QB, LANES), lambda g, i: (i, C_GATE // LANES)),
            pl.BlockSpec((1, LANES, 768), lambda g, i: (g, 0, 0)),
            pl.BlockSpec((QB, 256), lambda g, i: (i, C_NZ // 256 + g)),
        ],
        out_specs=pl.BlockSpec((QB, 256), lambda g, i: (i, g)),
        out_shape=jax.ShapeDtypeStruct((l, NSA_WIDTH), F32),
        scratch_shapes=[
            pltpu.VMEM((1, 4 * QB), F32),
            pltpu.VMEM((LANES, 4 * QB), F32),
            pltpu.VMEM((LANES, QB), F32),
            pltpu.VMEM((LANES, QB), F32),
        ],
        compiler_params=_cparams(("arbitrary", "arbitrary")),
        name="nsa_prompt",
    )(qb, kc, vct, kb, vt, ovt, proj, _gate_expand_mat(), proj)


def _softmax_with_new(s, valid, s_new):
    s = jnp.where(valid, s, NEG)
    m = jnp.maximum(jnp.max(s, axis=1, keepdims=True), s_new)
    p = jnp.where(valid, jnp.exp(s - m), 0.0)
    pn = jnp.exp(s_new - m)
    z = jnp.sum(p, axis=1, keepdims=True) + pn
    return p / z, pn / z


DEC_ROWS = 2


def _nsa_decode_kernel(pt_ref, q_ref, cache_hbm, crow_ref, win_ref, wrow_ref, wcol_ref, gate_ref, nz_ref,
                       w1_ref, pe_ref, w2_ref, kn_ref, ov_ref, ek_ref, *rest, qpos, li, fill):
    o_ref, wo_ref, buf, tok_ref, sem = rest[-5:]
    wo_ref = _layer_view(wo_ref, li, fill)
    step = pl.program_id(0)
    slot = step % 2
    n_pages = buf.shape[2]
    page = buf.shape[5]
    t = n_pages * page
    nr = t // CMP_STRIDE
    wb = win_ref.shape[-1]
    n_cmp = (t + 1 - CMP_BLOCK) // CMP_STRIDE + 1

    def page_copies(st, s):
        return [pltpu.make_async_copy(cache_hbm.at[li, pt_ref[st * DEC_ROWS + r, p]], buf.at[s, r, p], sem.at[s])
                for r in range(DEC_ROWS) for p in range(n_pages)]

    @pl.when(step == 0)
    def _():
        for c in page_copies(0, 0):
            c.start()

    @pl.when(step + 1 < pl.num_programs(0))
    def _():
        for c in page_copies(step + 1, 1 - slot):
            c.start()

    for c in page_copies(step, slot):
        c.wait()

    bd = _half_mean_mat()
    r8 = lax.broadcasted_iota(jnp.int32, (8, 8), 0)
    c8 = lax.broadcasted_iota(jnp.int32, (8, 8), 1)
    gsum = jnp.where((r8 >> 2) == (c8 >> 2), 1.0, 0.0).astype(BF16)
    j_io = lax.broadcasted_iota(jnp.int32, (1, LANES), 1)
    cur = qpos // SLC_BLOCK
    forced = (j_io == 0) | ((j_io <= cur) & (j_io > cur - N_LOCAL))
    valid = j_io <= cur
    n_io = lax.broadcasted_iota(jnp.int32, (1, nr), 1)
    mask_c = ((n_io * CMP_STRIDE + (CMP_BLOCK - 1)) <= qpos) & (n_io < n_cmp)
    i_io = lax.broadcasted_iota(jnp.int32, (1, wb), 1)
    valid_w = ((wb - i_io) <= WINDOW) & ((qpos - wb + i_io) >= 0)
    lane8 = lax.broadcasted_iota(jnp.int32, (8, LANES), 1)
    row8 = lax.broadcasted_iota(jnp.int32, (8, LANES), 0)
    lo = lax.broadcasted_iota(jnp.int32, (1, LANES), 1) < 64
    last = lax.broadcasted_iota(jnp.int32, (LANES, wb), 1) == wb - 1
    ncol = wcol_ref.shape[2]

    def one_row(r):
        q8 = q_ref[r]
        q8f = q8.astype(F32)

        for p in range(n_pages):
            for j in range(2):
                tok_ref[r, j, p * page:(p + 1) * page, :] = buf[slot, r, p, j].T

        def mlp(j):
            acc_a = jnp.zeros((nr, LANES), F32)
            acc_b = jnp.zeros((nr, LANES), F32)
            for l2 in range(CMP_STRIDE // 2):
                x = jnp.concatenate([tok_ref[r, j, pl.ds(2 * l2, nr, stride=CMP_STRIDE), :],
                                     tok_ref[r, j, pl.ds(2 * l2 + 1, nr, stride=CMP_STRIDE), :]], axis=1)
                cols = slice(l2 * 256, (l2 + 1) * 256)
                acc_a = acc_a + _dot(_bf(x + pe_ref[2 * j:2 * j + 1, cols]), w1_ref[2 * j, cols, :])
                acc_b = acc_b + _dot(_bf(x + pe_ref[2 * j + 1:2 * j + 2, cols]), w1_ref[2 * j + 1, cols, :])
            hid = acc_a + pltpu.roll(acc_b, nr - 1, 0)
            return _dot(_bf(_silu(hid)), w2_ref[j])

        kcc = _rms64(mlp(0), kn_ref[...], bd)
        vcc = mlp(1)

        sc = jnp.where(mask_c, _dot_nt(q8, _bf(kcc)), NEG)
        mc = jnp.max(sc, axis=1, keepdims=True)
        pc = jnp.where(mask_c, jnp.exp(sc - mc), 0.0)
        zc = jnp.sum(pc, axis=1, keepdims=True)
        pc = pc / jnp.where(zc > 0, zc, 1.0)
        o_c = _dot(_bf(pc), _bf(vcc))

        imp = _dot_split(_dot_tri(gsum, pc), ov_ref[...])
        score = jnp.where(forced, FORCE_SCORE, imp)
        score = jnp.where(valid, score, -FORCE_SCORE)
        rank = jnp.zeros((8, LANES), F32)
        for i in range(cur + 1):
            col = score[:, i:i + 1]
            rank = rank + jnp.where((col > score) | ((col == score) & (i < j_io)), 1.0, 0.0)
        sel = jnp.where((rank < SLC_TOPN) & valid, 1.0, 0.0)

        crow = crow_ref[r]
        ks_new = _bf(crow[:, 256:384]).astype(F32)
        vs_new = _bf(crow[:, 384:512]).astype(F32)
        sel_past = _dot(_bf(sel), ek_ref[...]) > 0.5
        s_s = jnp.concatenate([_dot(q8, _bf(buf[slot, r, p, 2])) for p in range(n_pages)], axis=1)
        s_new = jnp.where(sel[:, cur:cur + 1] > 0.5, jnp.sum(q8f * ks_new, axis=-1, keepdims=True), NEG)
        p_s, pn_s = _softmax_with_new(s_s, sel_past, s_new)
        p_sb = _bf(p_s)
        o_s = _bf(pn_s).astype(F32) * vs_new
        for p in range(n_pages):
            o_s = o_s + _dot_nt(p_sb[:, p * page:(p + 1) * page], _bf(buf[slot, r, p, 3]))

        wrow = wrow_ref[r]
        kw_new = _bf(wrow[:, 0:128]).astype(F32)
        vw_new = _bf(wrow[:, 128:256]).astype(F32)
        s_w = _dot(q8, _bf(win_ref[r, 0]))
        p_w, pn_w = _softmax_with_new(s_w, valid_w, jnp.sum(q8f * kw_new, axis=-1, keepdims=True))
        o_w = _dot_nt(_bf(p_w), _bf(win_ref[r, 1])) + _bf(pn_w).astype(F32) * vw_new

        sig = jnp.broadcast_to(jax.nn.sigmoid(gate_ref[r]), (8, LANES))
        gate = lambda c: jnp.sum(jnp.where(lane8 == row8 * 3 + c, sig, 0.0), axis=-1, keepdims=True)
        o8 = gate(0) * o_c + gate(1) * o_s + gate(2) * o_w

        sw = pltpu.roll(o8, 64, 1)
        flat = jnp.concatenate([
            jnp.where(lo, o8[0:1], sw[1:2]), jnp.where(lo, o8[2:3], sw[3:4]),
            jnp.where(lo, sw[4:5], o8[5:6]), jnp.where(lo, sw[6:7], o8[7:8])], axis=1)
        o_ref[r] = flat * _silu(nz_ref[r])

        own_col = lax.broadcasted_iota(jnp.int32, (LANES, ncol), 1) == ((step * DEC_ROWS + r) % ncol)
        for j in range(2):
            col = jnp.sum(jnp.where(own_col, wcol_ref[j], 0.0), axis=1, keepdims=True)
            wo_ref[r, j] = jnp.where(last, col, pltpu.roll(win_ref[r, j], wb - 1, 1))

    for r in range(DEC_ROWS):
        one_row(r)


def _nsa_decode(pt, q8, cache_t, crow, win_t, wrow, wcol, gate, nz, w1x, pex, w2x, kn, qpos, li, prev):
    db, n_pages = pt.shape
    page = cache_t.shape[-1]
    t = n_pages * page
    wb = win_t.shape[-1]
    nr = t // CMP_STRIDE
    assert nr == LANES and page == LANES and t % SLC_BLOCK == 0 and qpos // SLC_BLOCK < LANES and db % LANES == 0
    ek = (np.arange(LANES)[:, None] == (np.arange(t)[None, :] // SLC_BLOCK))
    ek = jnp.asarray(ek.astype(np.float32), BF16)
    ov = jnp.asarray(_overlap_mat(nr), BF16)
    two = jnp.concatenate([kn, kn]).reshape(1, LANES)
    rows = DEC_ROWS
    row3 = lambda w: pl.BlockSpec((rows, 1, w), lambda i, pt: (i, 0, 0))
    full = lambda a: pl.BlockSpec(a.shape, lambda i, pt: (0,) * a.ndim)
    win_blk = (rows, 2, LANES, wb)
    extra, extra_specs, alias = _layer_out(prev)
    depth = win_t.shape[0]
    return pl.pallas_call(
        functools.partial(_nsa_decode_kernel, qpos=qpos, li=li, fill=depth if prev is None else 0),
        grid_spec=pltpu.PrefetchScalarGridSpec(
            num_scalar_prefetch=1,
            grid=(db // rows,),
            in_specs=[
                pl.BlockSpec((rows, 8, LANES), lambda i, pt: (i, 0, 0)),
                pl.BlockSpec(memory_space=pl.ANY),
                row3(512),
                pl.BlockSpec((None,) + win_blk, lambda i, pt: (li, i, 0, 0, 0)),
                row3(256),
                pl.BlockSpec((2, LANES, LANES), lambda i, pt: (0, 0, (i * rows) // LANES)),
                row3(LANES), row3(512),
                full(w1x), full(pex), full(w2x), full(two), full(ov), full(ek),
            ] + extra_specs,
            out_specs=[row3(512), _layer_spec(prev, depth, li, win_blk, lambda i, pt: (i, 0, 0, 0))],
            scratch_shapes=[
                pltpu.VMEM((2, rows, n_pages, 4, LANES, page), F32),
                pltpu.VMEM((rows, 2, t, LANES), F32),
                pltpu.SemaphoreType.DMA((2,)),
            ],
        ),
        out_shape=[jax.ShapeDtypeStruct((db, 1, NSA_WIDTH), F32), jax.ShapeDtypeStruct(win_t.shape, F32)],
        input_output_aliases={} if alias is None else {15: 1},
        compiler_params=_cparams(("arbitrary",)),
        name="nsa_decode",
    )(pt, q8, cache_t, crow, win_t, wrow, wcol, gate, nz, w1x, pex, w2x, two, ov, ek, *extra)


def _mem_kv_kernel(mem_ref, nw_ref, wk_ref, wv_ref, kn_ref, kv_ref, kvb_ref):
    ml = mem_ref.shape[0]
    x = mem_ref[...]
    m = _bf(x * lax.rsqrt(jnp.mean(x * x, axis=-1, keepdims=True) + EPS) * nw_ref[...])
    k = _dot(m, wk_ref[...])
    v = _dot(m, wv_ref[...])
    for h in range(MEM_HEADS):
        cols = slice(h * 128, (h + 1) * 128)
        kh = k[:, cols]
        kh = kh * lax.rsqrt(jnp.mean(kh * kh, axis=-1, keepdims=True) + EPS) * kn_ref[...]
        kv_ref[pl.ds(h, ml, stride=2 * MEM_HEADS), :] = kh
        kv_ref[pl.ds(MEM_HEADS + h, ml, stride=2 * MEM_HEADS), :] = v[:, cols]
        kvb_ref[:, cols] = _bf(kh)
    kvb_ref[:, MEM_WIDTH:2 * MEM_WIDTH] = _bf(v)


def _mem_kv(mem, nw, wk, wv, kn):
    ml = mem.shape[0]
    return pl.pallas_call(
        _mem_kv_kernel,
        out_shape=[jax.ShapeDtypeStruct((ml * 2 * MEM_HEADS, MEM_HD), F32),
                   jax.ShapeDtypeStruct((ml, 2 * MEM_WIDTH), BF16)],
        compiler_params=pltpu.CompilerParams(vmem_limit_bytes=VMEM_LIMIT),
        name="mem_kv",
    )(mem, nw.reshape(1, -1), wk, wv, kn.reshape(1, -1))


TAIL_T = 256


def _out_proj(x_ref, ret_ref, ssd_ref, nsa_ref, wout_ref):
    return (x_ref[...] + _dot(_bf(ret_ref[...]), wout_ref[0:512, :])
            + _dot(_bf(ssd_ref[...]), wout_ref[512:1536, :])
            + _dot(_bf(nsa_ref[...]), wout_ref[1536:2048, :]))


def _cross_q(x1, ncw_ref, wq_ref, qn_ref):
    h = _bf(x1 * lax.rsqrt(jnp.mean(x1 * x1, axis=-1, keepdims=True) + EPS) * ncw_ref[...])
    q = _dot(h, wq_ref[...])
    out = []
    for hd in range(MEM_HEADS):
        qh = q[:, hd * 128:(hd + 1) * 128]
        out.append(_bf(qh * lax.rsqrt(jnp.mean(qh * qh, axis=-1, keepdims=True) + EPS) * qn_ref[...]))
    return out


def _tail_kernel(x_ref, ret_ref, ssd_ref, nsa_ref, wout_ref, ncw_ref, wq_ref, qn_ref, kvb_ref, wo_ref, y_ref):
    x1 = _out_proj(x_ref, ret_ref, ssd_ref, nsa_ref, wout_ref)
    outs = []
    for hd, qh in enumerate(_cross_q(x1, ncw_ref, wq_ref, qn_ref)):
        s = _dot_nt(qh, kvb_ref[:, hd * 128:(hd + 1) * 128]) * (MEM_HD ** -0.5)
        s = s - jnp.max(s, axis=-1, keepdims=True)
        p = jnp.exp(s)
        p = p / jnp.sum(p, axis=-1, keepdims=True)
        outs.append(_dot(_bf(p), kvb_ref[:, MEM_WIDTH + hd * 128:MEM_WIDTH + (hd + 1) * 128]))
    y_ref[...] = x1 + _dot(_bf(jnp.concatenate(outs, axis=1)), wo_ref[...])


def _tail(x, ret, ssd, nsa, wout, ncw, wq, qn, kvb, wo):
    m = x.shape[0]
    t = min(m, TAIL_T)
    ml = kvb.shape[0]
    full2 = lambda shape: pl.BlockSpec(shape, lambda i: (0, 0))
    return pl.pallas_call(
        _tail_kernel,
        grid=(m // t,),
        in_specs=[
            pl.BlockSpec((t, D_MODEL), lambda i: (i, 0)),
            pl.BlockSpec((t, RET_WIDTH), lambda i: (i, 0)),
            pl.BlockSpec((t, SSD_WIDTH), lambda i: (i, 0)),
            pl.BlockSpec((t, NSA_WIDTH), lambda i: (i, 0)),
            full2((D_MODEL, D_MODEL)),
            full2((1, D_MODEL)),
            full2((D_MODEL, MEM_WIDTH)),
            full2((1, MEM_HD)),
            full2((ml, 2 * MEM_WIDTH)),
            full2((MEM_WIDTH, D_MODEL)),
        ],
        out_specs=pl.BlockSpec((t, D_MODEL), lambda i: (i, 0)),
        out_shape=jax.ShapeDtypeStruct((m, D_MODEL), F32),
        compiler_params=_cparams(("arbitrary",)),
        name="layer_tail",
    )(x, ret, ssd, nsa, wout, ncw.reshape(1, -1), wq, qn.reshape(1, -1), kvb, wo)


def _dec_tail_a_kernel(x_ref, ret_ref, ssd_ref, nsa_ref, wout_ref, ncw_ref, wq_ref, qn_ref, x1_ref, q_ref):
    x1 = _out_proj(x_ref, ret_ref, ssd_ref, nsa_ref, wout_ref)
    x1_ref[...] = x1
    for hd, qh in enumerate(_cross_q(x1, ncw_ref, wq_ref, qn_ref)):
        q_ref[:, hd * 128:(hd + 1) * 128] = qh


def _dec_tail_a(x, ret, ssd, nsa, wout, ncw, wq, qn):
    m = x.shape[0]
    return pl.pallas_call(
        _dec_tail_a_kernel,
        out_shape=[jax.ShapeDtypeStruct((m, D_MODEL), F32), jax.ShapeDtypeStruct((m, MEM_WIDTH), BF16)],
        compiler_params=pltpu.CompilerParams(vmem_limit_bytes=VMEM_LIMIT),
        name="dec_tail_a",
    )(x, ret, ssd, nsa, wout, ncw.reshape(1, -1), wq, qn.reshape(1, -1))


def _dec_xattn_kernel(q_ref, mem_ref, x1_ref, wo_ref, y_ref):
    ml = mem_ref.shape[-2] // (2 * MEM_HEADS)
    lane = lax.broadcasted_iota(jnp.int32, (8, MEM_WIDTH), 1)
    row = lax.broadcasted_iota(jnp.int32, (8, MEM_WIDTH), 0)
    own = (lane >> 7) == row
    outs = []
    for b in range(DEC_BB):
        q4 = _bf(jnp.where(own, jnp.broadcast_to(q_ref[b:b + 1, :].astype(F32), (8, MEM_WIDTH)), 0.0))
        kb = jnp.concatenate([_bf(mem_ref[b, pl.ds(h, ml, stride=2 * MEM_HEADS), :]) for h in range(MEM_HEADS)], axis=1)
        vb = jnp.concatenate([_bf(mem_ref[b, pl.ds(MEM_HEADS + h, ml, stride=2 * MEM_HEADS), :])
                              for h in range(MEM_HEADS)], axis=1)
        s = _dot_nt(q4, kb) * (MEM_HD ** -0.5)
        s = s - jnp.max(s, axis=-1, keepdims=True)
        p = jnp.exp(s)
        p = p / jnp.sum(p, axis=-1, keepdims=True)
        o4 = _dot(_bf(p), vb)
        outs.append(jnp.sum(jnp.where(own, o4, 0.0), axis=0, keepdims=True))
    y_ref[...] = x1_ref[...] + _dot(_bf(jnp.concatenate(outs, axis=0)), wo_ref[...])


def _dec_xattn(q, mem, x1, wo, li):
    db = q.shape[0]
    rows = mem.shape[2]
    return pl.pallas_call(
        _dec_xattn_kernel,
        grid=(db // DEC_BB,),
        in_specs=[
            pl.BlockSpec((DEC_BB, MEM_WIDTH), lambda i: (i, 0)),
            pl.BlockSpec((None, DEC_BB, rows, MEM_HD), lambda i: (li, i, 0, 0)),
            pl.BlockSpec((DEC_BB, D_MODEL), lambda i: (i, 0)),
            pl.BlockSpec((MEM_WIDTH, D_MODEL), lambda i: (0, 0)),
        ],
        out_specs=pl.BlockSpec((DEC_BB, D_MODEL), lambda i: (i, 0)),
        out_shape=jax.ShapeDtypeStruct((db, D_MODEL), F32),
        compiler_params=_cparams(("arbitrary",)),
        name="dec_xattn",
    )(q, mem, x1, wo)


def _prep_w_tail(wt):
    k = wt.shape[1]
    nq = wt[4624:5136].reshape(NSA_HEADS, NSA_HD, k)
    z = jnp.zeros_like(nq)
    nq_pad = jnp.concatenate([
        jnp.concatenate([nq[:4], z[:4]], axis=1),
        jnp.concatenate([z[4:], nq[4:]], axis=1)], axis=0).reshape(NSA_HEADS * LANES, k)
    padr = lambda a: jnp.pad(a, ((0, LANES - a.shape[0]), (0, 0)))
    return jnp.concatenate([
        wt[5928:6440],
        nq_pad,
        wt[5136:5904],
        padr(wt[4608:4624]),
        padr(wt[5904:5928]),
    ], axis=0)


def _rope_tables(pos, head_dim, rows):
    half = head_dim // 2
    inv = jnp.exp(-math.log(ROPE_THETA) * jnp.arange(half, dtype=F32) / half)
    ang = pos.astype(F32)[:, None] * inv[None, :]
    cos = jnp.cos(ang)
    sin = jnp.sin(ang)
    reps = LANES // head_dim
    cos_t = jnp.tile(jnp.concatenate([cos, cos], axis=-1), (1, reps))
    sin_t = jnp.tile(jnp.concatenate([-sin, sin], axis=-1), (1, reps))
    if cos_t.shape[0] != rows:
        cos_t = jnp.broadcast_to(cos_t, (rows, LANES))
        sin_t = jnp.broadcast_to(sin_t, (rows, LANES))
    return cos_t, sin_t


def kernel(x_prompt, x_sample, mem_prompt, state_ret, state_ssm, state_conv, cache_nsa_kv, cache_win_kv,
           cache_mem_kv, page_table, norm_mix, w_in, ssd_conv_w, ssd_conv_b, ssd_dt_bias, ssd_a_log, ssd_d,
           ssd_norm, nsa_q_norm, nsa_kc_norm, nsa_ks_norm, nsa_kw_norm, nsa_cmp_pe, nsa_cmp_w1, nsa_cmp_w2,
           w_out, norm_cross, norm_mem, mem_wq, mem_wk, mem_wv, mem_q_norm, mem_k_norm, mem_wo):
    b, l, _ = x_prompt.shape
    assert b == 1
    db, dl, _ = x_sample.shape
    assert dl == 1 and db % DEC_BB == 0
    depth = w_in.shape[0]
    n_pages = page_table.shape[1]
    page = cache_nsa_kv.shape[2]
    past_len = n_pages * page
    wbuf = cache_win_kv.shape[2]
    ml = cache_mem_kv.shape[2]
    wp = min(WINDOW, l)
    pos_p = jnp.arange(l, dtype=jnp.int32)
    pos_s = jnp.full((1,), past_len, dtype=jnp.int32)
    cos128, sin128 = _rope_tables(pos_p, RET_DK, l)
    cos64, sin64 = _rope_tables(pos_p, NSA_HD, l)
    cos128s, sin128s = _rope_tables(pos_s, RET_DK, db)
    cos64s, sin64s = _rope_tables(pos_s, NSA_HD, db)

    xp = x_prompt[0]
    xs = x_sample[:, 0, :]
    w_in_t = jnp.swapaxes(w_in, 1, 2)
    cache_t = jnp.transpose(cache_nsa_kv, (0, 1, 3, 4, 5, 2)).reshape(depth, -1, 4, LANES, page)
    win_t = jnp.transpose(cache_win_kv, (0, 1, 3, 4, 5, 2)).reshape(depth, db, 2, LANES, wbuf)
    conv_t = jnp.transpose(state_conv, (0, 2, 1, 3))
    mem_rows = cache_mem_kv.reshape(depth, db, ml * 2 * MEM_HEADS, MEM_HD)
    untok = lambda a: jnp.moveaxis(a.reshape(a.shape[:-2] + (NSA_KV_HEADS, NSA_HD, a.shape[-1])), -1, -4)
    ret_p, ssm_p, conv_p, win_p, mem_p, conv_s = [], [], [], [], [], []
    cache_p = cache_s = ret_s_all = ssm_s_all = win_s_all = None
    for li in range(depth):
        w_tail = _prep_w_tail(w_in_t[li])
        w_out_b = w_out[li].astype(BF16)
        wq_b = mem_wq[li].astype(BF16)
        wo_b = mem_wo[li].astype(BF16)
        w1x, pex, w2x = _compress_weights(nsa_cmp_pe[li], nsa_cmp_w1[li], nsa_cmp_w2[li])
        ssd_w = (ssd_conv_w[li], ssd_conv_b[li], ssd_dt_bias[li], ssd_a_log[li], ssd_d[li], ssd_norm[li])
        nsa_n = (nsa_q_norm[li], nsa_ks_norm[li], nsa_kw_norm[li])
        proj = _inproj(xp, norm_mix[li], w_in_t, w_tail, li)
        ret_out, ret_s = _ret_prompt(proj, cos128, sin128)
        ssd_out, ssm_h, conv8 = _ssd_prompt(proj, *ssd_w)
        qb, cache, _, cache_p, win_tp, kb, vt = _nsa_prep(proj, cos64, sin64, li, depth, cache_p, *nsa_n)
        rk = cache[:, 0:128].reshape(l // CMP_STRIDE, CMP_STRIDE * LANES)
        rv = cache[:, 128:256].reshape(l // CMP_STRIDE, CMP_STRIDE * LANES)
        kc_b, vct = _compress(rk, rv, w1x, pex, w2x, nsa_kc_norm[li])
        nsa_out = _nsa_prompt(qb, kc_b, vct, kb, vt, proj)
        mkv, mkv_b = _mem_kv(mem_prompt[0], norm_mem[li], mem_wk[li].astype(BF16), mem_wv[li].astype(BF16),
                             mem_k_norm[li])
        xp = _tail(xp, ret_out, ssd_out, nsa_out, w_out_b, norm_cross[li], wq_b, mem_q_norm[li], mkv_b, wo_b)
        ret_p.append(ret_s[None])
        ssm_p.append(ssm_h[None])
        conv_p.append(conv8[None, 8 - (SSD_CONV - 1):])
        win_p.append(untok(win_tp[:, :, l - wp:])[None])
        mem_p.append(mkv.reshape(1, ml, 2, MEM_HEADS, MEM_HD))
        sproj = _inproj(xs, norm_mix[li], w_in_t, w_tail, li)
        s_ret_out, ret_s_all = _ret_decode(sproj, cos128s, sin128s, state_ret, li, ret_s_all)
        s_ssd_out, s_conv, ssm_s_all = _ssd_decode(sproj, conv_t, state_ssm, li, ssm_s_all, *ssd_w)
        s_qb, s_cache, s_wrow, cache_s, s_wcol, _, _ = _nsa_prep(sproj, cos64s, sin64s, li, depth, cache_s, *nsa_n)
        s_nsa, win_s_all = _nsa_decode(
            page_table, s_qb.reshape(db, NSA_HEADS, LANES), cache_t, s_cache[:, None, :], win_t,
            s_wrow[:, None, :], s_wcol, sproj[:, None, C_GATE:C_GATE + LANES],
            sproj[:, None, C_NZ:C_NZ + NSA_WIDTH], w1x, pex, w2x, nsa_kc_norm[li], past_len, li, win_s_all)
        x1, s_q = _dec_tail_a(xs, s_ret_out, s_ssd_out, s_nsa[:, 0, :], w_out_b, norm_cross[li], wq_b,
                              mem_q_norm[li])
        xs = _dec_xattn(s_q, mem_rows, x1, wo_b, li)
        conv_s.append(jnp.transpose(s_conv, (1, 0, 2)))
    stack = jnp.stack
    return (xp[None], xs[:, None, :], stack(ret_p), ret_s_all, stack(ssm_p), ssm_s_all, stack(conv_p), stack(conv_s),
            untok(cache_p)[:, None], untok(cache_s)[:, :, None], stack(win_p), untok(win_s_all), stack(mem_p))
```

```python
import functools
import math

import numpy as np
import jax
import jax.numpy as jnp
from jax import lax
from jax.experimental import pallas as pl
from jax.experimental.pallas import tpu as pltpu

F32 = jnp.float32
BF16 = jnp.bfloat16

D_MODEL = 2048
RET_HEADS = 4
RET_DK = 128
RET_WIDTH = 512
CHUNK = 128
SSD_WIDTH = 1024
SSD_HEADDIM = 64
SSD_HEADS = 16
SSD_GROUPS = 2
SSD_STATE = 128
SSD_CONV = 4
SSD_CONV_DIM = 1536
NSA_WIDTH = 512
NSA_HEADS = 8
NSA_HD = 64
NSA_KV_HEADS = 2
CMP_BLOCK = 32
CMP_STRIDE = 16
SLC_BLOCK = 64
SLC_TOPN = 16
N_LOCAL = 2
WINDOW = 512
FORCE_SCORE = 1.0e6
MEM_HEADS = 4
MEM_HD = 128
MEM_WIDTH = 512
ROPE_THETA = 10000.0
EPS = 1e-6
NEG = -1.0e30

C_SZ = 2048
C_XBC = 3072
C_NZ = 4608
C_NQ = 5120
C_NKV = 6144
C_SDT = 6912
C_GATE = 7040
N_PROJ = 7168

LANES = 128
VMEM_LIMIT = 56 * 1024 * 1024


def _cparams(sem):
    return pltpu.CompilerParams(dimension_semantics=sem, vmem_limit_bytes=VMEM_LIMIT)


def _bf(x):
    return x.astype(BF16)


def _dot(a, b):
    return jnp.dot(a, b, preferred_element_type=F32)


def _dot_nt(a, b):
    return lax.dot_general(a, b, (((1,), (1,)), ((), ())), preferred_element_type=F32)


def _dot_tn(a, b):
    return lax.dot_general(a, b, (((0,), (0,)), ((), ())), preferred_element_type=F32)


def _split3(a):
    hi = a.astype(BF16)
    r = a - hi.astype(F32)
    mid = r.astype(BF16)
    lo = (r - mid.astype(F32)).astype(BF16)
    return hi, mid, lo


def _split2(a):
    hi = a.astype(BF16)
    lo = (a - hi.astype(F32)).astype(BF16)
    return hi, lo


def _dot_split(a, b_bf16):
    hi, mid, lo = _split3(a)
    return _dot(hi, b_bf16) + _dot(mid, b_bf16) + _dot(lo, b_bf16)


def _dot_tri(tri_bf16, a):
    hi, mid, lo = _split3(a)
    return _dot(tri_bf16, hi) + _dot(tri_bf16, mid) + _dot(tri_bf16, lo)


def _silu(x):
    return x * jax.nn.sigmoid(x)


def _softplus(x):
    return jnp.maximum(x, 0.0) + jnp.log1p(jnp.exp(-jnp.abs(x)))


def _rope128(x, cos, sin):
    return x * cos + pltpu.roll(x, 64, 1) * sin


def _rope64(x, cos, sin):
    lane = lax.broadcasted_iota(jnp.int32, x.shape, 1)
    first = (lane & 63) < 32
    partner = jnp.where(first, pltpu.roll(x, 96, 1), pltpu.roll(x, 32, 1))
    return x * cos + partner * sin


def _half_mean_mat():
    r = lax.broadcasted_iota(jnp.int32, (LANES, LANES), 0)
    c = lax.broadcasted_iota(jnp.int32, (LANES, LANES), 1)
    return jnp.where((r >> 6) == (c >> 6), 1.0 / 64.0, 0.0).astype(BF16)


def _rms64(x, w, bd):
    ms = _dot_split(x * x, bd)
    return x * lax.rsqrt(ms + EPS) * w


def _pad_lanes(v, n=LANES):
    v = v.reshape(1, -1)
    return jnp.pad(v, ((0, 0), (0, n - v.shape[1])))


def _layer_out(prev):
    if prev is None:
        return [], [], None
    return [prev], [pl.BlockSpec(memory_space=pl.ANY)], prev


def _layer_spec(prev, depth, li, block, index):
    if prev is None:
        return pl.BlockSpec((depth,) + block, lambda *a: (0,) + index(*a))
    return pl.BlockSpec((None,) + block, lambda *a: (li,) + index(*a))


def _layer_view(ref, li, fill_depth):
    if not fill_depth:
        return ref
    for d in range(fill_depth):
        if d != li:
            ref[d] = jnp.zeros(ref.shape[1:], ref.dtype)
    return ref.at[li]


def _row_only(x, b):
    rid = lax.broadcasted_iota(jnp.int32, (x.shape[0], 1), 0)
    return jnp.where(rid == b, x, jnp.zeros_like(x))


N_MAIN = 4608
PROJ_TN = 512


def _inproj_kernel(x_ref, nw_ref, wa_ref, wb_ref, o_ref, h_ref):
    j = pl.program_id(1)

    @pl.when(j == 0)
    def _():
        nw = nw_ref[...]
        rows = min(128, x_ref.shape[0])

        def body(i, c):
            r = pl.ds(pl.multiple_of(i * rows, rows), rows)
            x = x_ref[r, :]
            ms = jnp.mean(x * x, axis=-1, keepdims=True)
            h_ref[r, :] = (x * lax.rsqrt(ms + EPS) * nw).astype(BF16)
            return c

        lax.fori_loop(0, x_ref.shape[0] // rows, body, 0)

    @pl.when(j < N_MAIN // PROJ_TN)
    def _():
        o_ref[...] = _dot_nt(h_ref[...], _bf(wa_ref[...]))

    @pl.when(j >= N_MAIN // PROJ_TN)
    def _():
        o_ref[...] = _dot_nt(h_ref[...], _bf(wb_ref[...]))


def _inproj(x, nw, w_t, w_tail, li):
    m = x.shape[0]
    tm = min(m, 1024)
    nja = N_MAIN // PROJ_TN
    return pl.pallas_call(
        _inproj_kernel,
        grid=(m // tm, N_PROJ // PROJ_TN),
        in_specs=[
            pl.BlockSpec((tm, D_MODEL), lambda i, j: (i, 0)),
            pl.BlockSpec((1, D_MODEL), lambda i, j: (0, 0)),
            pl.BlockSpec((None, PROJ_TN, D_MODEL), lambda i, j: (li, jnp.minimum(j, nja - 1), 0)),
            pl.BlockSpec((PROJ_TN, D_MODEL), lambda i, j: (jnp.maximum(j - nja, 0), 0)),
        ],
        out_specs=pl.BlockSpec((tm, PROJ_TN), lambda i, j: (i, j)),
        out_shape=jax.ShapeDtypeStruct((m, N_PROJ), F32),
        scratch_shapes=[pltpu.VMEM((tm, D_MODEL), BF16)],
        compiler_params=_cparams(("arbitrary", "arbitrary")),
        name="inproj",
    )(x, nw.reshape(1, D_MODEL), w_t, w_tail)


RET_T = 512


def _ret_prompt_kernel(q_ref, k_ref, v_ref, g_ref, cos_ref, sin_ref, dec_ref, qd_ref, kd_ref, cd_ref,
                       o_ref, so_ref, s_ref):
    i = pl.program_id(0)

    @pl.when(i == 0)
    def _():
        s_ref[...] = jnp.zeros(s_ref.shape, F32)

    for c in range(RET_T // CHUNK):
        rows = slice(c * CHUNK, (c + 1) * CHUNK)
        cos = cos_ref[rows, :]
        sin = sin_ref[rows, :]
        for h in range(RET_HEADS):
            cols = slice(h * 128, (h + 1) * 128)
            q = _rope128(q_ref[rows, cols], cos, sin)
            k = _rope128(k_ref[rows, cols], cos, sin) * (RET_DK ** -0.5)
            v = v_ref[rows, cols]
            s = s_ref[h]
            qb = _bf(q)
            vb = _bf(v)
            att = _dot_nt(qb, _bf(k)) * dec_ref[h]
            o = _dot(_bf(att), vb) + _dot(qb, _bf(s)) * qd_ref[h]
            s_ref[h] = s * cd_ref[h] + _dot_tn(_bf(k * kd_ref[h]), vb)
            r = o * lax.rsqrt(jnp.mean(o * o, axis=-1, keepdims=True) + EPS)
            o_ref[rows, cols] = r * _silu(g_ref[rows, cols])

    @pl.when(i == pl.num_programs(0) - 1)
    def _():
        so_ref[...] = s_ref[...]


def _ret_gamma():
    return 1.0 - np.exp2(-5.0 - np.arange(RET_HEADS, dtype=np.float64))


def _ret_consts():
    lg = np.log(_ret_gamma())
    idx = np.arange(CHUNK, dtype=np.float64)
    diff = idx[:, None] - idx[None, :]
    dec = np.where(diff[None] >= 0, np.exp(lg[:, None, None] * np.maximum(diff, 0.0)[None]), 0.0)
    qd = np.exp(lg[:, None] * (idx + 1.0)[None])
    kd = np.exp(lg[:, None] * (CHUNK - 1.0 - idx)[None])
    cd = np.exp(lg * CHUNK)
    bc = lambda a: np.ascontiguousarray(np.broadcast_to(a[:, :, None], (RET_HEADS, CHUNK, LANES)))
    return (jnp.asarray(dec, F32), jnp.asarray(bc(qd), F32), jnp.asarray(bc(kd), F32),
            jnp.asarray(np.broadcast_to(cd[:, None, None], (RET_HEADS, 1, LANES)).copy(), F32))


def _ret_prompt(proj, cos, sin):
    l = proj.shape[0]
    dec, qd, kd, cd = _ret_consts()
    full3 = lambda shape: pl.BlockSpec(shape, lambda i: (0, 0, 0))
    return pl.pallas_call(
        _ret_prompt_kernel,
        grid=(l // RET_T,),
        in_specs=[
            pl.BlockSpec((RET_T, 512), lambda i: (i, 0)),
            pl.BlockSpec((RET_T, 512), lambda i: (i, 1)),
            pl.BlockSpec((RET_T, 512), lambda i: (i, 2)),
            pl.BlockSpec((RET_T, 512), lambda i: (i, 3)),
            pl.BlockSpec((RET_T, LANES), lambda i: (i, 0)),
            pl.BlockSpec((RET_T, LANES), lambda i: (i, 0)),
            full3((RET_HEADS, CHUNK, CHUNK)),
            full3((RET_HEADS, CHUNK, LANES)),
            full3((RET_HEADS, CHUNK, LANES)),
            full3((RET_HEADS, 1, LANES)),
        ],
        out_specs=[
            pl.BlockSpec((RET_T, 512), lambda i: (i, 0)),
            full3((RET_HEADS, RET_DK, RET_DK)),
        ],
        out_shape=[
            jax.ShapeDtypeStruct((l, RET_WIDTH), F32),
            jax.ShapeDtypeStruct((RET_HEADS, RET_DK, RET_DK), F32),
        ],
        scratch_shapes=[pltpu.VMEM((RET_HEADS, RET_DK, RET_DK), F32)],
        compiler_params=_cparams(("arbitrary",)),
        name="ret_prompt",
    )(proj, proj, proj, proj, cos, sin, dec, qd, kd, cd)


DEC_BB = 8


def _ret_decode_kernel(q_ref, k_ref, v_ref, g_ref, cos_ref, sin_ref, gam_ref, s_ref, *rest, li, fill):
    o_ref, so_ref = rest[-2:]
    so_ref = _layer_view(so_ref, li, fill)
    cos = cos_ref[...]
    sin = sin_ref[...]
    for h in range(RET_HEADS):
        cols = slice(h * 128, (h + 1) * 128)
        qb = _bf(_rope128(q_ref[:, cols], cos, sin))
        kb = _bf(_rope128(k_ref[:, cols], cos, sin) * (RET_DK ** -0.5))
        vb = _bf(v_ref[:, cols])
        gam = gam_ref[h]
        qk = jnp.sum(qb.astype(F32) * kb.astype(F32), axis=-1, keepdims=True)
        o = _bf(qk).astype(F32) * vb.astype(F32)
        rows = []
        for b in range(DEC_BB):
            s = s_ref[b, h]
            rows.append(_dot(qb, _bf(s))[b:b + 1])
            so_ref[b, h] = s * gam + _dot_tn(_row_only(kb, b), vb)
        o = o + jnp.concatenate(rows, axis=0) * gam
        r = o * lax.rsqrt(jnp.mean(o * o, axis=-1, keepdims=True) + EPS)
        o_ref[:, cols] = r * _silu(g_ref[:, cols])


def _ret_decode(proj, cos, sin, state, li, prev):
    db = proj.shape[0]
    gam = jnp.asarray(np.broadcast_to(_ret_gamma()[:, None, None], (RET_HEADS, 1, LANES)).copy(), F32)
    extra, extra_specs, alias = _layer_out(prev)
    depth = state.shape[0]
    blk = (DEC_BB, RET_HEADS, RET_DK, RET_DK)
    st_spec = pl.BlockSpec((None,) + blk, lambda i: (li, i, 0, 0, 0))
    return pl.pallas_call(
        functools.partial(_ret_decode_kernel, li=li, fill=depth if prev is None else 0),
        grid=(db // DEC_BB,),
        in_specs=[
            pl.BlockSpec((DEC_BB, 512), lambda i: (i, 0)),
            pl.BlockSpec((DEC_BB, 512), lambda i: (i, 1)),
            pl.BlockSpec((DEC_BB, 512), lambda i: (i, 2)),
            pl.BlockSpec((DEC_BB, 512), lambda i: (i, 3)),
            pl.BlockSpec((DEC_BB, LANES), lambda i: (i, 0)),
            pl.BlockSpec((DEC_BB, LANES), lambda i: (i, 0)),
            pl.BlockSpec((RET_HEADS, 1, LANES), lambda i: (0, 0, 0)),
            st_spec,
        ] + extra_specs,
        out_specs=[pl.BlockSpec((DEC_BB, 512), lambda i: (i, 0)),
                   _layer_spec(prev, depth, li, blk, lambda i: (i, 0, 0, 0))],
        out_shape=[
            jax.ShapeDtypeStruct((db, RET_WIDTH), F32),
            jax.ShapeDtypeStruct(state.shape, F32),
        ],
        input_output_aliases={} if alias is None else {8: 1},
        compiler_params=_cparams(("arbitrary",)),
        name="ret_decode",
    )(proj, proj, proj, proj, cos, sin, gam, state, *extra)


def _head_expand_mat():
    r = lax.broadcasted_iota(jnp.int32, (LANES, SSD_WIDTH), 0)
    c = lax.broadcasted_iota(jnp.int32, (LANES, SSD_WIDTH), 1)
    return jnp.where(r == (c >> 6), 1.0, 0.0).astype(BF16)


def _ssd_prompt_kernel(z_ref, xbc_ref, dt_ref, cw_ref, cb_ref, dtb_ref, alog_ref, dexp_ref, nw_ref,
                       y_ref, ho_ref, co_ref, ext_ref, ht_ref):
    i = pl.program_id(0)

    @pl.when(i == 0)
    def _():
        ext_ref[0:8, :] = jnp.zeros((8, SSD_CONV_DIM), F32)
        ht_ref[...] = jnp.zeros(ht_ref.shape, F32)

    u = xbc_ref[...]
    ext_ref[8:8 + CHUNK, :] = u
    cw = cw_ref[...]
    conv = (cb_ref[...] + cw[3:4, :] * u + cw[2:3, :] * ext_ref[7:7 + CHUNK, :]
            + cw[1:2, :] * ext_ref[6:6 + CHUNK, :] + cw[0:1, :] * ext_ref[5:5 + CHUNK, :])
    ext_ref[0:8, :] = u[CHUNK - 8:CHUNK, :]
    xbc = _silu(conv)
    xs = xbc[:, 0:SSD_WIDTH]

    dt = _softplus(dt_ref[...] + dtb_ref[...])
    a = dt * (-jnp.exp(alog_ref[...]))
    ri = lax.broadcasted_iota(jnp.int32, (CHUNK, CHUNK), 0)
    ci = lax.broadcasted_iota(jnp.int32, (CHUNK, CHUNK), 1)
    causal = ri >= ci
    tri = jnp.where(causal, 1.0, 0.0).astype(BF16)
    cum = _dot_tri(tri, a)
    cum_t = cum.T
    dt_t = dt.T
    cum_last = cum[CHUNK - 1:CHUNK, :]
    eh = _head_expand_mat()
    ecum_x = _dot_split(jnp.exp(cum), eh)
    wgt_x = _dot_split(jnp.exp(cum_last - cum) * dt, eh)
    elast_x = _dot_split(jnp.broadcast_to(jnp.exp(cum_last), (8, LANES)), eh)[0:1, :]

    lane = lax.broadcasted_iota(jnp.int32, (CHUNK, LANES), 1)
    lo_half = lane < 64
    xw = _bf(xs * wgt_x)
    y_parts = []
    ch_parts = []
    for g in range(SSD_GROUPS):
        bg = xbc[:, SSD_WIDTH + g * 128:SSD_WIDTH + (g + 1) * 128]
        cg = xbc[:, SSD_WIDTH + 256 + g * 128:SSD_WIDTH + 256 + (g + 1) * 128]
        cgb = _bf(cg)
        cb = _dot_nt(cgb, _bf(bg))
        ht = ht_ref[g]
        ch_parts.append(_dot(cgb, _bf(ht)))
        for k in range(4):
            h0 = g * 8 + 2 * k
            xp = _bf(xs[:, h0 * 64:(h0 + 2) * 64])
            ys = []
            for hh in (h0, h0 + 1):
                seg = cum[:, hh:hh + 1] - cum_t[hh:hh + 1, :]
                lm = jnp.where(causal, jnp.exp(jnp.minimum(seg, 0.0)), 0.0)
                sc = cb * lm * dt_t[hh:hh + 1, :]
                ys.append(_dot(_bf(sc), xp))
            y_parts.append(jnp.where(lo_half, ys[0], ys[1]))
        bgt = _bf(bg.T)
        ht_ref[g] = ht * elast_x[:, g * 512:(g + 1) * 512] + _dot(bgt, xw[:, g * 512:(g + 1) * 512])
    y = jnp.concatenate(y_parts, axis=1) + jnp.concatenate(ch_parts, axis=1) * ecum_x + dexp_ref[...] * xs
    gated = y * _silu(z_ref[...])
    y_ref[...] = gated * lax.rsqrt(jnp.mean(gated * gated, axis=-1, keepdims=True) + EPS) * nw_ref[...]

    @pl.when(i == pl.num_programs(0) - 1)
    def _():
        co_ref[...] = u[CHUNK - 8:CHUNK, :]
        for g in range(SSD_GROUPS):
            htf = ht_ref[g]
            for k in range(4):
                h0 = g * 8 + 2 * k
                ho_ref[h0:h0 + 2] = htf[:, k * 128:(k + 1) * 128].T.reshape(2, SSD_HEADDIM, SSD_STATE)


def _ssd_prompt(proj, conv_w, conv_b, dt_bias, a_log, d, norm_w):
    l = proj.shape[0]
    full2 = lambda shape: pl.BlockSpec(shape, lambda i: (0, 0))
    return pl.pallas_call(
        _ssd_prompt_kernel,
        grid=(l // CHUNK,),
        in_specs=[
            pl.BlockSpec((CHUNK, SSD_WIDTH), lambda i: (i, C_SZ // SSD_WIDTH)),
            pl.BlockSpec((CHUNK, SSD_CONV_DIM), lambda i: (i, C_XBC // SSD_CONV_DIM)),
            pl.BlockSpec((CHUNK, LANES), lambda i: (i, C_SDT // LANES)),
            full2((SSD_CONV, SSD_CONV_DIM)),
            full2((1, SSD_CONV_DIM)),
            full2((1, LANES)),
            full2((1, LANES)),
            full2((1, SSD_WIDTH)),
            full2((1, SSD_WIDTH)),
        ],
        out_specs=[
            pl.BlockSpec((CHUNK, SSD_WIDTH), lambda i: (i, 0)),
            pl.BlockSpec((SSD_HEADS, SSD_HEADDIM, SSD_STATE), lambda i: (0, 0, 0)),
            full2((8, SSD_CONV_DIM)),
        ],
        out_shape=[
            jax.ShapeDtypeStruct((l, SSD_WIDTH), F32),
            jax.ShapeDtypeStruct((SSD_HEADS, SSD_HEADDIM, SSD_STATE), F32),
            jax.ShapeDtypeStruct((8, SSD_CONV_DIM), F32),
        ],
        scratch_shapes=[
            pltpu.VMEM((8 + CHUNK, SSD_CONV_DIM), F32),
            pltpu.VMEM((SSD_GROUPS, SSD_STATE, 512), F32),
        ],
        compiler_params=_cparams(("arbitrary",)),
        name="ssd_prompt",
    )(proj, proj, proj, conv_w, conv_b.reshape(1, -1), _pad_lanes(dt_bias), _pad_lanes(a_log),
      jnp.repeat(d, SSD_HEADDIM).reshape(1, -1), norm_w.reshape(1, -1))


def _ssd_decode_kernel(z_ref, xbc_ref, dt_ref, cs_ref, h_ref, cw_ref, cb_ref, dtb_ref, alog_ref, dexp_ref, nw_ref,
                       *rest, li, fill):
    y_ref, co_ref, ho_ref = rest[-3:]
    ho_ref = _layer_view(ho_ref, li, fill)
    u = xbc_ref[...]
    c0 = cs_ref[0]
    c1 = cs_ref[1]
    c2 = cs_ref[2]
    cw = cw_ref[...]
    conv = cb_ref[...] + cw[3:4, :] * u + cw[2:3, :] * c2 + cw[1:2, :] * c1 + cw[0:1, :] * c0
    co_ref[0] = c1
    co_ref[1] = c2
    co_ref[2] = u
    xbc = _silu(conv)
    xs = xbc[:, 0:SSD_WIDTH]
    dt = _softplus(dt_ref[...] + dtb_ref[...])
    ea = jnp.exp(dt * (-jnp.exp(alog_ref[...])))
    eh = _head_expand_mat()
    dt_x = _dot_split(dt, eh)
    ea_x = _dot_split(ea, eh)
    dtx = dt_x * xs
    ones = jnp.ones((DEC_BB, LANES), BF16)
    ych = [[None, None] for _ in range(DEC_BB)]
    cbs = []
    for g in range(SSD_GROUPS):
        gc = slice(g * 512, (g + 1) * 512)
        bg = xbc[:, SSD_WIDTH + g * 128:SSD_WIDTH + (g + 1) * 128]
        cg = xbc[:, SSD_WIDTH + 256 + g * 128:SSD_WIDTH + 256 + (g + 1) * 128]
        cgb = _bf(cg)
        cbs.append(jnp.sum(cgb.astype(F32) * _bf(bg).astype(F32), axis=-1, keepdims=True))
        b_hi, b_lo = _split2(bg)
        for b in range(DEC_BB):
            hs = h_ref[b, g * 8:(g + 1) * 8].reshape(512, SSD_STATE)
            ych[b][g] = _dot_nt(cgb, _bf(hs))[b:b + 1]
            e_hi, e_lo = _split2(_row_only(ea_x[:, gc], b))
            decay = _dot_tn(e_hi, ones) + _dot_tn(e_lo, ones)
            x_hi, x_lo = _split2(_row_only(dtx[:, gc], b))
            upd = _dot_tn(x_hi, b_hi) + _dot_tn(x_hi, b_lo) + _dot_tn(x_lo, b_hi)
            ho_ref[b, g * 8:(g + 1) * 8] = (hs * decay + upd).reshape(8, SSD_HEADDIM, SSD_STATE)
    ych = jnp.concatenate([jnp.concatenate(r, axis=1) for r in ych], axis=0)
    lane = lax.broadcasted_iota(jnp.int32, (DEC_BB, SSD_WIDTH), 1)
    cbx = jnp.where(lane < 512, cbs[0], cbs[1])
    y = dt_x * cbx * xs + ych * ea_x + dexp_ref[...] * xs
    gated = y * _silu(z_ref[...])
    y_ref[...] = gated * lax.rsqrt(jnp.mean(gated * gated, axis=-1, keepdims=True) + EPS) * nw_ref[...]


def _ssd_decode(proj, conv_state_t, ssm_state, li, prev, conv_w, conv_b, dt_bias, a_log, d, norm_w):
    db = proj.shape[0]
    full2 = lambda shape: pl.BlockSpec(shape, lambda i: (0, 0))
    extra, extra_specs, alias = _layer_out(prev)
    depth = ssm_state.shape[0]
    blk = (DEC_BB, SSD_HEADS, SSD_HEADDIM, SSD_STATE)
    st_spec = pl.BlockSpec((None,) + blk, lambda i: (li, i, 0, 0, 0))
    return pl.pallas_call(
        functools.partial(_ssd_decode_kernel, li=li, fill=depth if prev is None else 0),
        grid=(db // DEC_BB,),
        in_specs=[
            pl.BlockSpec((DEC_BB, SSD_WIDTH), lambda i: (i, C_SZ // SSD_WIDTH)),
            pl.BlockSpec((DEC_BB, SSD_CONV_DIM), lambda i: (i, C_XBC // SSD_CONV_DIM)),
            pl.BlockSpec((DEC_BB, LANES), lambda i: (i, C_SDT // LANES)),
            pl.BlockSpec((None, SSD_CONV - 1, DEC_BB, SSD_CONV_DIM), lambda i: (li, 0, i, 0)),
            st_spec,
            full2((SSD_CONV, SSD_CONV_DIM)),
            full2((1, SSD_CONV_DIM)),
            full2((1, LANES)),
            full2((1, LANES)),
            full2((1, SSD_WIDTH)),
            full2((1, SSD_WIDTH)),
        ] + extra_specs,
        out_specs=[
            pl.BlockSpec((DEC_BB, SSD_WIDTH), lambda i: (i, 0)),
            pl.BlockSpec((SSD_CONV - 1, DEC_BB, SSD_CONV_DIM), lambda i: (0, i, 0)),
            _layer_spec(prev, depth, li, blk, lambda i: (i, 0, 0, 0)),
        ],
        out_shape=[
            jax.ShapeDtypeStruct((db, SSD_WIDTH), F32),
            jax.ShapeDtypeStruct((SSD_CONV - 1, db, SSD_CONV_DIM), F32),
            jax.ShapeDtypeStruct(ssm_state.shape, F32),
        ],
        input_output_aliases={} if alias is None else {11: 2},
        compiler_params=_cparams(("arbitrary",)),
        name="ssd_decode",
    )(proj, proj, proj, conv_state_t, ssm_state, conv_w, conv_b.reshape(1, -1), _pad_lanes(dt_bias),
      _pad_lanes(a_log), jnp.repeat(d, SSD_HEADDIM).reshape(1, -1), norm_w.reshape(1, -1), *extra)


def _value_variants(v):
    lane = lax.broadcasted_iota(jnp.int32, v.shape, 1)
    lo = lane < 64
    sw = pltpu.roll(v, 64, 1)
    one = jnp.ones_like(v)
    return [jnp.where(lo, v, one), jnp.where(lo, one, sw), jnp.where(lo, sw, one), jnp.where(lo, one, v)]


def _nsa_prep_kernel(nq_ref, nkv_ref, cos_ref, sin_ref, qn_ref, ksn_ref, kwn_ref, *rest, li, fill):
    qb_ref, cache_ref, win_ref, cache_t_ref, win_t_ref, kb_ref, vt_ref = rest[-7:]
    cache_t_ref = _layer_view(cache_t_ref, li, fill)
    cos = cos_ref[...]
    sin = sin_ref[...]
    bd = _half_mean_mat()
    qn = qn_ref[...]
    for h in range(NSA_HEADS):
        cols = slice(h * 128, (h + 1) * 128)
        x = nq_ref[:, cols]
        ms = jnp.sum(x * x, axis=-1, keepdims=True) * (1.0 / NSA_HD)
        qh = _rope64(x * lax.rsqrt(ms + EPS) * qn, cos, sin)
        qb_ref[:, cols] = _bf(qh * (NSA_HD ** -0.5))
    kc = _rope64(nkv_ref[:, 0:128], cos, sin)
    vc = nkv_ref[:, 128:256]
    ks = _rope64(_rms64(nkv_ref[:, 256:384], ksn_ref[...], bd), cos, sin)
    vs = nkv_ref[:, 384:512]
    kw = _rope64(_rms64(nkv_ref[:, 512:640], kwn_ref[...], bd), cos, sin)
    vw = nkv_ref[:, 640:768]
    for i, v in enumerate((kc, vc, ks, vs)):
        cache_ref[:, i * 128:(i + 1) * 128] = v
        cache_t_ref[i] = v.T
    for i, v in enumerate((kw, vw)):
        win_ref[:, i * 128:(i + 1) * 128] = v
        win_t_ref[i] = v.T
    kb_ref[0] = _bf(ks)
    kb_ref[1] = _bf(kw)
    for i, v in enumerate(_value_variants(vs) + _value_variants(vw)):
        vt_ref[i] = _bf(v.T)


def _nsa_prep(proj, cos, sin, li, depth, prev, qn, ksn, kwn):
    m = proj.shape[0]
    t = min(m, 256)
    extra, extra_specs, alias = _layer_out(prev)
    two = lambda w: jnp.concatenate([w, w]).reshape(1, LANES)
    full2 = lambda shape: pl.BlockSpec(shape, lambda i: (0, 0))
    return pl.pallas_call(
        functools.partial(_nsa_prep_kernel, li=li, fill=depth if prev is None else 0),
        grid=(m // t,),
        in_specs=[
            pl.BlockSpec((t, 1024), lambda i: (i, C_NQ // 1024)),
            pl.BlockSpec((t, 768), lambda i: (i, C_NKV // 768)),
            pl.BlockSpec((t, LANES), lambda i: (i, 0)),
            pl.BlockSpec((t, LANES), lambda i: (i, 0)),
            full2((1, LANES)), full2((1, LANES)), full2((1, LANES)),
        ] + extra_specs,
        out_specs=[
            pl.BlockSpec((t, 1024), lambda i: (i, 0)),
            pl.BlockSpec((t, 512), lambda i: (i, 0)),
            pl.BlockSpec((t, 256), lambda i: (i, 0)),
            _layer_spec(prev, depth, li, (4, LANES, t), lambda i: (0, 0, i)),
            pl.BlockSpec((2, LANES, t), lambda i: (0, 0, i)),
            pl.BlockSpec((2, t, LANES), lambda i: (0, i, 0)),
            pl.BlockSpec((8, LANES, t), lambda i: (0, 0, i)),
        ],
        out_shape=[
            jax.ShapeDtypeStruct((m, 1024), BF16),
            jax.ShapeDtypeStruct((m, 512), F32),
            jax.ShapeDtypeStruct((m, 256), F32),
            jax.ShapeDtypeStruct((depth, 4, LANES, m), F32),
            jax.ShapeDtypeStruct((2, LANES, m), F32),
            jax.ShapeDtypeStruct((2, m, LANES), BF16),
            jax.ShapeDtypeStruct((8, LANES, m), BF16),
        ],
        input_output_aliases={} if alias is None else {7: 3},
        compiler_params=_cparams(("arbitrary",)),
        name="nsa_prep",
    )(proj, proj, cos, sin, two(qn), two(ksn), two(kwn), *extra)


def _compress_kernel(rk_ref, rv_ref, w1_ref, pe_ref, w2_ref, kn_ref, kc_ref, vc_ref):
    tr = rk_ref.shape[0]
    bd = _half_mean_mat()

    def mlp(r, j):
        ha = _dot(_bf(r + pe_ref[2 * j:2 * j + 1, :]), w1_ref[2 * j])
        hb = _dot(_bf(r + pe_ref[2 * j + 1:2 * j + 2, :]), w1_ref[2 * j + 1])
        hid = ha + pltpu.roll(hb, tr - 1, 0)
        return _dot(_bf(_silu(hid)), w2_ref[j])

    kc = _rms64(mlp(rk_ref[...], 0), kn_ref[...], bd)
    vc = mlp(rv_ref[...], 1)
    kc_ref[...] = _bf(kc)
    vct = vc.T
    vc_ref[0] = _bf(vct)
    vc_ref[1] = _bf(pltpu.roll(vct, 64, 0))


def _compress_weights(pe, w1, w2):
    w1r = w1.reshape(2, 2, 16, 64, 64)
    z = jnp.zeros_like(w1r)
    top = jnp.concatenate([w1r, z], axis=-1)
    bot = jnp.concatenate([z, w1r], axis=-1)
    w1x = jnp.stack([top, bot], axis=3)
    w1x = w1x.reshape(4, 2048, LANES).astype(BF16)
    per = pe.reshape(2, 2, 16, 1, 64)
    pex = jnp.broadcast_to(per, (2, 2, 16, 2, 64)).reshape(4, 2048)
    z2 = jnp.zeros_like(w2)
    w2x = jnp.concatenate([jnp.concatenate([w2, z2], -1), jnp.concatenate([z2, w2], -1)], axis=1).astype(BF16)
    return w1x, pex, w2x


def _compress(rk, rv, w1x, pex, w2x, kn):
    nr = rk.shape[0]
    tr = min(nr, 512)
    two = jnp.concatenate([kn, kn]).reshape(1, LANES)
    return pl.pallas_call(
        _compress_kernel,
        grid=(nr // tr,),
        in_specs=[
            pl.BlockSpec((tr, 2048), lambda i: (i, 0)),
            pl.BlockSpec((tr, 2048), lambda i: (i, 0)),
            pl.BlockSpec((4, 2048, LANES), lambda i: (0, 0, 0)),
            pl.BlockSpec((4, 2048), lambda i: (0, 0)),
            pl.BlockSpec((2, LANES, LANES), lambda i: (0, 0, 0)),
            pl.BlockSpec((1, LANES), lambda i: (0, 0)),
        ],
        out_specs=[
            pl.BlockSpec((tr, LANES), lambda i: (i, 0)),
            pl.BlockSpec((2, LANES, tr), lambda i: (0, 0, i)),
        ],
        out_shape=[
            jax.ShapeDtypeStruct((nr, LANES), BF16),
            jax.ShapeDtypeStruct((2, LANES, nr), BF16),
        ],
        compiler_params=_cparams(("arbitrary",)),
        name="nsa_compress",
    )(rk, rv, w1x, pex, w2x, two)


QB = 128
SEL_TK = 512
WIN_TK = 128
RANK_STEP = 32


def _nsa_prompt_kernel(q_ref, kc_ref, vct_ref, kb_ref, vt_ref, ovt_ref, gate_ref, eg_ref, nz_ref, o_ref,
                       m_ref, acc_ref, st_ref, sb_ref):
    g = pl.program_id(0)
    qi = pl.program_id(1)
    t0 = qi * QB
    nc = kc_ref.shape[0]
    q = q_ref[...]
    qs = jnp.concatenate([q[:, 0:128], q[:, 256:384], q[:, 128:256], q[:, 384:512]], axis=0)
    tcol = t0 + (lax.broadcasted_iota(jnp.int32, (1, 4 * QB), 1) & (QB - 1))
    tq = t0 + lax.broadcasted_iota(jnp.int32, (1, QB), 1)

    def pv(vte, vto, pb):
        return jnp.concatenate([_dot(vte, pb[:, 0:2 * QB]), _dot(vto, pb[:, 2 * QB:4 * QB])], axis=1)

    sc = _dot_nt(kc_ref[...], qs)
    n_io = lax.broadcasted_iota(jnp.int32, (nc, 1), 0)
    mask_c = (n_io * CMP_STRIDE + (CMP_BLOCK - 1)) <= tcol
    sc = jnp.where(mask_c, sc, NEG)
    mc = jnp.max(sc, axis=0, keepdims=True)
    pc = jnp.where(mask_c, jnp.exp(sc - mc), 0.0)
    zc = jnp.sum(pc, axis=0, keepdims=True)
    pc = pc * (1.0 / jnp.where(zc > 0, zc, 1.0))
    o_c = pv(vct_ref[g], vct_ref[1 - g], _bf(pc))

    p4 = pc[:, 0:QB] + pc[:, QB:2 * QB] + pc[:, 2 * QB:3 * QB] + pc[:, 3 * QB:4 * QB]
    imp = _dot_tri(ovt_ref[...], p4)
    jb = lax.broadcasted_iota(jnp.int32, (LANES, QB), 0)
    cur = tq >> 6
    forced = (jb == 0) | ((jb <= cur) & (jb > cur - N_LOCAL))
    valid = jb <= cur
    score = jnp.where(forced, FORCE_SCORE, imp)
    score = jnp.where(valid, score, -FORCE_SCORE)
    st_ref[...] = score
    n_blk = ((t0 + QB - 1) >> 6) + 1
    sb_ref[...] = jnp.full(sb_ref.shape, NEG, F32)

    def rank_rows(nrows):
        sc_n = score[0:nrows]
        jb_n = jb[0:nrows]

        def rank_body(i, rank):
            row = st_ref[pl.ds(i, 1), :]
            beats = (row > sc_n) | ((row == sc_n) & (i < jb_n))
            return rank + jnp.where(beats, 1.0, 0.0)

        rank = lax.fori_loop(0, n_blk, rank_body, jnp.zeros((nrows, QB), F32))
        sb_ref[0:nrows, :] = jnp.where((rank < SLC_TOPN) & valid[0:nrows], 0.0, NEG)

    for nrows in range(RANK_STEP, LANES + 1, RANK_STEP):
        pl.when((n_blk > nrows - RANK_STEP) & (n_blk <= nrows))(functools.partial(rank_rows, nrows))

    def reset():
        m_ref[...] = jnp.full(m_ref.shape, NEG, F32)
        acc_ref[...] = jnp.zeros(acc_ref.shape, F32)

    def update(kidx, ve, vo, tk, tiles):
        m_old = m_ref[...]
        m_new = m_old
        ss = []
        for k0, bias, causal in tiles:
            s = _dot_nt(kb_ref[kidx, pl.ds(k0, tk), :], qs)
            if bias is not None:
                s = s + bias
            if causal:
                kpos = k0 + lax.broadcasted_iota(jnp.int32, (tk, 1), 0)
                s = jnp.where(kpos <= tcol, s, NEG)
            m_new = jnp.maximum(m_new, jnp.max(s, axis=0, keepdims=True))
            ss.append(s)
        acc = jnp.exp(m_old - m_new) * acc_ref[...]
        for (k0, _, _), s in zip(tiles, ss):
            acc = acc + pv(vt_ref[ve, :, pl.ds(k0, tk)], vt_ref[vo, :, pl.ds(k0, tk)], _bf(jnp.exp(s - m_new)))
        acc_ref[...] = acc
        m_ref[...] = m_new

    def result():
        a = acc_ref[...]
        den = pltpu.roll(a, 64, 0)
        return a / jnp.where(den > 0, den, 1.0)

    def sel_bias(k0):
        rows = sb_ref[pl.ds(pl.multiple_of(k0 // SLC_BLOCK, 8), SEL_TK // SLC_BLOCK), :]
        b = jnp.concatenate([jnp.broadcast_to(rows[r:r + 1, :], (SLC_BLOCK, QB))
                             for r in range(SEL_TK // SLC_BLOCK)], axis=0)
        return jnp.concatenate([b, b, b, b], axis=1)

    def sel_tile(k0, causal):
        return (k0, sel_bias(k0), causal)

    reset()
    vse = 2 * g
    n_full = t0 // SEL_TK

    def sel_body(kp, c):
        k0 = pl.multiple_of(kp * 2 * SEL_TK, 2 * SEL_TK)
        update(0, vse, vse + 1, SEL_TK, [sel_tile(k0, False), sel_tile(k0 + SEL_TK, False)])
        return c

    lax.fori_loop(0, n_full // 2, sel_body, 0)

    k_diag = pl.multiple_of(n_full * SEL_TK, SEL_TK)

    @pl.when(n_full % 2 == 1)
    def _():
        update(0, vse, vse + 1, SEL_TK, [sel_tile(k_diag - SEL_TK, False), sel_tile(k_diag, True)])

    @pl.when(n_full % 2 == 0)
    def _():
        update(0, vse, vse + 1, SEL_TK, [sel_tile(k_diag, True)])

    o_s = result()

    reset()
    vwe = 4 + 2 * g
    n_old = WINDOW // WIN_TK

    def win_tile(kt, bound, causal):
        k0 = pl.multiple_of(kt * WIN_TK, WIN_TK)
        bias = None
        if bound:
            kpos = k0 + lax.broadcasted_iota(jnp.int32, (WIN_TK, 1), 0)
            bias = jnp.where(tcol - kpos <= WINDOW, 0.0, NEG)
        return (k0, bias, causal)

    @pl.when(qi >= n_old)
    def _():
        update(1, vwe, vwe + 1, WIN_TK, [win_tile(qi - n_old, True, False)]
               + [win_tile(qi - n_old + i, False, False) for i in range(1, n_old)] + [win_tile(qi, False, True)])

    @pl.when(qi < n_old)
    def _():
        def win_body(kt, c):
            update(1, vwe, vwe + 1, WIN_TK, [win_tile(kt, False, False)])
            return c

        lax.fori_loop(0, qi, win_body, 0)
        update(1, vwe, vwe + 1, WIN_TK, [win_tile(qi, False, True)])

    o_w = result()

    gx = _dot_split(jax.nn.sigmoid(gate_ref[...]), eg_ref[0])
    lo_rows = lax.broadcasted_iota(jnp.int32, (LANES, QB), 0) < 64
    for k in range(2):
        ca = slice(k * QB, (k + 1) * QB)
        cb = slice(2 * QB + k * QB, 2 * QB + (k + 1) * QB)
        cols = slice(k * 128, (k + 1) * 128)
        tile = lambda o: jnp.where(lo_rows, o[:, ca], o[:, cb]).T
        o = (gx[:, k * 128:(k + 1) * 128] * tile(o_c)
             + gx[:, 256 + k * 128:256 + (k + 1) * 128] * tile(o_s)
             + gx[:, 512 + k * 128:512 + (k + 1) * 128] * tile(o_w))
        o_ref[:, cols] = o * _silu(nz_ref[:, cols])


def _overlap_mat(nc):
    n = np.arange(nc)[:, None]
    j = np.arange(LANES)[None, :]
    ov = ((n * CMP_STRIDE < (j + 1) * SLC_BLOCK) & (n * CMP_STRIDE + CMP_BLOCK - 1 >= j * SLC_BLOCK))
    return ov.astype(np.float32)


def _gate_expand_mat():
    eg = np.zeros((NSA_KV_HEADS, LANES, 3 * 256), np.float32)
    for g in range(NSA_KV_HEADS):
        for hh in range(4):
            for c in range(3):
                eg[g, (g * 4 + hh) * 3 + c, c * 256 + hh * 64:c * 256 + (hh + 1) * 64] = 1.0
    return jnp.asarray(eg, BF16)


def _nsa_prompt(qb, kc, vct, kb, vt, proj):
    l = qb.shape[0]
    nc = kc.shape[0]
    assert l // SLC_BLOCK <= LANES and l % SEL_TK == 0
    ovt = jnp.asarray(_overlap_mat(nc).T, BF16)
    return pl.pallas_call(
        _nsa_prompt_kernel,
        grid=(NSA_KV_HEADS, l // QB),
        in_specs=[
            pl.BlockSpec((QB, 512), lambda g, i: (i, g)),
            pl.BlockSpec((nc, LANES), lambda g, i: (0, 0)),
            pl.BlockSpec((2, LANES, nc), lambda g, i: (0, 0, 0)),
            pl.BlockSpec((2, l, LANES), lambda g, i: (0, 0, 0)),
            pl.BlockSpec((8, LANES, l), lambda g, i: (0, 0, 0)),
            pl.BlockSpec((LANES, nc), lambda g, i: (0, 0)),
            pl.BlockSpec((QB, LANES), lambda g, i: (i, C_GATE // LANES)),
            pl.BlockSpec((1, LANES, 768), lambda g, i: (g, 0, 0)),
            pl.BlockSpec((QB, 256), lambda g, i: (i, C_NZ // 256 + g)),
        ],
        out_specs=pl.BlockSpec((QB, 256), lambda g, i: (i, g)),
        out_shape=jax.ShapeDtypeStruct((l, NSA_WIDTH), F32),
        scratch_shapes=[
            pltpu.VMEM((1, 4 * QB), F32),
            pltpu.VMEM((LANES, 4 * QB), F32),
            pltpu.VMEM((LANES, QB), F32),
            pltpu.VMEM((LANES, QB), F32),
        ],
        compiler_params=_cparams(("arbitrary", "arbitrary")),
        name="nsa_prompt",
    )(qb, kc, vct, kb, vt, ovt, proj, _gate_expand_mat(), proj)


def _softmax_with_new(s, valid, s_new):
    s = jnp.where(valid, s, NEG)
    m = jnp.maximum(jnp.max(s, axis=1, keepdims=True), s_new)
    p = jnp.where(valid, jnp.exp(s - m), 0.0)
    pn = jnp.exp(s_new - m)
    z = jnp.sum(p, axis=1, keepdims=True) + pn
    return p / z, pn / z


DEC_ROWS = 2


def _nsa_decode_kernel(pt_ref, q_ref, cache_hbm, crow_ref, win_ref, wrow_ref, wcol_ref, gate_ref, nz_ref,
                       w1_ref, pe_ref, w2_ref, kn_ref, ov_ref, ek_ref, *rest, qpos, li, fill):
    o_ref, wo_ref, buf, tok_ref, sem = rest[-5:]
    wo_ref = _layer_view(wo_ref, li, fill)
    step = pl.program_id(0)
    slot = step % 2
    n_pages = buf.shape[2]
    page = buf.shape[5]
    t = n_pages * page
    nr = t // CMP_STRIDE
    wb = win_ref.shape[-1]
    n_cmp = (t + 1 - CMP_BLOCK) // CMP_STRIDE + 1

    def page_copies(st, s):
        return [pltpu.make_async_copy(cache_hbm.at[li, pt_ref[st * DEC_ROWS + r, p]], buf.at[s, r, p], sem.at[s])
                for r in range(DEC_ROWS) for p in range(n_pages)]

    @pl.when(step == 0)
    def _():
        for c in page_copies(0, 0):
            c.start()

    @pl.when(step + 1 < pl.num_programs(0))
    def _():
        for c in page_copies(step + 1, 1 - slot):
            c.start()

    for c in page_copies(step, slot):
        c.wait()

    bd = _half_mean_mat()
    r8 = lax.broadcasted_iota(jnp.int32, (8, 8), 0)
    c8 = lax.broadcasted_iota(jnp.int32, (8, 8), 1)
    gsum = jnp.where((r8 >> 2) == (c8 >> 2), 1.0, 0.0).astype(BF16)
    j_io = lax.broadcasted_iota(jnp.int32, (1, LANES), 1)
    cur = qpos // SLC_BLOCK
    forced = (j_io == 0) | ((j_io <= cur) & (j_io > cur - N_LOCAL))
    valid = j_io <= cur
    n_io = lax.broadcasted_iota(jnp.int32, (1, nr), 1)
    mask_c = ((n_io * CMP_STRIDE + (CMP_BLOCK - 1)) <= qpos) & (n_io < n_cmp)
    i_io = lax.broadcasted_iota(jnp.int32, (1, wb), 1)
    valid_w = ((wb - i_io) <= WINDOW) & ((qpos - wb + i_io) >= 0)
    lane8 = lax.broadcasted_iota(jnp.int32, (8, LANES), 1)
    row8 = lax.broadcasted_iota(jnp.int32, (8, LANES), 0)
    lo = lax.broadcasted_iota(jnp.int32, (1, LANES), 1) < 64
    last = lax.broadcasted_iota(jnp.int32, (LANES, wb), 1) == wb - 1
    ncol = wcol_ref.shape[2]

    def one_row(r):
        q8 = q_ref[r]
        q8f = q8.astype(F32)

        for p in range(n_pages):
            for j in range(2):
                tok_ref[r, j, p * page:(p + 1) * page, :] = buf[slot, r, p, j].T

        def mlp(j):
            acc_a = jnp.zeros((nr, LANES), F32)
            acc_b = jnp.zeros((nr, LANES), F32)
            for l2 in range(CMP_STRIDE // 2):
                x = jnp.concatenate([tok_ref[r, j, pl.ds(2 * l2, nr, stride=CMP_STRIDE), :],
                                     tok_ref[r, j, pl.ds(2 * l2 + 1, nr, stride=CMP_STRIDE), :]], axis=1)
                cols = slice(l2 * 256, (l2 + 1) * 256)
                acc_a = acc_a + _dot(_bf(x + pe_ref[2 * j:2 * j + 1, cols]), w1_ref[2 * j, cols, :])
                acc_b = acc_b + _dot(_bf(x + pe_ref[2 * j + 1:2 * j + 2, cols]), w1_ref[2 * j + 1, cols, :])
            hid = acc_a + pltpu.roll(acc_b, nr - 1, 0)
            return _dot(_bf(_silu(hid)), w2_ref[j])

        kcc = _rms64(mlp(0), kn_ref[...], bd)
        vcc = mlp(1)

        sc = jnp.where(mask_c, _dot_nt(q8, _bf(kcc)), NEG)
        mc = jnp.max(sc, axis=1, keepdims=True)
        pc = jnp.where(mask_c, jnp.exp(sc - mc), 0.0)
        zc = jnp.sum(pc, axis=1, keepdims=True)
        pc = pc / jnp.where(zc > 0, zc, 1.0)
        o_c = _dot(_bf(pc), _bf(vcc))

        imp = _dot_split(_dot_tri(gsum, pc), ov_ref[...])
        score = jnp.where(forced, FORCE_SCORE, imp)
        score = jnp.where(valid, score, -FORCE_SCORE)
        rank = jnp.zeros((8, LANES), F32)
        for i in range(cur + 1):
            col = score[:, i:i + 1]
            rank = rank + jnp.where((col > score) | ((col == score) & (i < j_io)), 1.0, 0.0)
        sel = jnp.where((rank < SLC_TOPN) & valid, 1.0, 0.0)

        crow = crow_ref[r]
        ks_new = _bf(crow[:, 256:384]).astype(F32)
        vs_new = _bf(crow[:, 384:512]).astype(F32)
        sel_past = _dot(_bf(sel), ek_ref[...]) > 0.5
        s_s = jnp.concatenate([_dot(q8, _bf(buf[slot, r, p, 2])) for p in range(n_pages)], axis=1)
        s_new = jnp.where(sel[:, cur:cur + 1] > 0.5, jnp.sum(q8f * ks_new, axis=-1, keepdims=True), NEG)
        p_s, pn_s = _softmax_with_new(s_s, sel_past, s_new)
        p_sb = _bf(p_s)
        o_s = _bf(pn_s).astype(F32) * vs_new
        for p in range(n_pages):
            o_s = o_s + _dot_nt(p_sb[:, p * page:(p + 1) * page], _bf(buf[slot, r, p, 3]))

        wrow = wrow_ref[r]
        kw_new = _bf(wrow[:, 0:128]).astype(F32)
        vw_new = _bf(wrow[:, 128:256]).astype(F32)
        s_w = _dot(q8, _bf(win_ref[r, 0]))
        p_w, pn_w = _softmax_with_new(s_w, valid_w, jnp.sum(q8f * kw_new, axis=-1, keepdims=True))
        o_w = _dot_nt(_bf(p_w), _bf(win_ref[r, 1])) + _bf(pn_w).astype(F32) * vw_new

        sig = jnp.broadcast_to(jax.nn.sigmoid(gate_ref[r]), (8, LANES))
        gate = lambda c: jnp.sum(jnp.where(lane8 == row8 * 3 + c, sig, 0.0), axis=-1, keepdims=True)
        o8 = gate(0) * o_c + gate(1) * o_s + gate(2) * o_w

        sw = pltpu.roll(o8, 64, 1)
        flat = jnp.concatenate([
            jnp.where(lo, o8[0:1], sw[1:2]), jnp.where(lo, o8[2:3], sw[3:4]),
            jnp.where(lo, sw[4:5], o8[5:6]), jnp.where(lo, sw[6:7], o8[7:8])], axis=1)
        o_ref[r] = flat * _silu(nz_ref[r])

        own_col = lax.broadcasted_iota(jnp.int32, (LANES, ncol), 1) == ((step * DEC_ROWS + r) % ncol)
        for j in range(2):
            col = jnp.sum(jnp.where(own_col, wcol_ref[j], 0.0), axis=1, keepdims=True)
            wo_ref[r, j] = jnp.where(last, col, pltpu.roll(win_ref[r, j], wb - 1, 1))

    for r in range(DEC_ROWS):
        one_row(r)


def _nsa_decode(pt, q8, cache_t, crow, win_t, wrow, wcol, gate, nz, w1x, pex, w2x, kn, qpos, li, prev):
    db, n_pages = pt.shape
    page = cache_t.shape[-1]
    t = n_pages * page
    wb = win_t.shape[-1]
    nr = t // CMP_STRIDE
    assert nr == LANES and page == LANES and t % SLC_BLOCK == 0 and qpos // SLC_BLOCK < LANES and db % LANES == 0
    ek = (np.arange(LANES)[:, None] == (np.arange(t)[None, :] // SLC_BLOCK))
    ek = jnp.asarray(ek.astype(np.float32), BF16)
    ov = jnp.asarray(_overlap_mat(nr), BF16)
    two = jnp.concatenate([kn, kn]).reshape(1, LANES)
    rows = DEC_ROWS
    row3 = lambda w: pl.BlockSpec((rows, 1, w), lambda i, pt: (i, 0, 0))
    full = lambda a: pl.BlockSpec(a.shape, lambda i, pt: (0,) * a.ndim)
    win_blk = (rows, 2, LANES, wb)
    extra, extra_specs, alias = _layer_out(prev)
    depth = win_t.shape[0]
    return pl.pallas_call(
        functools.partial(_nsa_decode_kernel, qpos=qpos, li=li, fill=depth if prev is None else 0),
        grid_spec=pltpu.PrefetchScalarGridSpec(
            num_scalar_prefetch=1,
            grid=(db // rows,),
            in_specs=[
                pl.BlockSpec((rows, 8, LANES), lambda i, pt: (i, 0, 0)),
                pl.BlockSpec(memory_space=pl.ANY),
                row3(512),
                pl.BlockSpec((None,) + win_blk, lambda i, pt: (li, i, 0, 0, 0)),
                row3(256),
                pl.BlockSpec((2, LANES, LANES), lambda i, pt: (0, 0, (i * rows) // LANES)),
                row3(LANES), row3(512),
                full(w1x), full(pex), full(w2x), full(two), full(ov), full(ek),
            ] + extra_specs,
            out_specs=[row3(512), _layer_spec(prev, depth, li, win_blk, lambda i, pt: (i, 0, 0, 0))],
            scratch_shapes=[
                pltpu.VMEM((2, rows, n_pages, 4, LANES, page), F32),
                pltpu.VMEM((rows, 2, t, LANES), F32),
                pltpu.SemaphoreType.DMA((2,)),
            ],
        ),
        out_shape=[jax.ShapeDtypeStruct((db, 1, NSA_WIDTH), F32), jax.ShapeDtypeStruct(win_t.shape, F32)],
        input_output_aliases={} if alias is None else {15: 1},
        compiler_params=_cparams(("arbitrary",)),
        name="nsa_decode",
    )(pt, q8, cache_t, crow, win_t, wrow, wcol, gate, nz, w1x, pex, w2x, two, ov, ek, *extra)


def _mem_kv_kernel(mem_ref, nw_ref, wk_ref, wv_ref, kn_ref, kv_ref, kvb_ref):
    ml = mem_ref.shape[0]
    x = mem_ref[...]
    m = _bf(x * lax.rsqrt(jnp.mean(x * x, axis=-1, keepdims=True) + EPS) * nw_ref[...])
    k = _dot(m, wk_ref[...])
    v = _dot(m, wv_ref[...])
    for h in range(MEM_HEADS):
        cols = slice(h * 128, (h + 1) * 128)
        kh = k[:, cols]
        kh = kh * lax.rsqrt(jnp.mean(kh * kh, axis=-1, keepdims=True) + EPS) * kn_ref[...]
        kv_ref[pl.ds(h, ml, stride=2 * MEM_HEADS), :] = kh
        kv_ref[pl.ds(MEM_HEADS + h, ml, stride=2 * MEM_HEADS), :] = v[:, cols]
        kvb_ref[:, cols] = _bf(kh)
    kvb_ref[:, MEM_WIDTH:2 * MEM_WIDTH] = _bf(v)


def _mem_kv(mem, nw, wk, wv, kn):
    ml = mem.shape[0]
    return pl.pallas_call(
        _mem_kv_kernel,
        out_shape=[jax.ShapeDtypeStruct((ml * 2 * MEM_HEADS, MEM_HD), F32),
                   jax.ShapeDtypeStruct((ml, 2 * MEM_WIDTH), BF16)],
        compiler_params=pltpu.CompilerParams(vmem_limit_bytes=VMEM_LIMIT),
        name="mem_kv",
    )(mem, nw.reshape(1, -1), wk, wv, kn.reshape(1, -1))


TAIL_T = 512


def _out_proj(x_ref, ret_ref, ssd_ref, nsa_ref, wout_ref):
    return (x_ref[...] + _dot(_bf(ret_ref[...]), wout_ref[0:512, :])
            + _dot(_bf(ssd_ref[...]), wout_ref[512:1536, :])
            + _dot(_bf(nsa_ref[...]), wout_ref[1536:2048, :]))


def _cross_q(x1, ncw_ref, wq_ref, qn_ref):
    h = _bf(x1 * lax.rsqrt(jnp.mean(x1 * x1, axis=-1, keepdims=True) + EPS) * ncw_ref[...])
    q = _dot(h, wq_ref[...])
    out = []
    for hd in range(MEM_HEADS):
        qh = q[:, hd * 128:(hd + 1) * 128]
        out.append(_bf(qh * lax.rsqrt(jnp.mean(qh * qh, axis=-1, keepdims=True) + EPS) * qn_ref[...]))
    return out


def _tail_kernel(x_ref, ret_ref, ssd_ref, nsa_ref, wout_ref, ncw_ref, wq_ref, qn_ref, kvb_ref, wo_ref, y_ref):
    x1 = _out_proj(x_ref, ret_ref, ssd_ref, nsa_ref, wout_ref)
    outs = []
    for hd, qh in enumerate(_cross_q(x1, ncw_ref, wq_ref, qn_ref)):
        s = _dot_nt(qh, kvb_ref[:, hd * 128:(hd + 1) * 128]) * (MEM_HD ** -0.5)
        s = s - jnp.max(s, axis=-1, keepdims=True)
        p = jnp.exp(s)
        p = p / jnp.sum(p, axis=-1, keepdims=True)
        outs.append(_dot(_bf(p), kvb_ref[:, MEM_WIDTH + hd * 128:MEM_WIDTH + (hd + 1) * 128]))
    y_ref[...] = x1 + _dot(_bf(jnp.concatenate(outs, axis=1)), wo_ref[...])


def _tail(x, ret, ssd, nsa, wout, ncw, wq, qn, kvb, wo):
    m = x.shape[0]
    t = min(m, TAIL_T)
    ml = kvb.shape[0]
    full2 = lambda shape: pl.BlockSpec(shape, lambda i: (0, 0))
    return pl.pallas_call(
        _tail_kernel,
        grid=(m // t,),
        in_specs=[
            pl.BlockSpec((t, D_MODEL), lambda i: (i, 0)),
            pl.BlockSpec((t, RET_WIDTH), lambda i: (i, 0)),
            pl.BlockSpec((t, SSD_WIDTH), lambda i: (i, 0)),
            pl.BlockSpec((t, NSA_WIDTH), lambda i: (i, 0)),
            full2((D_MODEL, D_MODEL)),
            full2((1, D_MODEL)),
            full2((D_MODEL, MEM_WIDTH)),
            full2((1, MEM_HD)),
            full2((ml, 2 * MEM_WIDTH)),
            full2((MEM_WIDTH, D_MODEL)),
        ],
        out_specs=pl.BlockSpec((t, D_MODEL), lambda i: (i, 0)),
        out_shape=jax.ShapeDtypeStruct((m, D_MODEL), F32),
        compiler_params=_cparams(("arbitrary",)),
        name="layer_tail",
    )(x, ret, ssd, nsa, wout, ncw.reshape(1, -1), wq, qn.reshape(1, -1), kvb, wo)


def _dec_tail_a_kernel(x_ref, ret_ref, ssd_ref, nsa_ref, wout_ref, ncw_ref, wq_ref, qn_ref, x1_ref, q_ref):
    x1 = _out_proj(x_ref, ret_ref, ssd_ref, nsa_ref, wout_ref)
    x1_ref[...] = x1
    for hd, qh in enumerate(_cross_q(x1, ncw_ref, wq_ref, qn_ref)):
        q_ref[:, hd * 128:(hd + 1) * 128] = qh


def _dec_tail_a(x, ret, ssd, nsa, wout, ncw, wq, qn):
    m = x.shape[0]
    return pl.pallas_call(
        _dec_tail_a_kernel,
        out_shape=[jax.ShapeDtypeStruct((m, D_MODEL), F32), jax.ShapeDtypeStruct((m, MEM_WIDTH), BF16)],
        compiler_params=pltpu.CompilerParams(vmem_limit_bytes=VMEM_LIMIT),
        name="dec_tail_a",
    )(x, ret, ssd, nsa, wout, ncw.reshape(1, -1), wq, qn.reshape(1, -1))


def _dec_xattn_kernel(q_ref, mem_ref, x1_ref, wo_ref, y_ref):
    ml = mem_ref.shape[-2] // (2 * MEM_HEADS)
    lane = lax.broadcasted_iota(jnp.int32, (8, MEM_WIDTH), 1)
    row = lax.broadcasted_iota(jnp.int32, (8, MEM_WIDTH), 0)
    own = (lane >> 7) == row
    outs = []
    for b in range(DEC_BB):
        q4 = _bf(jnp.where(own, jnp.broadcast_to(q_ref[b:b + 1, :].astype(F32), (8, MEM_WIDTH)), 0.0))
        kb = jnp.concatenate([_bf(mem_ref[b, pl.ds(h, ml, stride=2 * MEM_HEADS), :]) for h in range(MEM_HEADS)], axis=1)
        vb = jnp.concatenate([_bf(mem_ref[b, pl.ds(MEM_HEADS + h, ml, stride=2 * MEM_HEADS), :])
                              for h in range(MEM_HEADS)], axis=1)
        s = _dot_nt(q4, kb) * (MEM_HD ** -0.5)
        s = s - jnp.max(s, axis=-1, keepdims=True)
        p = jnp.exp(s)
        p = p / jnp.sum(p, axis=-1, keepdims=True)
        o4 = _dot(_bf(p), vb)
        outs.append(jnp.sum(jnp.where(own, o4, 0.0), axis=0, keepdims=True))
    y_ref[...] = x1_ref[...] + _dot(_bf(jnp.concatenate(outs, axis=0)), wo_ref[...])


def _dec_xattn(q, mem, x1, wo, li):
    db = q.shape[0]
    rows = mem.shape[2]
    return pl.pallas_call(
        _dec_xattn_kernel,
        grid=(db // DEC_BB,),
        in_specs=[
            pl.BlockSpec((DEC_BB, MEM_WIDTH), lambda i: (i, 0)),
            pl.BlockSpec((None, DEC_BB, rows, MEM_HD), lambda i: (li, i, 0, 0)),
            pl.BlockSpec((DEC_BB, D_MODEL), lambda i: (i, 0)),
            pl.BlockSpec((MEM_WIDTH, D_MODEL), lambda i: (0, 0)),
        ],
        out_specs=pl.BlockSpec((DEC_BB, D_MODEL), lambda i: (i, 0)),
        out_shape=jax.ShapeDtypeStruct((db, D_MODEL), F32),
        compiler_params=_cparams(("arbitrary",)),
        name="dec_xattn",
    )(q, mem, x1, wo)


def _prep_w_tail(wt):
    k = wt.shape[1]
    nq = wt[4624:5136].reshape(NSA_HEADS, NSA_HD, k)
    z = jnp.zeros_like(nq)
    nq_pad = jnp.concatenate([
        jnp.concatenate([nq[:4], z[:4]], axis=1),
        jnp.concatenate([z[4:], nq[4:]], axis=1)], axis=0).reshape(NSA_HEADS * LANES, k)
    padr = lambda a: jnp.pad(a, ((0, LANES - a.shape[0]), (0, 0)))
    return jnp.concatenate([
        wt[5928:6440],
        nq_pad,
        wt[5136:5904],
        padr(wt[4608:4624]),
        padr(wt[5904:5928]),
    ], axis=0)


def _rope_tables(pos, head_dim, rows):
    half = head_dim // 2
    inv = jnp.exp(-math.log(ROPE_THETA) * jnp.arange(half, dtype=F32) / half)
    ang = pos.astype(F32)[:, None] * inv[None, :]
    cos = jnp.cos(ang)
    sin = jnp.sin(ang)
    reps = LANES // head_dim
    cos_t = jnp.tile(jnp.concatenate([cos, cos], axis=-1), (1, reps))
    sin_t = jnp.tile(jnp.concatenate([-sin, sin], axis=-1), (1, reps))
    if cos_t.shape[0] != rows:
        cos_t = jnp.broadcast_to(cos_t, (rows, LANES))
        sin_t = jnp.broadcast_to(sin_t, (rows, LANES))
    return cos_t, sin_t


def kernel(x_prompt, x_sample, mem_prompt, state_ret, state_ssm, state_conv, cache_nsa_kv, cache_win_kv,
           cache_mem_kv, page_table, norm_mix, w_in, ssd_conv_w, ssd_conv_b, ssd_dt_bias, ssd_a_log, ssd_d,
           ssd_norm, nsa_q_norm, nsa_kc_norm, nsa_ks_norm, nsa_kw_norm, nsa_cmp_pe, nsa_cmp_w1, nsa_cmp_w2,
           w_out, norm_cross, norm_mem, mem_wq, mem_wk, mem_wv, mem_q_norm, mem_k_norm, mem_wo):
    b, l, _ = x_prompt.shape
    assert b == 1
    db, dl, _ = x_sample.shape
    assert dl == 1 and db % DEC_BB == 0
    depth = w_in.shape[0]
    n_pages = page_table.shape[1]
    page = cache_nsa_kv.shape[2]
    past_len = n_pages * page
    wbuf = cache_win_kv.shape[2]
    ml = cache_mem_kv.shape[2]
    wp = min(WINDOW, l)
    pos_p = jnp.arange(l, dtype=jnp.int32)
    pos_s = jnp.full((1,), past_len, dtype=jnp.int32)
    cos128, sin128 = _rope_tables(pos_p, RET_DK, l)
    cos64, sin64 = _rope_tables(pos_p, NSA_HD, l)
    cos128s, sin128s = _rope_tables(pos_s, RET_DK, db)
    cos64s, sin64s = _rope_tables(pos_s, NSA_HD, db)

    xp = x_prompt[0]
    xs = x_sample[:, 0, :]
    w_in_t = jnp.swapaxes(w_in, 1, 2)
    cache_t = jnp.transpose(cache_nsa_kv, (0, 1, 3, 4, 5, 2)).reshape(depth, -1, 4, LANES, page)
    win_t = jnp.transpose(cache_win_kv, (0, 1, 3, 4, 5, 2)).reshape(depth, db, 2, LANES, wbuf)
    conv_t = jnp.transpose(state_conv, (0, 2, 1, 3))
    mem_rows = cache_mem_kv.reshape(depth, db, ml * 2 * MEM_HEADS, MEM_HD)
    untok = lambda a: jnp.moveaxis(a.reshape(a.shape[:-2] + (NSA_KV_HEADS, NSA_HD, a.shape[-1])), -1, -4)
    ret_p, ssm_p, conv_p, win_p, mem_p, conv_s = [], [], [], [], [], []
    cache_p = cache_s = ret_s_all = ssm_s_all = win_s_all = None
    for li in range(depth):
        w_tail = _prep_w_tail(w_in_t[li])
        w_out_b = w_out[li].astype(BF16)
        wq_b = mem_wq[li].astype(BF16)
        wo_b = mem_wo[li].astype(BF16)
        w1x, pex, w2x = _compress_weights(nsa_cmp_pe[li], nsa_cmp_w1[li], nsa_cmp_w2[li])
        ssd_w = (ssd_conv_w[li], ssd_conv_b[li], ssd_dt_bias[li], ssd_a_log[li], ssd_d[li], ssd_norm[li])
        nsa_n = (nsa_q_norm[li], nsa_ks_norm[li], nsa_kw_norm[li])
        proj = _inproj(xp, norm_mix[li], w_in_t, w_tail, li)
        ret_out, ret_s = _ret_prompt(proj, cos128, sin128)
        ssd_out, ssm_h, conv8 = _ssd_prompt(proj, *ssd_w)
        qb, cache, _, cache_p, win_tp, kb, vt = _nsa_prep(proj, cos64, sin64, li, depth, cache_p, *nsa_n)
        rk = cache[:, 0:128].reshape(l // CMP_STRIDE, CMP_STRIDE * LANES)
        rv = cache[:, 128:256].reshape(l // CMP_STRIDE, CMP_STRIDE * LANES)
        kc_b, vct = _compress(rk, rv, w1x, pex, w2x, nsa_kc_norm[li])
        nsa_out = _nsa_prompt(qb, kc_b, vct, kb, vt, proj)
        mkv, mkv_b = _mem_kv(mem_prompt[0], norm_mem[li], mem_wk[li].astype(BF16), mem_wv[li].astype(BF16),
                             mem_k_norm[li])
        xp = _tail(xp, ret_out, ssd_out, nsa_out, w_out_b, norm_cross[li], wq_b, mem_q_norm[li], mkv_b, wo_b)
        ret_p.append(ret_s[None])
        ssm_p.append(ssm_h[None])
        conv_p.append(conv8[None, 8 - (SSD_CONV - 1):])
        win_p.append(untok(win_tp[:, :, l - wp:])[None])
        mem_p.append(mkv.reshape(1, ml, 2, MEM_HEADS, MEM_HD))
        sproj = _inproj(xs, norm_mix[li], w_in_t, w_tail, li)
        s_ret_out, ret_s_all = _ret_decode(sproj, cos128s, sin128s, state_ret, li, ret_s_all)
        s_ssd_out, s_conv, ssm_s_all = _ssd_decode(sproj, conv_t, state_ssm, li, ssm_s_all, *ssd_w)
        s_qb, s_cache, s_wrow, cache_s, s_wcol, _, _ = _nsa_prep(sproj, cos64s, sin64s, li, depth, cache_s, *nsa_n)
        s_nsa, win_s_all = _nsa_decode(
            page_table, s_qb.reshape(db, NSA_HEADS, LANES), cache_t, s_cache[:, None, :], win_t,
            s_wrow[:, None, :], s_wcol, sproj[:, None, C_GATE:C_GATE + LANES],
            sproj[:, None, C_NZ:C_NZ + NSA_WIDTH], w1x, pex, w2x, nsa_kc_norm[li], past_len, li, win_s_all)
        x1, s_q = _dec_tail_a(xs, s_ret_out, s_ssd_out, s_nsa[:, 0, :], w_out_b, norm_cross[li], wq_b,
                              mem_q_norm[li])
        xs = _dec_xattn(s_q, mem_rows, x1, wo_b, li)
        conv_s.append(jnp.transpose(s_conv, (1, 0, 2)))
    stack = jnp.stack
    return (xp[None], xs[:, None, :], stack(ret_p), ret_s_all, stack(ssm_p), ssm_s_all, stack(conv_p), stack(conv_s),
            untok(cache_p)[:, None], untok(cache_s)[:, :, None], stack(win_p), untok(win_s_all), stack(mem_p))
```

```python
import functools
import math

import numpy as np
import jax
import jax.numpy as jnp
from jax import lax
from jax.experimental import pallas as pl
from jax.experimental.pallas import tpu as pltpu

F32 = jnp.float32
BF16 = jnp.bfloat16

D_MODEL = 2048
RET_HEADS = 4
RET_DK = 128
RET_WIDTH = 512
CHUNK = 128
SSD_WIDTH = 1024
SSD_HEADDIM = 64
SSD_HEADS = 16
SSD_GROUPS = 2
SSD_STATE = 128
SSD_CONV = 4
SSD_CONV_DIM = 1536
NSA_WIDTH = 512
NSA_HEADS = 8
NSA_HD = 64
NSA_KV_HEADS = 2
CMP_BLOCK = 32
CMP_STRIDE = 16
SLC_BLOCK = 64
SLC_TOPN = 16
N_LOCAL = 2
WINDOW = 512
FORCE_SCORE = 1.0e6
MEM_HEADS = 4
MEM_HD = 128
MEM_WIDTH = 512
ROPE_THETA = 10000.0
EPS = 1e-6
NEG = -1.0e30

C_SZ = 2048
C_XBC = 3072
C_NZ = 4608
C_NQ = 5120
C_NKV = 6144
C_SDT = 6912
C_GATE = 7040
N_PROJ = 7168

LANES = 128
VMEM_LIMIT = 56 * 1024 * 1024


def _cparams(sem):
    return pltpu.CompilerParams(dimension_semantics=sem, vmem_limit_bytes=VMEM_LIMIT)


def _bf(x):
    return x.astype(BF16)


def _dot(a, b):
    return jnp.dot(a, b, preferred_element_type=F32)


def _dot_nt(a, b):
    return lax.dot_general(a, b, (((1,), (1,)), ((), ())), preferred_element_type=F32)


def _dot_tn(a, b):
    return lax.dot_general(a, b, (((0,), (0,)), ((), ())), preferred_element_type=F32)


def _split3(a):
    hi = a.astype(BF16)
    r = a - hi.astype(F32)
    mid = r.astype(BF16)
    lo = (r - mid.astype(F32)).astype(BF16)
    return hi, mid, lo


def _split2(a):
    hi = a.astype(BF16)
    lo = (a - hi.astype(F32)).astype(BF16)
    return hi, lo


def _dot_split(a, b_bf16):
    hi, mid, lo = _split3(a)
    return _dot(hi, b_bf16) + _dot(mid, b_bf16) + _dot(lo, b_bf16)


def _dot_tri(tri_bf16, a):
    hi, mid, lo = _split3(a)
    return _dot(tri_bf16, hi) + _dot(tri_bf16, mid) + _dot(tri_bf16, lo)


def _silu(x):
    return x * jax.nn.sigmoid(x)


def _softplus(x):
    return jnp.maximum(x, 0.0) + jnp.log1p(jnp.exp(-jnp.abs(x)))


def _rope128(x, cos, sin):
    return x * cos + pltpu.roll(x, 64, 1) * sin


def _rope64(x, cos, sin):
    lane = lax.broadcasted_iota(jnp.int32, x.shape, 1)
    first = (lane & 63) < 32
    partner = jnp.where(first, pltpu.roll(x, 96, 1), pltpu.roll(x, 32, 1))
    return x * cos + partner * sin


def _half_mean_mat():
    r = lax.broadcasted_iota(jnp.int32, (LANES, LANES), 0)
    c = lax.broadcasted_iota(jnp.int32, (LANES, LANES), 1)
    return jnp.where((r >> 6) == (c >> 6), 1.0 / 64.0, 0.0).astype(BF16)


def _rms64(x, w, bd):
    ms = _dot_split(x * x, bd)
    return x * lax.rsqrt(ms + EPS) * w


def _pad_lanes(v, n=LANES):
    v = v.reshape(1, -1)
    return jnp.pad(v, ((0, 0), (0, n - v.shape[1])))


def _layer_out(prev):
    if prev is None:
        return [], [], None
    return [prev], [pl.BlockSpec(memory_space=pl.ANY)], prev


def _layer_spec(prev, depth, li, block, index):
    if prev is None:
        return pl.BlockSpec((depth,) + block, lambda *a: (0,) + index(*a))
    return pl.BlockSpec((None,) + block, lambda *a: (li,) + index(*a))


def _layer_view(ref, li, fill_depth):
    if not fill_depth:
        return ref
    for d in range(fill_depth):
        if d != li:
            ref[d] = jnp.zeros(ref.shape[1:], ref.dtype)
    return ref.at[li]


def _row_only(x, b):
    rid = lax.broadcasted_iota(jnp.int32, (x.shape[0], 1), 0)
    return jnp.where(rid == b, x, jnp.zeros_like(x))


N_MAIN = 4608
PROJ_TN = 512


def _inproj_kernel(x_ref, nw_ref, wa_ref, wb_ref, o_ref, h_ref):
    j = pl.program_id(1)

    @pl.when(j == 0)
    def _():
        nw = nw_ref[...]
        rows = min(128, x_ref.shape[0])

        def body(i, c):
            r = pl.ds(pl.multiple_of(i * rows, rows), rows)
            x = x_ref[r, :]
            ms = jnp.mean(x * x, axis=-1, keepdims=True)
            h_ref[r, :] = (x * lax.rsqrt(ms + EPS) * nw).astype(BF16)
            return c

        lax.fori_loop(0, x_ref.shape[0] // rows, body, 0)

    @pl.when(j < N_MAIN // PROJ_TN)
    def _():
        o_ref[...] = _dot_nt(h_ref[...], _bf(wa_ref[...]))

    @pl.when(j >= N_MAIN // PROJ_TN)
    def _():
        o_ref[...] = _dot_nt(h_ref[...], _bf(wb_ref[...]))


def _inproj(x, nw, w_t, w_tail, li):
    m = x.shape[0]
    tm = min(m, 1024)
    nja = N_MAIN // PROJ_TN
    return pl.pallas_call(
        _inproj_kernel,
        grid=(m // tm, N_PROJ // PROJ_TN),
        in_specs=[
            pl.BlockSpec((tm, D_MODEL), lambda i, j: (i, 0)),
            pl.BlockSpec((1, D_MODEL), lambda i, j: (0, 0)),
            pl.BlockSpec((None, PROJ_TN, D_MODEL), lambda i, j: (li, jnp.minimum(j, nja - 1), 0)),
            pl.BlockSpec((PROJ_TN, D_MODEL), lambda i, j: (jnp.maximum(j - nja, 0), 0)),
        ],
        out_specs=pl.BlockSpec((tm, PROJ_TN), lambda i, j: (i, j)),
        out_shape=jax.ShapeDtypeStruct((m, N_PROJ), F32),
        scratch_shapes=[pltpu.VMEM((tm, D_MODEL), BF16)],
        compiler_params=_cparams(("arbitrary", "arbitrary")),
        name="inproj",
    )(x, nw.reshape(1, D_MODEL), w_t, w_tail)


RET_T = 512


def _ret_prompt_kernel(q_ref, k_ref, v_ref, g_ref, cos_ref, sin_ref, dec_ref, qd_ref, kd_ref, cd_ref,
                       o_ref, so_ref, s_ref):
    i = pl.program_id(0)

    @pl.when(i == 0)
    def _():
        s_ref[...] = jnp.zeros(s_ref.shape, F32)

    for c in range(RET_T // CHUNK):
        rows = slice(c * CHUNK, (c + 1) * CHUNK)
        cos = cos_ref[rows, :]
        sin = sin_ref[rows, :]
        for h in range(RET_HEADS):
            cols = slice(h * 128, (h + 1) * 128)
            q = _rope128(q_ref[rows, cols], cos, sin)
            k = _rope128(k_ref[rows, cols], cos, sin) * (RET_DK ** -0.5)
            v = v_ref[rows, cols]
            s = s_ref[h]
            qb = _bf(q)
            vb = _bf(v)
            att = _dot_nt(qb, _bf(k)) * dec_ref[h]
            o = _dot(_bf(att), vb) + _dot(qb, _bf(s)) * qd_ref[h]
            s_ref[h] = s * cd_ref[h] + _dot_tn(_bf(k * kd_ref[h]), vb)
            r = o * lax.rsqrt(jnp.mean(o * o, axis=-1, keepdims=True) + EPS)
            o_ref[rows, cols] = r * _silu(g_ref[rows, cols])

    @pl.when(i == pl.num_programs(0) - 1)
    def _():
        so_ref[...] = s_ref[...]


def _ret_gamma():
    return 1.0 - np.exp2(-5.0 - np.arange(RET_HEADS, dtype=np.float64))


def _ret_consts():
    lg = np.log(_ret_gamma())
    idx = np.arange(CHUNK, dtype=np.float64)
    diff = idx[:, None] - idx[None, :]
    dec = np.where(diff[None] >= 0, np.exp(lg[:, None, None] * np.maximum(diff, 0.0)[None]), 0.0)
    qd = np.exp(lg[:, None] * (idx + 1.0)[None])
    kd = np.exp(lg[:, None] * (CHUNK - 1.0 - idx)[None])
    cd = np.exp(lg * CHUNK)
    bc = lambda a: np.ascontiguousarray(np.broadcast_to(a[:, :, None], (RET_HEADS, CHUNK, LANES)))
    return (jnp.asarray(dec, F32), jnp.asarray(bc(qd), F32), jnp.asarray(bc(kd), F32),
            jnp.asarray(np.broadcast_to(cd[:, None, None], (RET_HEADS, 1, LANES)).copy(), F32))


def _ret_prompt(proj, cos, sin):
    l = proj.shape[0]
    dec, qd, kd, cd = _ret_consts()
    full3 = lambda shape: pl.BlockSpec(shape, lambda i: (0, 0, 0))
    return pl.pallas_call(
        _ret_prompt_kernel,
        grid=(l // RET_T,),
        in_specs=[
            pl.BlockSpec((RET_T, 512), lambda i: (i, 0)),
            pl.BlockSpec((RET_T, 512), lambda i: (i, 1)),
            pl.BlockSpec((RET_T, 512), lambda i: (i, 2)),
            pl.BlockSpec((RET_T, 512), lambda i: (i, 3)),
            pl.BlockSpec((RET_T, LANES), lambda i: (i, 0)),
            pl.BlockSpec((RET_T, LANES), lambda i: (i, 0)),
            full3((RET_HEADS, CHUNK, CHUNK)),
            full3((RET_HEADS, CHUNK, LANES)),
            full3((RET_HEADS, CHUNK, LANES)),
            full3((RET_HEADS, 1, LANES)),
        ],
        out_specs=[
            pl.BlockSpec((RET_T, 512), lambda i: (i, 0)),
            full3((RET_HEADS, RET_DK, RET_DK)),
        ],
        out_shape=[
            jax.ShapeDtypeStruct((l, RET_WIDTH), F32),
            jax.ShapeDtypeStruct((RET_HEADS, RET_DK, RET_DK), F32),
        ],
        scratch_shapes=[pltpu.VMEM((RET_HEADS, RET_DK, RET_DK), F32)],
        compiler_params=_cparams(("arbitrary",)),
        name="ret_prompt",
    )(proj, proj, proj, proj, cos, sin, dec, qd, kd, cd)


DEC_BB = 8


def _ret_decode_kernel(q_ref, k_ref, v_ref, g_ref, cos_ref, sin_ref, gam_ref, s_ref, *rest, li, fill):
    o_ref, so_ref = rest[-2:]
    so_ref = _layer_view(so_ref, li, fill)
    cos = cos_ref[...]
    sin = sin_ref[...]
    for h in range(RET_HEADS):
        cols = slice(h * 128, (h + 1) * 128)
        qb = _bf(_rope128(q_ref[:, cols], cos, sin))
        kb = _bf(_rope128(k_ref[:, cols], cos, sin) * (RET_DK ** -0.5))
        vb = _bf(v_ref[:, cols])
        gam = gam_ref[h]
        qk = jnp.sum(qb.astype(F32) * kb.astype(F32), axis=-1, keepdims=True)
        o = _bf(qk).astype(F32) * vb.astype(F32)
        rows = []
        for b in range(DEC_BB):
            s = s_ref[b, h]
            rows.append(_dot(qb, _bf(s))[b:b + 1])
            so_ref[b, h] = s * gam + _dot_tn(_row_only(kb, b), vb)
        o = o + jnp.concatenate(rows, axis=0) * gam
        r = o * lax.rsqrt(jnp.mean(o * o, axis=-1, keepdims=True) + EPS)
        o_ref[:, cols] = r * _silu(g_ref[:, cols])


def _ret_decode(proj, cos, sin, state, li, prev):
    db = proj.shape[0]
    gam = jnp.asarray(np.broadcast_to(_ret_gamma()[:, None, None], (RET_HEADS, 1, LANES)).copy(), F32)
    extra, extra_specs, alias = _layer_out(prev)
    depth = state.shape[0]
    blk = (DEC_BB, RET_HEADS, RET_DK, RET_DK)
    st_spec = pl.BlockSpec((None,) + blk, lambda i: (li, i, 0, 0, 0))
    return pl.pallas_call(
        functools.partial(_ret_decode_kernel, li=li, fill=depth if prev is None else 0),
        grid=(db // DEC_BB,),
        in_specs=[
            pl.BlockSpec((DEC_BB, 512), lambda i: (i, 0)),
            pl.BlockSpec((DEC_BB, 512), lambda i: (i, 1)),
            pl.BlockSpec((DEC_BB, 512), lambda i: (i, 2)),
            pl.BlockSpec((DEC_BB, 512), lambda i: (i, 3)),
            pl.BlockSpec((DEC_BB, LANES), lambda i: (i, 0)),
            pl.BlockSpec((DEC_BB, LANES), lambda i: (i, 0)),
            pl.BlockSpec((RET_HEADS, 1, LANES), lambda i: (0, 0, 0)),
            st_spec,
        ] + extra_specs,
        out_specs=[pl.BlockSpec((DEC_BB, 512), lambda i: (i, 0)),
                   _layer_spec(prev, depth, li, blk, lambda i: (i, 0, 0, 0))],
        out_shape=[
            jax.ShapeDtypeStruct((db, RET_WIDTH), F32),
            jax.ShapeDtypeStruct(state.shape, F32),
        ],
        input_output_aliases={} if alias is None else {8: 1},
        compiler_params=_cparams(("arbitrary",)),
        name="ret_decode",
    )(proj, proj, proj, proj, cos, sin, gam, state, *extra)


def _head_expand_mat():
    r = lax.broadcasted_iota(jnp.int32, (LANES, SSD_WIDTH), 0)
    c = lax.broadcasted_iota(jnp.int32, (LANES, SSD_WIDTH), 1)
    return jnp.where(r == (c >> 6), 1.0, 0.0).astype(BF16)


def _ssd_prompt_kernel(z_ref, xbc_ref, dt_ref, cw_ref, cb_ref, dtb_ref, alog_ref, dexp_ref, nw_ref,
                       y_ref, ho_ref, co_ref, ext_ref, ht_ref):
    i = pl.program_id(0)

    @pl.when(i == 0)
    def _():
        ext_ref[0:8, :] = jnp.zeros((8, SSD_CONV_DIM), F32)
        ht_ref[...] = jnp.zeros(ht_ref.shape, F32)

    u = xbc_ref[...]
    ext_ref[8:8 + CHUNK, :] = u
    cw = cw_ref[...]
    conv = (cb_ref[...] + cw[3:4, :] * u + cw[2:3, :] * ext_ref[7:7 + CHUNK, :]
            + cw[1:2, :] * ext_ref[6:6 + CHUNK, :] + cw[0:1, :] * ext_ref[5:5 + CHUNK, :])
    ext_ref[0:8, :] = u[CHUNK - 8:CHUNK, :]
    xbc = _silu(conv)
    xs = xbc[:, 0:SSD_WIDTH]

    dt = _softplus(dt_ref[...] + dtb_ref[...])
    a = dt * (-jnp.exp(alog_ref[...]))
    ri = lax.broadcasted_iota(jnp.int32, (CHUNK, CHUNK), 0)
    ci = lax.broadcasted_iota(jnp.int32, (CHUNK, CHUNK), 1)
    causal = ri >= ci
    tri = jnp.where(causal, 1.0, 0.0).astype(BF16)
    cum = _dot_tri(tri, a)
    cum_t = cum.T
    dt_t = dt.T
    cum_last = cum[CHUNK - 1:CHUNK, :]
    eh = _head_expand_mat()
    ecum_x = _dot_split(jnp.exp(cum), eh)
    wgt_x = _dot_split(jnp.exp(cum_last - cum) * dt, eh)
    elast_x = _dot_split(jnp.broadcast_to(jnp.exp(cum_last), (8, LANES)), eh)[0:1, :]

    lane = lax.broadcasted_iota(jnp.int32, (CHUNK, LANES), 1)
    lo_half = lane < 64
    xw = _bf(xs * wgt_x)
    y_parts = []
    ch_parts = []
    for g in range(SSD_GROUPS):
        bg = xbc[:, SSD_WIDTH + g * 128:SSD_WIDTH + (g + 1) * 128]
        cg = xbc[:, SSD_WIDTH + 256 + g * 128:SSD_WIDTH + 256 + (g + 1) * 128]
        cgb = _bf(cg)
        cb = _dot_nt(cgb, _bf(bg))
        ht = ht_ref[g]
        ch_parts.append(_dot(cgb, _bf(ht)))
        for k in range(4):
            h0 = g * 8 + 2 * k
            xp = _bf(xs[:, h0 * 64:(h0 + 2) * 64])
            ys = []
            for hh in (h0, h0 + 1):
                seg = cum[:, hh:hh + 1] - cum_t[hh:hh + 1, :]
                lm = jnp.where(causal, jnp.exp(jnp.minimum(seg, 0.0)), 0.0)
                sc = cb * lm * dt_t[hh:hh + 1, :]
                ys.append(_dot(_bf(sc), xp))
            y_parts.append(jnp.where(lo_half, ys[0], ys[1]))
        bgt = _bf(bg.T)
        ht_ref[g] = ht * elast_x[:, g * 512:(g + 1) * 512] + _dot(bgt, xw[:, g * 512:(g + 1) * 512])
    y = jnp.concatenate(y_parts, axis=1) + jnp.concatenate(ch_parts, axis=1) * ecum_x + dexp_ref[...] * xs
    gated = y * _silu(z_ref[...])
    y_ref[...] = gated * lax.rsqrt(jnp.mean(gated * gated, axis=-1, keepdims=True) + EPS) * nw_ref[...]

    @pl.when(i == pl.num_programs(0) - 1)
    def _():
        co_ref[...] = u[CHUNK - 8:CHUNK, :]
        for g in range(SSD_GROUPS):
            htf = ht_ref[g]
            for k in range(4):
                h0 = g * 8 + 2 * k
                ho_ref[h0:h0 + 2] = htf[:, k * 128:(k + 1) * 128].T.reshape(2, SSD_HEADDIM, SSD_STATE)


def _ssd_prompt(proj, conv_w, conv_b, dt_bias, a_log, d, norm_w):
    l = proj.shape[0]
    full2 = lambda shape: pl.BlockSpec(shape, lambda i: (0, 0))
    return pl.pallas_call(
        _ssd_prompt_kernel,
        grid=(l // CHUNK,),
        in_specs=[
            pl.BlockSpec((CHUNK, SSD_WIDTH), lambda i: (i, C_SZ // SSD_WIDTH)),
            pl.BlockSpec((CHUNK, SSD_CONV_DIM), lambda i: (i, C_XBC // SSD_CONV_DIM)),
            pl.BlockSpec((CHUNK, LANES), lambda i: (i, C_SDT // LANES)),
            full2((SSD_CONV, SSD_CONV_DIM)),
            full2((1, SSD_CONV_DIM)),
            full2((1, LANES)),
            full2((1, LANES)),
            full2((1, SSD_WIDTH)),
            full2((1, SSD_WIDTH)),
        ],
        out_specs=[
            pl.BlockSpec((CHUNK, SSD_WIDTH), lambda i: (i, 0)),
            pl.BlockSpec((SSD_HEADS, SSD_HEADDIM, SSD_STATE), lambda i: (0, 0, 0)),
            full2((8, SSD_CONV_DIM)),
        ],
        out_shape=[
            jax.ShapeDtypeStruct((l, SSD_WIDTH), F32),
            jax.ShapeDtypeStruct((SSD_HEADS, SSD_HEADDIM, SSD_STATE), F32),
            jax.ShapeDtypeStruct((8, SSD_CONV_DIM), F32),
        ],
        scratch_shapes=[
            pltpu.VMEM((8 + CHUNK, SSD_CONV_DIM), F32),
            pltpu.VMEM((SSD_GROUPS, SSD_STATE, 512), F32),
        ],
        compiler_params=_cparams(("arbitrary",)),
        name="ssd_prompt",
    )(proj, proj, proj, conv_w, conv_b.reshape(1, -1), _pad_lanes(dt_bias), _pad_lanes(a_log),
      jnp.repeat(d, SSD_HEADDIM).reshape(1, -1), norm_w.reshape(1, -1))


def _ssd_decode_kernel(z_ref, xbc_ref, dt_ref, cs_ref, h_ref, cw_ref, cb_ref, dtb_ref, alog_ref, dexp_ref, nw_ref,
                       *rest, li, fill):
    y_ref, co_ref, ho_ref = rest[-3:]
    ho_ref = _layer_view(ho_ref, li, fill)
    u = xbc_ref[...]
    c0 = cs_ref[0]
    c1 = cs_ref[1]
    c2 = cs_ref[2]
    cw = cw_ref[...]
    conv = cb_ref[...] + cw[3:4, :] * u + cw[2:3, :] * c2 + cw[1:2, :] * c1 + cw[0:1, :] * c0
    co_ref[0] = c1
    co_ref[1] = c2
    co_ref[2] = u
    xbc = _silu(conv)
    xs = xbc[:, 0:SSD_WIDTH]
    dt = _softplus(dt_ref[...] + dtb_ref[...])
    ea = jnp.exp(dt * (-jnp.exp(alog_ref[...])))
    eh = _head_expand_mat()
    dt_x = _dot_split(dt, eh)
    ea_x = _dot_split(ea, eh)
    dtx = dt_x * xs
    ones = jnp.ones((DEC_BB, LANES), BF16)
    ych = [[None, None] for _ in range(DEC_BB)]
    cbs = []
    for g in range(SSD_GROUPS):
        gc = slice(g * 512, (g + 1) * 512)
        bg = xbc[:, SSD_WIDTH + g * 128:SSD_WIDTH + (g + 1) * 128]
        cg = xbc[:, SSD_WIDTH + 256 + g * 128:SSD_WIDTH + 256 + (g + 1) * 128]
        cgb = _bf(cg)
        cbs.append(jnp.sum(cgb.astype(F32) * _bf(bg).astype(F32), axis=-1, keepdims=True))
        b_hi, b_lo = _split2(bg)
        for b in range(DEC_BB):
            hs = h_ref[b, g * 8:(g + 1) * 8].reshape(512, SSD_STATE)
            ych[b][g] = _dot_nt(cgb, _bf(hs))[b:b + 1]
            e_hi, e_lo = _split2(_row_only(ea_x[:, gc], b))
            decay = _dot_tn(e_hi, ones) + _dot_tn(e_lo, ones)
            x_hi, x_lo = _split2(_row_only(dtx[:, gc], b))
            upd = _dot_tn(x_hi, b_hi) + _dot_tn(x_hi, b_lo) + _dot_tn(x_lo, b_hi)
            ho_ref[b, g * 8:(g + 1) * 8] = (hs * decay + upd).reshape(8, SSD_HEADDIM, SSD_STATE)
    ych = jnp.concatenate([jnp.concatenate(r, axis=1) for r in ych], axis=0)
    lane = lax.broadcasted_iota(jnp.int32, (DEC_BB, SSD_WIDTH), 1)
    cbx = jnp.where(lane < 512, cbs[0], cbs[1])
    y = dt_x * cbx * xs + ych * ea_x + dexp_ref[...] * xs
    gated = y * _silu(z_ref[...])
    y_ref[...] = gated * lax.rsqrt(jnp.mean(gated * gated, axis=-1, keepdims=True) + EPS) * nw_ref[...]


def _ssd_decode(proj, conv_state_t, ssm_state, li, prev, conv_w, conv_b, dt_bias, a_log, d, norm_w):
    db = proj.shape[0]
    full2 = lambda shape: pl.BlockSpec(shape, lambda i: (0, 0))
    extra, extra_specs, alias = _layer_out(prev)
    depth = ssm_state.shape[0]
    blk = (DEC_BB, SSD_HEADS, SSD_HEADDIM, SSD_STATE)
    st_spec = pl.BlockSpec((None,) + blk, lambda i: (li, i, 0, 0, 0))
    return pl.pallas_call(
        functools.partial(_ssd_decode_kernel, li=li, fill=depth if prev is None else 0),
        grid=(db // DEC_BB,),
        in_specs=[
            pl.BlockSpec((DEC_BB, SSD_WIDTH), lambda i: (i, C_SZ // SSD_WIDTH)),
            pl.BlockSpec((DEC_BB, SSD_CONV_DIM), lambda i: (i, C_XBC // SSD_CONV_DIM)),
            pl.BlockSpec((DEC_BB, LANES), lambda i: (i, C_SDT // LANES)),
            pl.BlockSpec((None, SSD_CONV - 1, DEC_BB, SSD_CONV_DIM), lambda i: (li, 0, i, 0)),
            st_spec,
            full2((SSD_CONV, SSD_CONV_DIM)),
            full2((1, SSD_CONV_DIM)),
            full2((1, LANES)),
            full2((1, LANES)),
            full2((1, SSD_WIDTH)),
            full2((1, SSD_WIDTH)),
        ] + extra_specs,
        out_specs=[
            pl.BlockSpec((DEC_BB, SSD_WIDTH), lambda i: (i, 0)),
            pl.BlockSpec((SSD_CONV - 1, DEC_BB, SSD_CONV_DIM), lambda i: (0, i, 0)),
            _layer_spec(prev, depth, li, blk, lambda i: (i, 0, 0, 0)),
        ],
        out_shape=[
            jax.ShapeDtypeStruct((db, SSD_WIDTH), F32),
            jax.ShapeDtypeStruct((SSD_CONV - 1, db, SSD_CONV_DIM), F32),
            jax.ShapeDtypeStruct(ssm_state.shape, F32),
        ],
        input_output_aliases={} if alias is None else {11: 2},
        compiler_params=_cparams(("arbitrary",)),
        name="ssd_decode",
    )(proj, proj, proj, conv_state_t, ssm_state, conv_w, conv_b.reshape(1, -1), _pad_lanes(dt_bias),
      _pad_lanes(a_log), jnp.repeat(d, SSD_HEADDIM).reshape(1, -1), norm_w.reshape(1, -1), *extra)


def _value_variants(v):
    lane = lax.broadcasted_iota(jnp.int32, v.shape, 1)
    lo = lane < 64
    sw = pltpu.roll(v, 64, 1)
    one = jnp.ones_like(v)
    return [jnp.where(lo, v, one), jnp.where(lo, one, sw), jnp.where(lo, sw, one), jnp.where(lo, one, v)]


def _nsa_prep_kernel(nq_ref, nkv_ref, cos_ref, sin_ref, qn_ref, ksn_ref, kwn_ref, *rest, li, fill):
    qb_ref, cache_ref, win_ref, cache_t_ref, win_t_ref, kb_ref, vt_ref = rest[-7:]
    cache_t_ref = _layer_view(cache_t_ref, li, fill)
    cos = cos_ref[...]
    sin = sin_ref[...]
    bd = _half_mean_mat()
    qn = qn_ref[...]
    for h in range(NSA_HEADS):
        cols = slice(h * 128, (h + 1) * 128)
        x = nq_ref[:, cols]
        ms = jnp.sum(x * x, axis=-1, keepdims=True) * (1.0 / NSA_HD)
        qh = _rope64(x * lax.rsqrt(ms + EPS) * qn, cos, sin)
        qb_ref[:, cols] = _bf(qh * (NSA_HD ** -0.5))
    kc = _rope64(nkv_ref[:, 0:128], cos, sin)
    vc = nkv_ref[:, 128:256]
    ks = _rope64(_rms64(nkv_ref[:, 256:384], ksn_ref[...], bd), cos, sin)
    vs = nkv_ref[:, 384:512]
    kw = _rope64(_rms64(nkv_ref[:, 512:640], kwn_ref[...], bd), cos, sin)
    vw = nkv_ref[:, 640:768]
    for i, v in enumerate((kc, vc, ks, vs)):
        cache_ref[:, i * 128:(i + 1) * 128] = v
        cache_t_ref[i] = v.T
    for i, v in enumerate((kw, vw)):
        win_ref[:, i * 128:(i + 1) * 128] = v
        win_t_ref[i] = v.T
    kb_ref[0] = _bf(ks)
    kb_ref[1] = _bf(kw)
    for i, v in enumerate(_value_variants(vs) + _value_variants(vw)):
        vt_ref[i] = _bf(v.T)


def _nsa_prep(proj, cos, sin, li, depth, prev, qn, ksn, kwn):
    m = proj.shape[0]
    t = min(m, 256)
    extra, extra_specs, alias = _layer_out(prev)
    two = lambda w: jnp.concatenate([w, w]).reshape(1, LANES)
    full2 = lambda shape: pl.BlockSpec(shape, lambda i: (0, 0))
    return pl.pallas_call(
        functools.partial(_nsa_prep_kernel, li=li, fill=depth if prev is None else 0),
        grid=(m // t,),
        in_specs=[
            pl.BlockSpec((t, 1024), lambda i: (i, C_NQ // 1024)),
            pl.BlockSpec((t, 768), lambda i: (i, C_NKV // 768)),
            pl.BlockSpec((t, LANES), lambda i: (i, 0)),
            pl.BlockSpec((t, LANES), lambda i: (i, 0)),
            full2((1, LANES)), full2((1, LANES)), full2((1, LANES)),
        ] + extra_specs,
        out_specs=[
            pl.BlockSpec((t, 1024), lambda i: (i, 0)),
            pl.BlockSpec((t, 512), lambda i: (i, 0)),
            pl.BlockSpec((t, 256), lambda i: (i, 0)),
            _layer_spec(prev, depth, li, (4, LANES, t), lambda i: (0, 0, i)),
            pl.BlockSpec((2, LANES, t), lambda i: (0, 0, i)),
            pl.BlockSpec((2, t, LANES), lambda i: (0, i, 0)),
            pl.BlockSpec((8, LANES, t), lambda i: (0, 0, i)),
        ],
        out_shape=[
            jax.ShapeDtypeStruct((m, 1024), BF16),
            jax.ShapeDtypeStruct((m, 512), F32),
            jax.ShapeDtypeStruct((m, 256), F32),
            jax.ShapeDtypeStruct((depth, 4, LANES, m), F32),
            jax.ShapeDtypeStruct((2, LANES, m), F32),
            jax.ShapeDtypeStruct((2, m, LANES), BF16),
            jax.ShapeDtypeStruct((8, LANES, m), BF16),
        ],
        input_output_aliases={} if alias is None else {7: 3},
        compiler_params=_cparams(("arbitrary",)),
        name="nsa_prep",
    )(proj, proj, cos, sin, two(qn), two(ksn), two(kwn), *extra)


def _compress_kernel(rk_ref, rv_ref, w1_ref, pe_ref, w2_ref, kn_ref, kc_ref, vc_ref):
    tr = rk_ref.shape[0]
    bd = _half_mean_mat()

    def mlp(r, j):
        ha = _dot(_bf(r + pe_ref[2 * j:2 * j + 1, :]), w1_ref[2 * j])
        hb = _dot(_bf(r + pe_ref[2 * j + 1:2 * j + 2, :]), w1_ref[2 * j + 1])
        hid = ha + pltpu.roll(hb, tr - 1, 0)
        return _dot(_bf(_silu(hid)), w2_ref[j])

    kc = _rms64(mlp(rk_ref[...], 0), kn_ref[...], bd)
    vc = mlp(rv_ref[...], 1)
    kc_ref[...] = _bf(kc)
    vct = vc.T
    vc_ref[0] = _bf(vct)
    vc_ref[1] = _bf(pltpu.roll(vct, 64, 0))


def _compress_weights(pe, w1, w2):
    w1r = w1.reshape(2, 2, 16, 64, 64)
    z = jnp.zeros_like(w1r)
    top = jnp.concatenate([w1r, z], axis=-1)
    bot = jnp.concatenate([z, w1r], axis=-1)
    w1x = jnp.stack([top, bot], axis=3)
    w1x = w1x.reshape(4, 2048, LANES).astype(BF16)
    per = pe.reshape(2, 2, 16, 1, 64)
    pex = jnp.broadcast_to(per, (2, 2, 16, 2, 64)).reshape(4, 2048)
    z2 = jnp.zeros_like(w2)
    w2x = jnp.concatenate([jnp.concatenate([w2, z2], -1), jnp.concatenate([z2, w2], -1)], axis=1).astype(BF16)
    return w1x, pex, w2x


def _compress(rk, rv, w1x, pex, w2x, kn):
    nr = rk.shape[0]
    tr = min(nr, 512)
    two = jnp.concatenate([kn, kn]).reshape(1, LANES)
    return pl.pallas_call(
        _compress_kernel,
        grid=(nr // tr,),
        in_specs=[
            pl.BlockSpec((tr, 2048), lambda i: (i, 0)),
            pl.BlockSpec((tr, 2048), lambda i: (i, 0)),
            pl.BlockSpec((4, 2048, LANES), lambda i: (0, 0, 0)),
            pl.BlockSpec((4, 2048), lambda i: (0, 0)),
            pl.BlockSpec((2, LANES, LANES), lambda i: (0, 0, 0)),
            pl.BlockSpec((1, LANES), lambda i: (0, 0)),
        ],
        out_specs=[
            pl.BlockSpec((tr, LANES), lambda i: (i, 0)),
            pl.BlockSpec((2, LANES, tr), lambda i: (0, 0, i)),
        ],
        out_shape=[
            jax.ShapeDtypeStruct((nr, LANES), BF16),
            jax.ShapeDtypeStruct((2, LANES, nr), BF16),
        ],
        compiler_params=_cparams(("arbitrary",)),
        name="nsa_compress",
    )(rk, rv, w1x, pex, w2x, two)


QB = 128
SEL_TK = 512
WIN_TK = 128
SEL_GROUP = 4
RANK_STEP = 32


def _nsa_prompt_kernel(q_ref, kc_ref, vct_ref, kb_ref, vt_ref, ovt_ref, gate_ref, eg_ref, nz_ref, o_ref,
                       m_ref, acc_ref, st_ref, sb_ref):
    g = pl.program_id(0)
    qi = pl.program_id(1)
    t0 = qi * QB
    nc = kc_ref.shape[0]
    q = q_ref[...]
    qs = jnp.concatenate([q[:, 0:128], q[:, 256:384], q[:, 128:256], q[:, 384:512]], axis=0)
    tcol = t0 + (lax.broadcasted_iota(jnp.int32, (1, 4 * QB), 1) & (QB - 1))
    tq = t0 + lax.broadcasted_iota(jnp.int32, (1, QB), 1)

    def pv(vte, vto, pb):
        return jnp.concatenate([_dot(vte, pb[:, 0:2 * QB]), _dot(vto, pb[:, 2 * QB:4 * QB])], axis=1)

    sc = _dot_nt(kc_ref[...], qs)
    n_io = lax.broadcasted_iota(jnp.int32, (nc, 1), 0)
    mask_c = (n_io * CMP_STRIDE + (CMP_BLOCK - 1)) <= tcol
    sc = jnp.where(mask_c, sc, NEG)
    mc = jnp.max(sc, axis=0, keepdims=True)
    pc = jnp.where(mask_c, jnp.exp(sc - mc), 0.0)
    zc = jnp.sum(pc, axis=0, keepdims=True)
    pc = pc * (1.0 / jnp.where(zc > 0, zc, 1.0))
    o_c = pv(vct_ref[g], vct_ref[1 - g], _bf(pc))

    p4 = pc[:, 0:QB] + pc[:, QB:2 * QB] + pc[:, 2 * QB:3 * QB] + pc[:, 3 * QB:4 * QB]
    imp = _dot_tri(ovt_ref[...], p4)
    jb = lax.broadcasted_iota(jnp.int32, (LANES, QB), 0)
    cur = tq >> 6
    forced = (jb == 0) | ((jb <= cur) & (jb > cur - N_LOCAL))
    valid = jb <= cur
    score = jnp.where(forced, FORCE_SCORE, imp)
    score = jnp.where(valid, score, -FORCE_SCORE)
    st_ref[...] = score
    n_blk = ((t0 + QB - 1) >> 6) + 1
    sb_ref[...] = jnp.full(sb_ref.shape, NEG, F32)

    def rank_rows(nrows):
        sc_n = score[0:nrows]
        jb_n = jb[0:nrows]

        def rank_body(i, rank):
            row = st_ref[pl.ds(i, 1), :]
            beats = (row > sc_n) | ((row == sc_n) & (i < jb_n))
            return rank + jnp.where(beats, 1.0, 0.0)

        rank = lax.fori_loop(0, n_blk, rank_body, jnp.zeros((nrows, QB), F32))
        sb_ref[0:nrows, :] = jnp.where((rank < SLC_TOPN) & valid[0:nrows], 0.0, NEG)

    for nrows in range(RANK_STEP, LANES + 1, RANK_STEP):
        pl.when((n_blk > nrows - RANK_STEP) & (n_blk <= nrows))(functools.partial(rank_rows, nrows))

    def reset():
        m_ref[...] = jnp.full(m_ref.shape, NEG, F32)
        acc_ref[...] = jnp.zeros(acc_ref.shape, F32)

    def update(kidx, ve, vo, tk, tiles):
        m_old = m_ref[...]
        m_new = m_old
        ss = []
        for k0, bias, causal in tiles:
            s = _dot_nt(kb_ref[kidx, pl.ds(k0, tk), :], qs)
            if bias is not None:
                s = s + bias
            if causal:
                kpos = k0 + lax.broadcasted_iota(jnp.int32, (tk, 1), 0)
                s = jnp.where(kpos <= tcol, s, NEG)
            m_new = jnp.maximum(m_new, jnp.max(s, axis=0, keepdims=True))
            ss.append(s)
        acc = jnp.exp(m_old - m_new) * acc_ref[...]
        for (k0, _, _), s in zip(tiles, ss):
            acc = acc + pv(vt_ref[ve, :, pl.ds(k0, tk)], vt_ref[vo, :, pl.ds(k0, tk)], _bf(jnp.exp(s - m_new)))
        acc_ref[...] = acc
        m_ref[...] = m_new

    def result():
        a = acc_ref[...]
        den = pltpu.roll(a, 64, 0)
        return a / jnp.where(den > 0, den, 1.0)

    def sel_bias(k0):
        rows = sb_ref[pl.ds(pl.multiple_of(k0 // SLC_BLOCK, 8), SEL_TK // SLC_BLOCK), :]
        b = jnp.concatenate([jnp.broadcast_to(rows[r:r + 1, :], (SLC_BLOCK, QB))
                             for r in range(SEL_TK // SLC_BLOCK)], axis=0)
        return jnp.concatenate([b, b, b, b], axis=1)

    def sel_tile(k0, causal):
        return (k0, sel_bias(k0), causal)

    reset()
    vse = 2 * g
    n_full = t0 // SEL_TK

    def sel_body(kp, c):
        k0 = pl.multiple_of(kp * SEL_GROUP * SEL_TK, SEL_GROUP * SEL_TK)
        update(0, vse, vse + 1, SEL_TK, [sel_tile(k0 + i * SEL_TK, False) for i in range(SEL_GROUP)])
        return c

    lax.fori_loop(0, n_full // SEL_GROUP, sel_body, 0)
    k_diag = pl.multiple_of(n_full * SEL_TK, SEL_TK)

    for rem in range(SEL_GROUP):
        @pl.when(n_full % SEL_GROUP == rem)
        def _(rem=rem):
            update(0, vse, vse + 1, SEL_TK,
                   [sel_tile(k_diag - (rem - i) * SEL_TK, False) for i in range(rem)] + [sel_tile(k_diag, True)])

    o_s = result()

    reset()
    vwe = 4 + 2 * g
    n_old = WINDOW // WIN_TK

    def win_tile(kt, bound, causal):
        k0 = pl.multiple_of(kt * WIN_TK, WIN_TK)
        bias = None
        if bound:
            kpos = k0 + lax.broadcasted_iota(jnp.int32, (WIN_TK, 1), 0)
            bias = jnp.where(tcol - kpos <= WINDOW, 0.0, NEG)
        return (k0, bias, causal)

    @pl.when(qi >= n_old)
    def _():
        update(1, vwe, vwe + 1, WIN_TK, [win_tile(qi - n_old, True, False)]
               + [win_tile(qi - n_old + i, False, False) for i in range(1, n_old)] + [win_tile(qi, False, True)])

    @pl.when(qi < n_old)
    def _():
        def win_body(kt, c):
            update(1, vwe, vwe + 1, WIN_TK, [win_tile(kt, False, False)])
            return c

        lax.fori_loop(0, qi, win_body, 0)
        update(1, vwe, vwe + 1, WIN_TK, [win_tile(qi, False, True)])

    o_w = result()

    gx = _dot_split(jax.nn.sigmoid(gate_ref[...]), eg_ref[0])
    lo_rows = lax.broadcasted_iota(jnp.int32, (LANES, QB), 0) < 64
    for k in range(2):
        ca = slice(k * QB, (k + 1) * QB)
        cb = slice(2 * QB + k * QB, 2 * QB + (k + 1) * QB)
        cols = slice(k * 128, (k + 1) * 128)
        tile = lambda o: jnp.where(lo_rows, o[:, ca], o[:, cb]).T
        o = (gx[:, k * 128:(k + 1) * 128] * tile(o_c)
             + gx[:, 256 + k * 128:256 + (k + 1) * 128] * tile(o_s)
             + gx[:, 512 + k * 128:512 + (k + 1) * 128] * tile(o_w))
        o_ref[:, cols] = o * _silu(nz_ref[:, cols])


def _overlap_mat(nc):
    n = np.arange(nc)[:, None]
    j = np.arange(LANES)[None, :]
    ov = ((n * CMP_STRIDE < (j + 1) * SLC_BLOCK) & (n * CMP_STRIDE + CMP_BLOCK - 1 >= j * SLC_BLOCK))
    return ov.astype(np.float32)


def _gate_expand_mat():
    eg = np.zeros((NSA_KV_HEADS, LANES, 3 * 256), np.float32)
    for g in range(NSA_KV_HEADS):
        for hh in range(4):
            for c in range(3):
                eg[g, (g * 4 + hh) * 3 + c, c * 256 + hh * 64:c * 256 + (hh + 1) * 64] = 1.0
    return jnp.asarray(eg, BF16)


def _nsa_prompt(qb, kc, vct, kb, vt, proj):
    l = qb.shape[0]
    nc = kc.shape[0]
    assert l // SLC_BLOCK <= LANES and l % SEL_TK == 0
    ovt = jnp.asarray(_overlap_mat(nc).T, BF16)
    return pl.pallas_call(
        _nsa_prompt_kernel,
        grid=(NSA_KV_HEADS, l // QB),
        in_specs=[
            pl.BlockSpec((QB, 512), lambda g, i: (i, g)),
            pl.BlockSpec((nc, LANES), lambda g, i: (0, 0)),
            pl.BlockSpec((2, LANES, nc), lambda g, i: (0, 0, 0)),
            pl.BlockSpec((2, l, LANES), lambda g, i: (0, 0, 0)),
            pl.BlockSpec((8, LANES, l), lambda g, i: (0, 0, 0)),
            pl.BlockSpec((LANES, nc), lambda g, i: (0, 0)),
            pl.BlockSpec((QB, LANES), lambda g, i: (i, C_GATE // LANES)),
            pl.BlockSpec((1, LANES, 768), lambda g, i: (g, 0, 0)),
            pl.BlockSpec((QB, 256), lambda g, i: (i, C_NZ // 256 + g)),
        ],
        out_specs=pl.BlockSpec((QB, 256), lambda g, i: (i, g)),
        out_shape=jax.ShapeDtypeStruct((l, NSA_WIDTH), F32),
        scratch_shapes=[
            pltpu.VMEM((1, 4 * QB), F32),
            pltpu.VMEM((LANES, 4 * QB), F32),
            pltpu.VMEM((LANES, QB), F32),
            pltpu.VMEM((LANES, QB), F32),
        ],
        compiler_params=_cparams(("arbitrary", "arbitrary")),
        name="nsa_prompt",
    )(qb, kc, vct, kb, vt, ovt, proj, _gate_expand_mat(), proj)


def _softmax_with_new(s, valid, s_new):
    s = jnp.where(valid, s, NEG)
    m = jnp.maximum(jnp.max(s, axis=1, keepdims=True), s_new)
    p = jnp.where(valid, jnp.exp(s - m), 0.0)
    pn = jnp.exp(s_new - m)
    z = jnp.sum(p, axis=1, keepdims=True) + pn
    return p / z, pn / z


DEC_ROWS = 2


def _nsa_decode_kernel(pt_ref, q_ref, cache_hbm, crow_ref, win_ref, wrow_ref, wcol_ref, gate_ref, nz_ref,
                       w1_ref, pe_ref, w2_ref, kn_ref, ov_ref, ek_ref, *rest, qpos, li, fill):
    o_ref, wo_ref, buf, tok_ref, sem = rest[-5:]
    wo_ref = _layer_view(wo_ref, li, fill)
    step = pl.program_id(0)
    slot = step % 2
    n_pages = buf.shape[2]
    page = buf.shape[5]
    t = n_pages * page
    nr = t // CMP_STRIDE
    wb = win_ref.shape[-1]
    n_cmp = (t + 1 - CMP_BLOCK) // CMP_STRIDE + 1

    def page_copies(st, s):
        return [pltpu.make_async_copy(cache_hbm.at[li, pt_ref[st * DEC_ROWS + r, p]], buf.at[s, r, p], sem.at[s])
                for r in range(DEC_ROWS) for p in range(n_pages)]

    @pl.when(step == 0)
    def _():
        for c in page_copies(0, 0):
            c.start()

    @pl.when(step + 1 < pl.num_programs(0))
    def _():
        for c in page_copies(step + 1, 1 - slot):
            c.start()

    for c in page_copies(step, slot):
        c.wait()

    bd = _half_mean_mat()
    r8 = lax.broadcasted_iota(jnp.int32, (8, 8), 0)
    c8 = lax.broadcasted_iota(jnp.int32, (8, 8), 1)
    gsum = jnp.where((r8 >> 2) == (c8 >> 2), 1.0, 0.0).astype(BF16)
    j_io = lax.broadcasted_iota(jnp.int32, (1, LANES), 1)
    cur = qpos // SLC_BLOCK
    forced = (j_io == 0) | ((j_io <= cur) & (j_io > cur - N_LOCAL))
    valid = j_io <= cur
    n_io = lax.broadcasted_iota(jnp.int32, (1, nr), 1)
    mask_c = ((n_io * CMP_STRIDE + (CMP_BLOCK - 1)) <= qpos) & (n_io < n_cmp)
    i_io = lax.broadcasted_iota(jnp.int32, (1, wb), 1)
    valid_w = ((wb - i_io) <= WINDOW) & ((qpos - wb + i_io) >= 0)
    lane8 = lax.broadcasted_iota(jnp.int32, (8, LANES), 1)
    row8 = lax.broadcasted_iota(jnp.int32, (8, LANES), 0)
    lo = lax.broadcasted_iota(jnp.int32, (1, LANES), 1) < 64
    last = lax.broadcasted_iota(jnp.int32, (LANES, wb), 1) == wb - 1
    ncol = wcol_ref.shape[2]

    def one_row(r):
        q8 = q_ref[r]
        q8f = q8.astype(F32)

        for p in range(n_pages):
            for j in range(2):
                tok_ref[r, j, p * page:(p + 1) * page, :] = buf[slot, r, p, j].T

        def mlp(j):
            acc_a = jnp.zeros((nr, LANES), F32)
            acc_b = jnp.zeros((nr, LANES), F32)
            for l2 in range(CMP_STRIDE // 2):
                x = jnp.concatenate([tok_ref[r, j, pl.ds(2 * l2, nr, stride=CMP_STRIDE), :],
                                     tok_ref[r, j, pl.ds(2 * l2 + 1, nr, stride=CMP_STRIDE), :]], axis=1)
                cols = slice(l2 * 256, (l2 + 1) * 256)
                acc_a = acc_a + _dot(_bf(x + pe_ref[2 * j:2 * j + 1, cols]), w1_ref[2 * j, cols, :])
                acc_b = acc_b + _dot(_bf(x + pe_ref[2 * j + 1:2 * j + 2, cols]), w1_ref[2 * j + 1, cols, :])
            hid = acc_a + pltpu.roll(acc_b, nr - 1, 0)
            return _dot(_bf(_silu(hid)), w2_ref[j])

        kcc = _rms64(mlp(0), kn_ref[...], bd)
        vcc = mlp(1)

        sc = jnp.where(mask_c, _dot_nt(q8, _bf(kcc)), NEG)
        mc = jnp.max(sc, axis=1, keepdims=True)
        pc = jnp.where(mask_c, jnp.exp(sc - mc), 0.0)
        zc = jnp.sum(pc, axis=1, keepdims=True)
        pc = pc / jnp.where(zc > 0, zc, 1.0)
        o_c = _dot(_bf(pc), _bf(vcc))

        imp = _dot_split(_dot_tri(gsum, pc), ov_ref[...])
        score = jnp.where(forced, FORCE_SCORE, imp)
        score = jnp.where(valid, score, -FORCE_SCORE)
        rank = jnp.zeros((8, LANES), F32)
        for i in range(cur + 1):
            col = score[:, i:i + 1]
            rank = rank + jnp.where((col > score) | ((col == score) & (i < j_io)), 1.0, 0.0)
        sel = jnp.where((rank < SLC_TOPN) & valid, 1.0, 0.0)

        crow = crow_ref[r]
        ks_new = _bf(crow[:, 256:384]).astype(F32)
        vs_new = _bf(crow[:, 384:512]).astype(F32)
        sel_past = _dot(_bf(sel), ek_ref[...]) > 0.5
        s_s = jnp.concatenate([_dot(q8, _bf(buf[slot, r, p, 2])) for p in range(n_pages)], axis=1)
        s_new = jnp.where(sel[:, cur:cur + 1] > 0.5, jnp.sum(q8f * ks_new, axis=-1, keepdims=True), NEG)
        p_s, pn_s = _softmax_with_new(s_s, sel_past, s_new)
        p_sb = _bf(p_s)
        o_s = _bf(pn_s).astype(F32) * vs_new
        for p in range(n_pages):
            o_s = o_s + _dot_nt(p_sb[:, p * page:(p + 1) * page], _bf(buf[slot, r, p, 3]))

        wrow = wrow_ref[r]
        kw_new = _bf(wrow[:, 0:128]).astype(F32)
        vw_new = _bf(wrow[:, 128:256]).astype(F32)
        s_w = _dot(q8, _bf(win_ref[r, 0]))
        p_w, pn_w = _softmax_with_new(s_w, valid_w, jnp.sum(q8f * kw_new, axis=-1, keepdims=True))
        o_w = _dot_nt(_bf(p_w), _bf(win_ref[r, 1])) + _bf(pn_w).astype(F32) * vw_new

        sig = jnp.broadcast_to(jax.nn.sigmoid(gate_ref[r]), (8, LANES))
        gate = lambda c: jnp.sum(jnp.where(lane8 == row8 * 3 + c, sig, 0.0), axis=-1, keepdims=True)
        o8 = gate(0) * o_c + gate(1) * o_s + gate(2) * o_w

        sw = pltpu.roll(o8, 64, 1)
        flat = jnp.concatenate([
            jnp.where(lo, o8[0:1], sw[1:2]), jnp.where(lo, o8[2:3], sw[3:4]),
            jnp.where(lo, sw[4:5], o8[5:6]), jnp.where(lo, sw[6:7], o8[7:8])], axis=1)
        o_ref[r] = flat * _silu(nz_ref[r])

        own_col = lax.broadcasted_iota(jnp.int32, (LANES, ncol), 1) == ((step * DEC_ROWS + r) % ncol)
        for j in range(2):
            col = jnp.sum(jnp.where(own_col, wcol_ref[j], 0.0), axis=1, keepdims=True)
            wo_ref[r, j] = jnp.where(last, col, pltpu.roll(win_ref[r, j], wb - 1, 1))

    for r in range(DEC_ROWS):
        one_row(r)


def _nsa_decode(pt, q8, cache_t, crow, win_t, wrow, wcol, gate, nz, w1x, pex, w2x, kn, qpos, li, prev):
    db, n_pages = pt.shape
    page = cache_t.shape[-1]
    t = n_pages * page
    wb = win_t.shape[-1]
    nr = t // CMP_STRIDE
    assert nr == LANES and page == LANES and t % SLC_BLOCK == 0 and qpos // SLC_BLOCK < LANES and db % LANES == 0
    ek = (np.arange(LANES)[:, None] == (np.arange(t)[None, :] // SLC_BLOCK))
    ek = jnp.asarray(ek.astype(np.float32), BF16)
    ov = jnp.asarray(_overlap_mat(nr), BF16)
    two = jnp.concatenate([kn, kn]).reshape(1, LANES)
    rows = DEC_ROWS
    row3 = lambda w: pl.BlockSpec((rows, 1, w), lambda i, pt: (i, 0, 0))
    full = lambda a: pl.BlockSpec(a.shape, lambda i, pt: (0,) * a.ndim)
    win_blk = (rows, 2, LANES, wb)
    extra, extra_specs, alias = _layer_out(prev)
    depth = win_t.shape[0]
    return pl.pallas_call(
        functools.partial(_nsa_decode_kernel, qpos=qpos, li=li, fill=depth if prev is None else 0),
        grid_spec=pltpu.PrefetchScalarGridSpec(
            num_scalar_prefetch=1,
            grid=(db // rows,),
            in_specs=[
                pl.BlockSpec((rows, 8, LANES), lambda i, pt: (i, 0, 0)),
                pl.BlockSpec(memory_space=pl.ANY),
                row3(512),
                pl.BlockSpec((None,) + win_blk, lambda i, pt: (li, i, 0, 0, 0)),
                row3(256),
                pl.BlockSpec((2, LANES, LANES), lambda i, pt: (0, 0, (i * rows) // LANES)),
                row3(LANES), row3(512),
                full(w1x), full(pex), full(w2x), full(two), full(ov), full(ek),
            ] + extra_specs,
            out_specs=[row3(512), _layer_spec(prev, depth, li, win_blk, lambda i, pt: (i, 0, 0, 0))],
            scratch_shapes=[
                pltpu.VMEM((2, rows, n_pages, 4, LANES, page), F32),
                pltpu.VMEM((rows, 2, t, LANES), F32),
                pltpu.SemaphoreType.DMA((2,)),
            ],
        ),
        out_shape=[jax.ShapeDtypeStruct((db, 1, NSA_WIDTH), F32), jax.ShapeDtypeStruct(win_t.shape, F32)],
        input_output_aliases={} if alias is None else {15: 1},
        compiler_params=_cparams(("arbitrary",)),
        name="nsa_decode",
    )(pt, q8, cache_t, crow, win_t, wrow, wcol, gate, nz, w1x, pex, w2x, two, ov, ek, *extra)


def _mem_kv_kernel(mem_ref, nw_ref, wk_ref, wv_ref, kn_ref, kv_ref, kvb_ref):
    ml = mem_ref.shape[0]
    x = mem_ref[...]
    m = _bf(x * lax.rsqrt(jnp.mean(x * x, axis=-1, keepdims=True) + EPS) * nw_ref[...])
    k = _dot(m, wk_ref[...])
    v = _dot(m, wv_ref[...])
    for h in range(MEM_HEADS):
        cols = slice(h * 128, (h + 1) * 128)
        kh = k[:, cols]
        kh = kh * lax.rsqrt(jnp.mean(kh * kh, axis=-1, keepdims=True) + EPS) * kn_ref[...]
        kv_ref[pl.ds(h, ml, stride=2 * MEM_HEADS), :] = kh
        kv_ref[pl.ds(MEM_HEADS + h, ml, stride=2 * MEM_HEADS), :] = v[:, cols]
        kvb_ref[:, cols] = _bf(kh)
    kvb_ref[:, MEM_WIDTH:2 * MEM_WIDTH] = _bf(v)


def _mem_kv(mem, nw, wk, wv, kn):
    ml = mem.shape[0]
    return pl.pallas_call(
        _mem_kv_kernel,
        out_shape=[jax.ShapeDtypeStruct((ml * 2 * MEM_HEADS, MEM_HD), F32),
                   jax.ShapeDtypeStruct((ml, 2 * MEM_WIDTH), BF16)],
        compiler_params=pltpu.CompilerParams(vmem_limit_bytes=VMEM_LIMIT),
        name="mem_kv",
    )(mem, nw.reshape(1, -1), wk, wv, kn.reshape(1, -1))


TAIL_T = 512


def _out_proj(x_ref, ret_ref, ssd_ref, nsa_ref, wout_ref):
    return (x_ref[...] + _dot(_bf(ret_ref[...]), wout_ref[0:512, :])
            + _dot(_bf(ssd_ref[...]), wout_ref[512:1536, :])
            + _dot(_bf(nsa_ref[...]), wout_ref[1536:2048, :]))


def _cross_q(x1, ncw_ref, wq_ref, qn_ref):
    h = _bf(x1 * lax.rsqrt(jnp.mean(x1 * x1, axis=-1, keepdims=True) + EPS) * ncw_ref[...])
    q = _dot(h, wq_ref[...])
    out = []
    for hd in range(MEM_HEADS):
        qh = q[:, hd * 128:(hd + 1) * 128]
        out.append(_bf(qh * lax.rsqrt(jnp.mean(qh * qh, axis=-1, keepdims=True) + EPS) * qn_ref[...]))
    return out


def _tail_kernel(x_ref, ret_ref, ssd_ref, nsa_ref, wout_ref, ncw_ref, wq_ref, qn_ref, kvb_ref, wo_ref, y_ref):
    x1 = _out_proj(x_ref, ret_ref, ssd_ref, nsa_ref, wout_ref)
    outs = []
    for hd, qh in enumerate(_cross_q(x1, ncw_ref, wq_ref, qn_ref)):
        s = _dot_nt(qh, kvb_ref[:, hd * 128:(hd + 1) * 128]) * (MEM_HD ** -0.5)
        s = s - jnp.max(s, axis=-1, keepdims=True)
        p = jnp.exp(s)
        p = p / jnp.sum(p, axis=-1, keepdims=True)
        outs.append(_dot(_bf(p), kvb_ref[:, MEM_WIDTH + hd * 128:MEM_WIDTH + (hd + 1) * 128]))
    y_ref[...] = x1 + _dot(_bf(jnp.concatenate(outs, axis=1)), wo_ref[...])


def _tail(x, ret, ssd, nsa, wout, ncw, wq, qn, kvb, wo):
    m = x.shape[0]
    t = min(m, TAIL_T)
    ml = kvb.shape[0]
    full2 = lambda shape: pl.BlockSpec(shape, lambda i: (0, 0))
    return pl.pallas_call(
        _tail_kernel,
        grid=(m // t,),
        in_specs=[
            pl.BlockSpec((t, D_MODEL), lambda i: (i, 0)),
            pl.BlockSpec((t, RET_WIDTH), lambda i: (i, 0)),
            pl.BlockSpec((t, SSD_WIDTH), lambda i: (i, 0)),
            pl.BlockSpec((t, NSA_WIDTH), lambda i: (i, 0)),
            full2((D_MODEL, D_MODEL)),
            full2((1, D_MODEL)),
            full2((D_MODEL, MEM_WIDTH)),
            full2((1, MEM_HD)),
            full2((ml, 2 * MEM_WIDTH)),
            full2((MEM_WIDTH, D_MODEL)),
        ],
        out_specs=pl.BlockSpec((t, D_MODEL), lambda i: (i, 0)),
        out_shape=jax.ShapeDtypeStruct((m, D_MODEL), F32),
        compiler_params=_cparams(("arbitrary",)),
        name="layer_tail",
    )(x, ret, ssd, nsa, wout, ncw.reshape(1, -1), wq, qn.reshape(1, -1), kvb, wo)


def _dec_tail_a_kernel(x_ref, ret_ref, ssd_ref, nsa_ref, wout_ref, ncw_ref, wq_ref, qn_ref, x1_ref, q_ref):
    x1 = _out_proj(x_ref, ret_ref, ssd_ref, nsa_ref, wout_ref)
    x1_ref[...] = x1
    for hd, qh in enumerate(_cross_q(x1, ncw_ref, wq_ref, qn_ref)):
        q_ref[:, hd * 128:(hd + 1) * 128] = qh


def _dec_tail_a(x, ret, ssd, nsa, wout, ncw, wq, qn):
    m = x.shape[0]
    return pl.pallas_call(
        _dec_tail_a_kernel,
        out_shape=[jax.ShapeDtypeStruct((m, D_MODEL), F32), jax.ShapeDtypeStruct((m, MEM_WIDTH), BF16)],
        compiler_params=pltpu.CompilerParams(vmem_limit_bytes=VMEM_LIMIT),
        name="dec_tail_a",
    )(x, ret, ssd, nsa, wout, ncw.reshape(1, -1), wq, qn.reshape(1, -1))


def _dec_xattn_kernel(q_ref, mem_ref, x1_ref, wo_ref, y_ref):
    ml = mem_ref.shape[-2] // (2 * MEM_HEADS)
    lane = lax.broadcasted_iota(jnp.int32, (8, MEM_WIDTH), 1)
    row = lax.broadcasted_iota(jnp.int32, (8, MEM_WIDTH), 0)
    own = (lane >> 7) == row
    outs = []
    for b in range(DEC_BB):
        q4 = _bf(jnp.where(own, jnp.broadcast_to(q_ref[b:b + 1, :].astype(F32), (8, MEM_WIDTH)), 0.0))
        kb = jnp.concatenate([_bf(mem_ref[b, pl.ds(h, ml, stride=2 * MEM_HEADS), :]) for h in range(MEM_HEADS)], axis=1)
        vb = jnp.concatenate([_bf(mem_ref[b, pl.ds(MEM_HEADS + h, ml, stride=2 * MEM_HEADS), :])
                              for h in range(MEM_HEADS)], axis=1)
        s = _dot_nt(q4, kb) * (MEM_HD ** -0.5)
        s = s - jnp.max(s, axis=-1, keepdims=True)
        p = jnp.exp(s)
        p = p / jnp.sum(p, axis=-1, keepdims=True)
        o4 = _dot(_bf(p), vb)
        outs.append(jnp.sum(jnp.where(own, o4, 0.0), axis=0, keepdims=True))
    y_ref[...] = x1_ref[...] + _dot(_bf(jnp.concatenate(outs, axis=0)), wo_ref[...])


def _dec_xattn(q, mem, x1, wo, li):
    db = q.shape[0]
    rows = mem.shape[2]
    return pl.pallas_call(
        _dec_xattn_kernel,
        grid=(db // DEC_BB,),
        in_specs=[
            pl.BlockSpec((DEC_BB, MEM_WIDTH), lambda i: (i, 0)),
            pl.BlockSpec((None, DEC_BB, rows, MEM_HD), lambda i: (li, i, 0, 0)),
            pl.BlockSpec((DEC_BB, D_MODEL), lambda i: (i, 0)),
            pl.BlockSpec((MEM_WIDTH, D_MODEL), lambda i: (0, 0)),
        ],
        out_specs=pl.BlockSpec((DEC_BB, D_MODEL), lambda i: (i, 0)),
        out_shape=jax.ShapeDtypeStruct((db, D_MODEL), F32),
        compiler_params=_cparams(("arbitrary",)),
        name="dec_xattn",
    )(q, mem, x1, wo)


def _prep_w_tail(w_t, li):
    k = w_t.shape[2]
    wt = lambda a, b: w_t[li, a:b]
    nq = wt(4624, 5136).reshape(NSA_HEADS, NSA_HD, k)
    z = jnp.zeros_like(nq)
    nq_pad = jnp.concatenate([
        jnp.concatenate([nq[:4], z[:4]], axis=1),
        jnp.concatenate([z[4:], nq[4:]], axis=1)], axis=0).reshape(NSA_HEADS * LANES, k)
    padr = lambda a: jnp.pad(a, ((0, LANES - a.shape[0]), (0, 0)))
    return jnp.concatenate([
        wt(5928, 6440),
        nq_pad,
        wt(5136, 5904),
        padr(wt(4608, 4624)),
        padr(wt(5904, 5928)),
    ], axis=0)


def _rope_tables(pos, head_dim, rows):
    half = head_dim // 2
    inv = jnp.exp(-math.log(ROPE_THETA) * jnp.arange(half, dtype=F32) / half)
    ang = pos.astype(F32)[:, None] * inv[None, :]
    cos = jnp.cos(ang)
    sin = jnp.sin(ang)
    reps = LANES // head_dim
    cos_t = jnp.tile(jnp.concatenate([cos, cos], axis=-1), (1, reps))
    sin_t = jnp.tile(jnp.concatenate([-sin, sin], axis=-1), (1, reps))
    if cos_t.shape[0] != rows:
        cos_t = jnp.broadcast_to(cos_t, (rows, LANES))
        sin_t = jnp.broadcast_to(sin_t, (rows, LANES))
    return cos_t, sin_t


def kernel(x_prompt, x_sample, mem_prompt, state_ret, state_ssm, state_conv, cache_nsa_kv, cache_win_kv,
           cache_mem_kv, page_table, norm_mix, w_in, ssd_conv_w, ssd_conv_b, ssd_dt_bias, ssd_a_log, ssd_d,
           ssd_norm, nsa_q_norm, nsa_kc_norm, nsa_ks_norm, nsa_kw_norm, nsa_cmp_pe, nsa_cmp_w1, nsa_cmp_w2,
           w_out, norm_cross, norm_mem, mem_wq, mem_wk, mem_wv, mem_q_norm, mem_k_norm, mem_wo):
    b, l, _ = x_prompt.shape
    assert b == 1
    db, dl, _ = x_sample.shape
    assert dl == 1 and db % DEC_BB == 0
    depth = w_in.shape[0]
    n_pages = page_table.shape[1]
    page = cache_nsa_kv.shape[2]
    past_len = n_pages * page
    wbuf = cache_win_kv.shape[2]
    ml = cache_mem_kv.shape[2]
    wp = min(WINDOW, l)
    pos_p = jnp.arange(l, dtype=jnp.int32)
    pos_s = jnp.full((1,), past_len, dtype=jnp.int32)
    cos128, sin128 = _rope_tables(pos_p, RET_DK, l)
    cos64, sin64 = _rope_tables(pos_p, NSA_HD, l)
    cos128s, sin128s = _rope_tables(pos_s, RET_DK, db)
    cos64s, sin64s = _rope_tables(pos_s, NSA_HD, db)

    xp = x_prompt[0]
    xs = x_sample[:, 0, :]
    w_in_t = jnp.swapaxes(w_in, 1, 2)
    cache_t = jnp.transpose(cache_nsa_kv, (0, 1, 3, 4, 5, 2)).reshape(depth, -1, 4, LANES, page)
    win_t = jnp.transpose(cache_win_kv, (0, 1, 3, 4, 5, 2)).reshape(depth, db, 2, LANES, wbuf)
    conv_t = jnp.transpose(state_conv, (0, 2, 1, 3))
    mem_rows = cache_mem_kv.reshape(depth, db, ml * 2 * MEM_HEADS, MEM_HD)
    untok = lambda a: jnp.moveaxis(a.reshape(a.shape[:-2] + (NSA_KV_HEADS, NSA_HD, a.shape[-1])), -1, -4)
    ret_p, ssm_p, conv_p, win_p, mem_p, conv_s = [], [], [], [], [], []
    cache_p = cache_s = ret_s_all = ssm_s_all = win_s_all = None
    for li in range(depth):
        w_tail = _prep_w_tail(w_in_t, li)
        w_out_b = w_out[li].astype(BF16)
        wq_b = mem_wq[li].astype(BF16)
        wo_b = mem_wo[li].astype(BF16)
        w1x, pex, w2x = _compress_weights(nsa_cmp_pe[li], nsa_cmp_w1[li], nsa_cmp_w2[li])
        ssd_w = (ssd_conv_w[li], ssd_conv_b[li], ssd_dt_bias[li], ssd_a_log[li], ssd_d[li], ssd_norm[li])
        nsa_n = (nsa_q_norm[li], nsa_ks_norm[li], nsa_kw_norm[li])
        proj = _inproj(xp, norm_mix[li], w_in_t, w_tail, li)
        ret_out, ret_s = _ret_prompt(proj, cos128, sin128)
        ssd_out, ssm_h, conv8 = _ssd_prompt(proj, *ssd_w)
        qb, cache, _, cache_p, win_tp, kb, vt = _nsa_prep(proj, cos64, sin64, li, depth, cache_p, *nsa_n)
        rk = cache[:, 0:128].reshape(l // CMP_STRIDE, CMP_STRIDE * LANES)
        rv = cache[:, 128:256].reshape(l // CMP_STRIDE, CMP_STRIDE * LANES)
        kc_b, vct = _compress(rk, rv, w1x, pex, w2x, nsa_kc_norm[li])
        nsa_out = _nsa_prompt(qb, kc_b, vct, kb, vt, proj)
        mkv, mkv_b = _mem_kv(mem_prompt[0], norm_mem[li], mem_wk[li].astype(BF16), mem_wv[li].astype(BF16),
                             mem_k_norm[li])
        xp = _tail(xp, ret_out, ssd_out, nsa_out, w_out_b, norm_cross[li], wq_b, mem_q_norm[li], mkv_b, wo_b)
        ret_p.append(ret_s[None])
        ssm_p.append(ssm_h[None])
        conv_p.append(conv8[None, 8 - (SSD_CONV - 1):])
        win_p.append(untok(win_tp[:, :, l - wp:])[None])
        mem_p.append(mkv.reshape(1, ml, 2, MEM_HEADS, MEM_HD))
        sproj = _inproj(xs, norm_mix[li], w_in_t, w_tail, li)
        s_ret_out, ret_s_all = _ret_decode(sproj, cos128s, sin128s, state_ret, li, ret_s_all)
        s_ssd_out, s_conv, ssm_s_all = _ssd_decode(sproj, conv_t, state_ssm, li, ssm_s_all, *ssd_w)
        s_qb, s_cache, s_wrow, cache_s, s_wcol, _, _ = _nsa_prep(sproj, cos64s, sin64s, li, depth, cache_s, *nsa_n)
        s_nsa, win_s_all = _nsa_decode(
            page_table, s_qb.reshape(db, NSA_HEADS, LANES), cache_t, s_cache[:, None, :], win_t,
            s_wrow[:, None, :], s_wcol, sproj[:, None, C_GATE:C_GATE + LANES],
            sproj[:, None, C_NZ:C_NZ + NSA_WIDTH], w1x, pex, w2x, nsa_kc_norm[li], past_len, li, win_s_all)
        x1, s_q = _dec_tail_a(xs, s_ret_out, s_ssd_out, s_nsa[:, 0, :], w_out_b, norm_cross[li], wq_b,
                              mem_q_norm[li])
        xs = _dec_xattn(s_q, mem_rows, x1, wo_b, li)
        conv_s.append(jnp.transpose(s_conv, (1, 0, 2)))
    stack = jnp.stack
    return (xp[None], xs[:, None, :], stack(ret_p), ret_s_all, stack(ssm_p), ssm_s_all, stack(conv_p), stack(conv_s),
            untok(cache_p)[:, None], untok(cache_s)[:, :, None], stack(win_p), untok(win_s_all), stack(mem_p))
```

```python
import functools
import math

import numpy as np
import jax
import jax.numpy as jnp
from jax import lax
from jax.experimental import pallas as pl
from jax.experimental.pallas import tpu as pltpu

F32 = jnp.float32
BF16 = jnp.bfloat16

D_MODEL = 2048
RET_HEADS = 4
RET_DK = 128
RET_WIDTH = 512
CHUNK = 128
SSD_WIDTH = 1024
SSD_HEADDIM = 64
SSD_HEADS = 16
SSD_GROUPS = 2
SSD_STATE = 128
SSD_CONV = 4
SSD_CONV_DIM = 1536
NSA_WIDTH = 512
NSA_HEADS = 8
NSA_HD = 64
NSA_KV_HEADS = 2
CMP_BLOCK = 32
CMP_STRIDE = 16
SLC_BLOCK = 64
SLC_TOPN = 16
N_LOCAL = 2
WINDOW = 512
FORCE_SCORE = 1.0e6
MEM_HEADS = 4
MEM_HD = 128
MEM_WIDTH = 512
ROPE_THETA = 10000.0
EPS = 1e-6
NEG = -1.0e30

C_SZ = 2048
C_XBC = 3072
C_NZ = 4608
C_NQ = 5120
C_NKV = 6144
C_SDT = 6912
C_GATE = 7040
N_PROJ = 7168

LANES = 128
VMEM_LIMIT = 56 * 1024 * 1024


def _cparams(sem):
    return pltpu.CompilerParams(dimension_semantics=sem, vmem_limit_bytes=VMEM_LIMIT)


def _bf(x):
    return x.astype(BF16)


def _dot(a, b):
    return jnp.dot(a, b, preferred_element_type=F32)


def _dot_nt(a, b):
    return lax.dot_general(a, b, (((1,), (1,)), ((), ())), preferred_element_type=F32)


def _dot_tn(a, b):
    return lax.dot_general(a, b, (((0,), (0,)), ((), ())), preferred_element_type=F32)


def _split3(a):
    hi = a.astype(BF16)
    r = a - hi.astype(F32)
    mid = r.astype(BF16)
    lo = (r - mid.astype(F32)).astype(BF16)
    return hi, mid, lo


def _split2(a):
    hi = a.astype(BF16)
    lo = (a - hi.astype(F32)).astype(BF16)
    return hi, lo


def _dot_split(a, b_bf16):
    hi, mid, lo = _split3(a)
    return _dot(hi, b_bf16) + _dot(mid, b_bf16) + _dot(lo, b_bf16)


def _dot_tri(tri_bf16, a):
    hi, mid, lo = _split3(a)
    return _dot(tri_bf16, hi) + _dot(tri_bf16, mid) + _dot(tri_bf16, lo)


def _silu(x):
    return x * jax.nn.sigmoid(x)


def _softplus(x):
    return jnp.maximum(x, 0.0) + jnp.log1p(jnp.exp(-jnp.abs(x)))


def _rope128(x, cos, sin):
    return x * cos + pltpu.roll(x, 64, 1) * sin


def _rope64(x, cos, sin):
    lane = lax.broadcasted_iota(jnp.int32, x.shape, 1)
    first = (lane & 63) < 32
    partner = jnp.where(first, pltpu.roll(x, 96, 1), pltpu.roll(x, 32, 1))
    return x * cos + partner * sin


def _half_mean_mat():
    r = lax.broadcasted_iota(jnp.int32, (LANES, LANES), 0)
    c = lax.broadcasted_iota(jnp.int32, (LANES, LANES), 1)
    return jnp.where((r >> 6) == (c >> 6), 1.0 / 64.0, 0.0).astype(BF16)


def _rms64(x, w, bd):
    ms = _dot_split(x * x, bd)
    return x * lax.rsqrt(ms + EPS) * w


def _pad_lanes(v, n=LANES):
    v = v.reshape(1, -1)
    return jnp.pad(v, ((0, 0), (0, n - v.shape[1])))


def _layer_out(prev):
    if prev is None:
        return [], [], None
    return [prev], [pl.BlockSpec(memory_space=pl.ANY)], prev


def _layer_spec(prev, depth, li, block, index):
    if prev is None:
        return pl.BlockSpec((depth,) + block, lambda *a: (0,) + index(*a))
    return pl.BlockSpec((None,) + block, lambda *a: (li,) + index(*a))


def _layer_view(ref, li, fill_depth):
    if not fill_depth:
        return ref
    for d in range(fill_depth):
        if d != li:
            ref[d] = jnp.zeros(ref.shape[1:], ref.dtype)
    return ref.at[li]


def _row_only(x, b):
    rid = lax.broadcasted_iota(jnp.int32, (x.shape[0], 1), 0)
    return jnp.where(rid == b, x, jnp.zeros_like(x))


N_MAIN = 4608
PROJ_TN = 512


def _inproj_kernel(x_ref, nw_ref, wa_ref, wb_ref, o_ref, h_ref):
    j = pl.program_id(1)

    @pl.when(j == 0)
    def _():
        nw = nw_ref[...]
        rows = min(128, x_ref.shape[0])

        def body(i, c):
            r = pl.ds(pl.multiple_of(i * rows, rows), rows)
            x = x_ref[r, :]
            ms = jnp.mean(x * x, axis=-1, keepdims=True)
            h_ref[r, :] = (x * lax.rsqrt(ms + EPS) * nw).astype(BF16)
            return c

        lax.fori_loop(0, x_ref.shape[0] // rows, body, 0)

    @pl.when(j < N_MAIN // PROJ_TN)
    def _():
        o_ref[...] = _dot_nt(h_ref[...], _bf(wa_ref[...]))

    @pl.when(j >= N_MAIN // PROJ_TN)
    def _():
        o_ref[...] = _dot_nt(h_ref[...], _bf(wb_ref[...]))


def _inproj(x, nw, w_t, w_tail, li):
    m = x.shape[0]
    tm = min(m, 1024)
    nja = N_MAIN // PROJ_TN
    return pl.pallas_call(
        _inproj_kernel,
        grid=(m // tm, N_PROJ // PROJ_TN),
        in_specs=[
            pl.BlockSpec((tm, D_MODEL), lambda i, j: (i, 0)),
            pl.BlockSpec((1, D_MODEL), lambda i, j: (0, 0)),
            pl.BlockSpec((None, PROJ_TN, D_MODEL), lambda i, j: (li, jnp.minimum(j, nja - 1), 0)),
            pl.BlockSpec((PROJ_TN, D_MODEL), lambda i, j: (jnp.maximum(j - nja, 0), 0)),
        ],
        out_specs=pl.BlockSpec((tm, PROJ_TN), lambda i, j: (i, j)),
        out_shape=jax.ShapeDtypeStruct((m, N_PROJ), F32),
        scratch_shapes=[pltpu.VMEM((tm, D_MODEL), BF16)],
        compiler_params=_cparams(("arbitrary", "arbitrary")),
        name="inproj",
    )(x, nw.reshape(1, D_MODEL), w_t, w_tail)


RET_T = 512


def _ret_prompt_kernel(q_ref, k_ref, v_ref, g_ref, cos_ref, sin_ref, dec_ref, qd_ref, kd_ref, cd_ref,
                       o_ref, so_ref, s_ref):
    i = pl.program_id(0)

    @pl.when(i == 0)
    def _():
        s_ref[...] = jnp.zeros(s_ref.shape, F32)

    for c in range(RET_T // CHUNK):
        rows = slice(c * CHUNK, (c + 1) * CHUNK)
        cos = cos_ref[rows, :]
        sin = sin_ref[rows, :]
        for h in range(RET_HEADS):
            cols = slice(h * 128, (h + 1) * 128)
            q = _rope128(q_ref[rows, cols], cos, sin)
            k = _rope128(k_ref[rows, cols], cos, sin) * (RET_DK ** -0.5)
            v = v_ref[rows, cols]
            s = s_ref[h]
            qb = _bf(q)
            vb = _bf(v)
            att = _dot_nt(qb, _bf(k)) * dec_ref[h]
            o = _dot(_bf(att), vb) + _dot(qb, _bf(s)) * qd_ref[h]
            s_ref[h] = s * cd_ref[h] + _dot_tn(_bf(k * kd_ref[h]), vb)
            r = o * lax.rsqrt(jnp.mean(o * o, axis=-1, keepdims=True) + EPS)
            o_ref[rows, cols] = r * _silu(g_ref[rows, cols])

    @pl.when(i == pl.num_programs(0) - 1)
    def _():
        so_ref[...] = s_ref[...]


def _ret_gamma():
    return 1.0 - np.exp2(-5.0 - np.arange(RET_HEADS, dtype=np.float64))


def _ret_consts():
    lg = np.log(_ret_gamma())
    idx = np.arange(CHUNK, dtype=np.float64)
    diff = idx[:, None] - idx[None, :]
    dec = np.where(diff[None] >= 0, np.exp(lg[:, None, None] * np.maximum(diff, 0.0)[None]), 0.0)
    qd = np.exp(lg[:, None] * (idx + 1.0)[None])
    kd = np.exp(lg[:, None] * (CHUNK - 1.0 - idx)[None])
    cd = np.exp(lg * CHUNK)
    bc = lambda a: np.ascontiguousarray(np.broadcast_to(a[:, :, None], (RET_HEADS, CHUNK, LANES)))
    return (jnp.asarray(dec, F32), jnp.asarray(bc(qd), F32), jnp.asarray(bc(kd), F32),
            jnp.asarray(np.broadcast_to(cd[:, None, None], (RET_HEADS, 1, LANES)).copy(), F32))


def _ret_prompt(proj, cos, sin):
    l = proj.shape[0]
    dec, qd, kd, cd = _ret_consts()
    full3 = lambda shape: pl.BlockSpec(shape, lambda i: (0, 0, 0))
    return pl.pallas_call(
        _ret_prompt_kernel,
        grid=(l // RET_T,),
        in_specs=[
            pl.BlockSpec((RET_T, 512), lambda i: (i, 0)),
            pl.BlockSpec((RET_T, 512), lambda i: (i, 1)),
            pl.BlockSpec((RET_T, 512), lambda i: (i, 2)),
            pl.BlockSpec((RET_T, 512), lambda i: (i, 3)),
            pl.BlockSpec((RET_T, LANES), lambda i: (i, 0)),
            pl.BlockSpec((RET_T, LANES), lambda i: (i, 0)),
            full3((RET_HEADS, CHUNK, CHUNK)),
            full3((RET_HEADS, CHUNK, LANES)),
            full3((RET_HEADS, CHUNK, LANES)),
            full3((RET_HEADS, 1, LANES)),
        ],
        out_specs=[
            pl.BlockSpec((RET_T, 512), lambda i: (i, 0)),
            full3((RET_HEADS, RET_DK, RET_DK)),
        ],
        out_shape=[
            jax.ShapeDtypeStruct((l, RET_WIDTH), F32),
            jax.ShapeDtypeStruct((RET_HEADS, RET_DK, RET_DK), F32),
        ],
        scratch_shapes=[pltpu.VMEM((RET_HEADS, RET_DK, RET_DK), F32)],
        compiler_params=_cparams(("arbitrary",)),
        name="ret_prompt",
    )(proj, proj, proj, proj, cos, sin, dec, qd, kd, cd)


DEC_BB = 8


def _ret_decode_kernel(q_ref, k_ref, v_ref, g_ref, cos_ref, sin_ref, gam_ref, s_ref, *rest, li, fill):
    o_ref, so_ref = rest[-2:]
    so_ref = _layer_view(so_ref, li, fill)
    cos = cos_ref[...]
    sin = sin_ref[...]
    for h in range(RET_HEADS):
        cols = slice(h * 128, (h + 1) * 128)
        qb = _bf(_rope128(q_ref[:, cols], cos, sin))
        kb = _bf(_rope128(k_ref[:, cols], cos, sin) * (RET_DK ** -0.5))
        vb = _bf(v_ref[:, cols])
        gam = gam_ref[h]
        qk = jnp.sum(qb.astype(F32) * kb.astype(F32), axis=-1, keepdims=True)
        o = _bf(qk).astype(F32) * vb.astype(F32)
        rows = []
        for b in range(DEC_BB):
            s = s_ref[b, h]
            rows.append(_dot(qb, _bf(s))[b:b + 1])
            so_ref[b, h] = s * gam + _dot_tn(_row_only(kb, b), vb)
        o = o + jnp.concatenate(rows, axis=0) * gam
        r = o * lax.rsqrt(jnp.mean(o * o, axis=-1, keepdims=True) + EPS)
        o_ref[:, cols] = r * _silu(g_ref[:, cols])


def _ret_decode(proj, cos, sin, state, li, prev):
    db = proj.shape[0]
    gam = jnp.asarray(np.broadcast_to(_ret_gamma()[:, None, None], (RET_HEADS, 1, LANES)).copy(), F32)
    extra, extra_specs, alias = _layer_out(prev)
    depth = state.shape[0]
    blk = (DEC_BB, RET_HEADS, RET_DK, RET_DK)
    st_spec = pl.BlockSpec((None,) + blk, lambda i: (li, i, 0, 0, 0))
    return pl.pallas_call(
        functools.partial(_ret_decode_kernel, li=li, fill=depth if prev is None else 0),
        grid=(db // DEC_BB,),
        in_specs=[
            pl.BlockSpec((DEC_BB, 512), lambda i: (i, 0)),
            pl.BlockSpec((DEC_BB, 512), lambda i: (i, 1)),
            pl.BlockSpec((DEC_BB, 512), lambda i: (i, 2)),
            pl.BlockSpec((DEC_BB, 512), lambda i: (i, 3)),
            pl.BlockSpec((DEC_BB, LANES), lambda i: (i, 0)),
            pl.BlockSpec((DEC_BB, LANES), lambda i: (i, 0)),
            pl.BlockSpec((RET_HEADS, 1, LANES), lambda i: (0, 0, 0)),
            st_spec,
        ] + extra_specs,
        out_specs=[pl.BlockSpec((DEC_BB, 512), lambda i: (i, 0)),
                   _layer_spec(prev, depth, li, blk, lambda i: (i, 0, 0, 0))],
        out_shape=[
            jax.ShapeDtypeStruct((db, RET_WIDTH), F32),
            jax.ShapeDtypeStruct(state.shape, F32),
        ],
        input_output_aliases={} if alias is None else {8: 1},
        compiler_params=_cparams(("arbitrary",)),
        name="ret_decode",
    )(proj, proj, proj, proj, cos, sin, gam, state, *extra)


def _head_expand_mat():
    r = lax.broadcasted_iota(jnp.int32, (LANES, SSD_WIDTH), 0)
    c = lax.broadcasted_iota(jnp.int32, (LANES, SSD_WIDTH), 1)
    return jnp.where(r == (c >> 6), 1.0, 0.0).astype(BF16)


def _ssd_prompt_kernel(z_ref, xbc_ref, dt_ref, cw_ref, cb_ref, dtb_ref, alog_ref, dexp_ref, nw_ref,
                       y_ref, ho_ref, co_ref, ext_ref, ht_ref):
    i = pl.program_id(0)

    @pl.when(i == 0)
    def _():
        ext_ref[0:8, :] = jnp.zeros((8, SSD_CONV_DIM), F32)
        ht_ref[...] = jnp.zeros(ht_ref.shape, F32)

    u = xbc_ref[...]
    ext_ref[8:8 + CHUNK, :] = u
    cw = cw_ref[...]
    conv = (cb_ref[...] + cw[3:4, :] * u + cw[2:3, :] * ext_ref[7:7 + CHUNK, :]
            + cw[1:2, :] * ext_ref[6:6 + CHUNK, :] + cw[0:1, :] * ext_ref[5:5 + CHUNK, :])
    ext_ref[0:8, :] = u[CHUNK - 8:CHUNK, :]
    xbc = _silu(conv)
    xs = xbc[:, 0:SSD_WIDTH]

    dt = _softplus(dt_ref[...] + dtb_ref[...])
    a = dt * (-jnp.exp(alog_ref[...]))
    ri = lax.broadcasted_iota(jnp.int32, (CHUNK, CHUNK), 0)
    ci = lax.broadcasted_iota(jnp.int32, (CHUNK, CHUNK), 1)
    causal = ri >= ci
    tri = jnp.where(causal, 1.0, 0.0).astype(BF16)
    cum = _dot_tri(tri, a)
    cum_t = cum.T
    dt_t = dt.T
    cum_last = cum[CHUNK - 1:CHUNK, :]
    eh = _head_expand_mat()
    ecum_x = _dot_split(jnp.exp(cum), eh)
    wgt_x = _dot_split(jnp.exp(cum_last - cum) * dt, eh)
    elast_x = _dot_split(jnp.broadcast_to(jnp.exp(cum_last), (8, LANES)), eh)[0:1, :]

    lane = lax.broadcasted_iota(jnp.int32, (CHUNK, LANES), 1)
    lo_half = lane < 64
    xw = _bf(xs * wgt_x)
    y_parts = []
    ch_parts = []
    for g in range(SSD_GROUPS):
        bg = xbc[:, SSD_WIDTH + g * 128:SSD_WIDTH + (g + 1) * 128]
        cg = xbc[:, SSD_WIDTH + 256 + g * 128:SSD_WIDTH + 256 + (g + 1) * 128]
        cgb = _bf(cg)
        cb = _dot_nt(cgb, _bf(bg))
        ht = ht_ref[g]
        ch_parts.append(_dot(cgb, _bf(ht)))
        for k in range(4):
            h0 = g * 8 + 2 * k
            xp = _bf(xs[:, h0 * 64:(h0 + 2) * 64])
            ys = []
            for hh in (h0, h0 + 1):
                seg = cum[:, hh:hh + 1] - cum_t[hh:hh + 1, :]
                lm = jnp.where(causal, jnp.exp(jnp.minimum(seg, 0.0)), 0.0)
                sc = cb * lm * dt_t[hh:hh + 1, :]
                ys.append(_dot(_bf(sc), xp))
            y_parts.append(jnp.where(lo_half, ys[0], ys[1]))
        bgt = _bf(bg.T)
        ht_ref[g] = ht * elast_x[:, g * 512:(g + 1) * 512] + _dot(bgt, xw[:, g * 512:(g + 1) * 512])
    y = jnp.concatenate(y_parts, axis=1) + jnp.concatenate(ch_parts, axis=1) * ecum_x + dexp_ref[...] * xs
    gated = y * _silu(z_ref[...])
    y_ref[...] = gated * lax.rsqrt(jnp.mean(gated * gated, axis=-1, keepdims=True) + EPS) * nw_ref[...]

    @pl.when(i == pl.num_programs(0) - 1)
    def _():
        co_ref[...] = u[CHUNK - 8:CHUNK, :]
        for g in range(SSD_GROUPS):
            htf = ht_ref[g]
            for k in range(4):
                h0 = g * 8 + 2 * k
                ho_ref[h0:h0 + 2] = htf[:, k * 128:(k + 1) * 128].T.reshape(2, SSD_HEADDIM, SSD_STATE)


def _ssd_prompt(proj, conv_w, conv_b, dt_bias, a_log, d, norm_w):
    l = proj.shape[0]
    full2 = lambda shape: pl.BlockSpec(shape, lambda i: (0, 0))
    return pl.pallas_call(
        _ssd_prompt_kernel,
        grid=(l // CHUNK,),
        in_specs=[
            pl.BlockSpec((CHUNK, SSD_WIDTH), lambda i: (i, C_SZ // SSD_WIDTH)),
            pl.BlockSpec((CHUNK, SSD_CONV_DIM), lambda i: (i, C_XBC // SSD_CONV_DIM)),
            pl.BlockSpec((CHUNK, LANES), lambda i: (i, C_SDT // LANES)),
            full2((SSD_CONV, SSD_CONV_DIM)),
            full2((1, SSD_CONV_DIM)),
            full2((1, LANES)),
            full2((1, LANES)),
            full2((1, SSD_WIDTH)),
            full2((1, SSD_WIDTH)),
        ],
        out_specs=[
            pl.BlockSpec((CHUNK, SSD_WIDTH), lambda i: (i, 0)),
            pl.BlockSpec((SSD_HEADS, SSD_HEADDIM, SSD_STATE), lambda i: (0, 0, 0)),
            full2((8, SSD_CONV_DIM)),
        ],
        out_shape=[
            jax.ShapeDtypeStruct((l, SSD_WIDTH), F32),
            jax.ShapeDtypeStruct((SSD_HEADS, SSD_HEADDIM, SSD_STATE), F32),
            jax.ShapeDtypeStruct((8, SSD_CONV_DIM), F32),
        ],
        scratch_shapes=[
            pltpu.VMEM((8 + CHUNK, SSD_CONV_DIM), F32),
            pltpu.VMEM((SSD_GROUPS, SSD_STATE, 512), F32),
        ],
        compiler_params=_cparams(("arbitrary",)),
        name="ssd_prompt",
    )(proj, proj, proj, conv_w, conv_b.reshape(1, -1), _pad_lanes(dt_bias), _pad_lanes(a_log),
      jnp.repeat(d, SSD_HEADDIM).reshape(1, -1), norm_w.reshape(1, -1))


def _ssd_decode_kernel(z_ref, xbc_ref, dt_ref, cs_ref, h_ref, cw_ref, cb_ref, dtb_ref, alog_ref, dexp_ref, nw_ref,
                       *rest, li, fill):
    y_ref, co_ref, ho_ref = rest[-3:]
    ho_ref = _layer_view(ho_ref, li, fill)
    u = xbc_ref[...]
    c0 = cs_ref[0]
    c1 = cs_ref[1]
    c2 = cs_ref[2]
    cw = cw_ref[...]
    conv = cb_ref[...] + cw[3:4, :] * u + cw[2:3, :] * c2 + cw[1:2, :] * c1 + cw[0:1, :] * c0
    co_ref[0] = c1
    co_ref[1] = c2
    co_ref[2] = u
    xbc = _silu(conv)
    xs = xbc[:, 0:SSD_WIDTH]
    dt = _softplus(dt_ref[...] + dtb_ref[...])
    ea = jnp.exp(dt * (-jnp.exp(alog_ref[...])))
    eh = _head_expand_mat()
    dt_x = _dot_split(dt, eh)
    ea_x = _dot_split(ea, eh)
    dtx = dt_x * xs
    ones = jnp.ones((DEC_BB, LANES), BF16)
    ych = [[None, None] for _ in range(DEC_BB)]
    cbs = []
    for g in range(SSD_GROUPS):
        gc = slice(g * 512, (g + 1) * 512)
        bg = xbc[:, SSD_WIDTH + g * 128:SSD_WIDTH + (g + 1) * 128]
        cg = xbc[:, SSD_WIDTH + 256 + g * 128:SSD_WIDTH + 256 + (g + 1) * 128]
        cgb = _bf(cg)
        cbs.append(jnp.sum(cgb.astype(F32) * _bf(bg).astype(F32), axis=-1, keepdims=True))
        b_hi, b_lo = _split2(bg)
        for b in range(DEC_BB):
            hs = h_ref[b, g * 8:(g + 1) * 8].reshape(512, SSD_STATE)
            ych[b][g] = _dot_nt(cgb, _bf(hs))[b:b + 1]
            e_hi, e_lo = _split2(_row_only(ea_x[:, gc], b))
            decay = _dot_tn(e_hi, ones) + _dot_tn(e_lo, ones)
            x_hi, x_lo = _split2(_row_only(dtx[:, gc], b))
            upd = _dot_tn(x_hi, b_hi) + _dot_tn(x_hi, b_lo) + _dot_tn(x_lo, b_hi)
            ho_ref[b, g * 8:(g + 1) * 8] = (hs * decay + upd).reshape(8, SSD_HEADDIM, SSD_STATE)
    ych = jnp.concatenate([jnp.concatenate(r, axis=1) for r in ych], axis=0)
    lane = lax.broadcasted_iota(jnp.int32, (DEC_BB, SSD_WIDTH), 1)
    cbx = jnp.where(lane < 512, cbs[0], cbs[1])
    y = dt_x * cbx * xs + ych * ea_x + dexp_ref[...] * xs
    gated = y * _silu(z_ref[...])
    y_ref[...] = gated * lax.rsqrt(jnp.mean(gated * gated, axis=-1, keepdims=True) + EPS) * nw_ref[...]


def _ssd_decode(proj, conv_state_t, ssm_state, li, prev, conv_w, conv_b, dt_bias, a_log, d, norm_w):
    db = proj.shape[0]
    full2 = lambda shape: pl.BlockSpec(shape, lambda i: (0, 0))
    extra, extra_specs, alias = _layer_out(prev)
    depth = ssm_state.shape[0]
    blk = (DEC_BB, SSD_HEADS, SSD_HEADDIM, SSD_STATE)
    st_spec = pl.BlockSpec((None,) + blk, lambda i: (li, i, 0, 0, 0))
    return pl.pallas_call(
        functools.partial(_ssd_decode_kernel, li=li, fill=depth if prev is None else 0),
        grid=(db // DEC_BB,),
        in_specs=[
            pl.BlockSpec((DEC_BB, SSD_WIDTH), lambda i: (i, C_SZ // SSD_WIDTH)),
            pl.BlockSpec((DEC_BB, SSD_CONV_DIM), lambda i: (i, C_XBC // SSD_CONV_DIM)),
            pl.BlockSpec((DEC_BB, LANES), lambda i: (i, C_SDT // LANES)),
            pl.BlockSpec((None, SSD_CONV - 1, DEC_BB, SSD_CONV_DIM), lambda i: (li, 0, i, 0)),
            st_spec,
            full2((SSD_CONV, SSD_CONV_DIM)),
            full2((1, SSD_CONV_DIM)),
            full2((1, LANES)),
            full2((1, LANES)),
            full2((1, SSD_WIDTH)),
            full2((1, SSD_WIDTH)),
        ] + extra_specs,
        out_specs=[
            pl.BlockSpec((DEC_BB, SSD_WIDTH), lambda i: (i, 0)),
            pl.BlockSpec((SSD_CONV - 1, DEC_BB, SSD_CONV_DIM), lambda i: (0, i, 0)),
            _layer_spec(prev, depth, li, blk, lambda i: (i, 0, 0, 0)),
        ],
        out_shape=[
            jax.ShapeDtypeStruct((db, SSD_WIDTH), F32),
            jax.ShapeDtypeStruct((SSD_CONV - 1, db, SSD_CONV_DIM), F32),
            jax.ShapeDtypeStruct(ssm_state.shape, F32),
        ],
        input_output_aliases={} if alias is None else {11: 2},
        compiler_params=_cparams(("arbitrary",)),
        name="ssd_decode",
    )(proj, proj, proj, conv_state_t, ssm_state, conv_w, conv_b.reshape(1, -1), _pad_lanes(dt_bias),
      _pad_lanes(a_log), jnp.repeat(d, SSD_HEADDIM).reshape(1, -1), norm_w.reshape(1, -1), *extra)


def _value_variants(v):
    lane = lax.broadcasted_iota(jnp.int32, v.shape, 1)
    lo = lane < 64
    sw = pltpu.roll(v, 64, 1)
    one = jnp.ones_like(v)
    return [jnp.where(lo, v, one), jnp.where(lo, one, sw), jnp.where(lo, sw, one), jnp.where(lo, one, v)]


def _nsa_prep_kernel(nq_ref, nkv_ref, cos_ref, sin_ref, qn_ref, ksn_ref, kwn_ref, *rest, li, fill):
    qb_ref, cache_ref, win_ref, cache_t_ref, win_t_ref, kb_ref, vt_ref = rest[-7:]
    cache_t_ref = _layer_view(cache_t_ref, li, fill)
    cos = cos_ref[...]
    sin = sin_ref[...]
    bd = _half_mean_mat()
    qn = qn_ref[...]
    for h in range(NSA_HEADS):
        cols = slice(h * 128, (h + 1) * 128)
        x = nq_ref[:, cols]
        ms = jnp.sum(x * x, axis=-1, keepdims=True) * (1.0 / NSA_HD)
        qh = _rope64(x * lax.rsqrt(ms + EPS) * qn, cos, sin)
        qb_ref[:, cols] = _bf(qh * (NSA_HD ** -0.5))
    kc = _rope64(nkv_ref[:, 0:128], cos, sin)
    vc = nkv_ref[:, 128:256]
    ks = _rope64(_rms64(nkv_ref[:, 256:384], ksn_ref[...], bd), cos, sin)
    vs = nkv_ref[:, 384:512]
    kw = _rope64(_rms64(nkv_ref[:, 512:640], kwn_ref[...], bd), cos, sin)
    vw = nkv_ref[:, 640:768]
    for i, v in enumerate((kc, vc, ks, vs)):
        cache_ref[:, i * 128:(i + 1) * 128] = v
        cache_t_ref[i] = v.T
    for i, v in enumerate((kw, vw)):
        win_ref[:, i * 128:(i + 1) * 128] = v
        win_t_ref[i] = v.T
    kb_ref[0] = _bf(ks)
    kb_ref[1] = _bf(kw)
    for i, v in enumerate(_value_variants(vs) + _value_variants(vw)):
        vt_ref[i] = _bf(v.T)


def _nsa_prep(proj, cos, sin, li, depth, prev, qn, ksn, kwn):
    m = proj.shape[0]
    t = min(m, 256)
    extra, extra_specs, alias = _layer_out(prev)
    two = lambda w: jnp.concatenate([w, w]).reshape(1, LANES)
    full2 = lambda shape: pl.BlockSpec(shape, lambda i: (0, 0))
    return pl.pallas_call(
        functools.partial(_nsa_prep_kernel, li=li, fill=depth if prev is None else 0),
        grid=(m // t,),
        in_specs=[
            pl.BlockSpec((t, 1024), lambda i: (i, C_NQ // 1024)),
            pl.BlockSpec((t, 768), lambda i: (i, C_NKV // 768)),
            pl.BlockSpec((t, LANES), lambda i: (i, 0)),
            pl.BlockSpec((t, LANES), lambda i: (i, 0)),
            full2((1, LANES)), full2((1, LANES)), full2((1, LANES)),
        ] + extra_specs,
        out_specs=[
            pl.BlockSpec((t, 1024), lambda i: (i, 0)),
            pl.BlockSpec((t, 512), lambda i: (i, 0)),
            pl.BlockSpec((t, 256), lambda i: (i, 0)),
            _layer_spec(prev, depth, li, (4, LANES, t), lambda i: (0, 0, i)),
            pl.BlockSpec((2, LANES, t), lambda i: (0, 0, i)),
            pl.BlockSpec((2, t, LANES), lambda i: (0, i, 0)),
            pl.BlockSpec((8, LANES, t), lambda i: (0, 0, i)),
        ],
        out_shape=[
            jax.ShapeDtypeStruct((m, 1024), BF16),
            jax.ShapeDtypeStruct((m, 512), F32),
            jax.ShapeDtypeStruct((m, 256), F32),
            jax.ShapeDtypeStruct((depth, 4, LANES, m), F32),
            jax.ShapeDtypeStruct((2, LANES, m), F32),
            jax.ShapeDtypeStruct((2, m, LANES), BF16),
            jax.ShapeDtypeStruct((8, LANES, m), BF16),
        ],
        input_output_aliases={} if alias is None else {7: 3},
        compiler_params=_cparams(("arbitrary",)),
        name="nsa_prep",
    )(proj, proj, cos, sin, two(qn), two(ksn), two(kwn), *extra)


def _compress_kernel(rk_ref, rv_ref, w1_ref, pe_ref, w2_ref, kn_ref, kc_ref, vc_ref):
    tr = rk_ref.shape[0]
    bd = _half_mean_mat()

    def mlp(r, j):
        ha = _dot(_bf(r + pe_ref[2 * j:2 * j + 1, :]), w1_ref[2 * j])
        hb = _dot(_bf(r + pe_ref[2 * j + 1:2 * j + 2, :]), w1_ref[2 * j + 1])
        hid = ha + pltpu.roll(hb, tr - 1, 0)
        return _dot(_bf(_silu(hid)), w2_ref[j])

    kc = _rms64(mlp(rk_ref[...], 0), kn_ref[...], bd)
    vc = mlp(rv_ref[...], 1)
    kc_ref[...] = _bf(kc)
    vct = vc.T
    vc_ref[0] = _bf(vct)
    vc_ref[1] = _bf(pltpu.roll(vct, 64, 0))


def _compress_weights(pe, w1, w2):
    w1r = w1.reshape(2, 2, 16, 64, 64)
    z = jnp.zeros_like(w1r)
    top = jnp.concatenate([w1r, z], axis=-1)
    bot = jnp.concatenate([z, w1r], axis=-1)
    w1x = jnp.stack([top, bot], axis=3)
    w1x = w1x.reshape(4, 2048, LANES).astype(BF16)
    per = pe.reshape(2, 2, 16, 1, 64)
    pex = jnp.broadcast_to(per, (2, 2, 16, 2, 64)).reshape(4, 2048)
    z2 = jnp.zeros_like(w2)
    w2x = jnp.concatenate([jnp.concatenate([w2, z2], -1), jnp.concatenate([z2, w2], -1)], axis=1).astype(BF16)
    return w1x, pex, w2x


def _compress(rk, rv, w1x, pex, w2x, kn):
    nr = rk.shape[0]
    tr = min(nr, 512)
    two = jnp.concatenate([kn, kn]).reshape(1, LANES)
    return pl.pallas_call(
        _compress_kernel,
        grid=(nr // tr,),
        in_specs=[
            pl.BlockSpec((tr, 2048), lambda i: (i, 0)),
            pl.BlockSpec((tr, 2048), lambda i: (i, 0)),
            pl.BlockSpec((4, 2048, LANES), lambda i: (0, 0, 0)),
            pl.BlockSpec((4, 2048), lambda i: (0, 0)),
            pl.BlockSpec((2, LANES, LANES), lambda i: (0, 0, 0)),
            pl.BlockSpec((1, LANES), lambda i: (0, 0)),
        ],
        out_specs=[
            pl.BlockSpec((tr, LANES), lambda i: (i, 0)),
            pl.BlockSpec((2, LANES, tr), lambda i: (0, 0, i)),
        ],
        out_shape=[
            jax.ShapeDtypeStruct((nr, LANES), BF16),
            jax.ShapeDtypeStruct((2, LANES, nr), BF16),
        ],
        compiler_params=_cparams(("arbitrary",)),
        name="nsa_compress",
    )(rk, rv, w1x, pex, w2x, two)


QB = 128
SEL_TK = 512
WIN_TK = 128
CMP_ROWS = 128
SEL_GROUP = 4
RANK_STEP = 32


def _nsa_prompt_kernel(q_ref, kc_ref, vct_ref, kb_ref, vt_ref, ovt_ref, gate_ref, eg_ref, nz_ref, o_ref,
                       m_ref, acc_ref, st_ref, sb_ref, oc_ref):
    g = pl.program_id(0)
    qi = pl.program_id(1)
    t0 = qi * QB
    nc = kc_ref.shape[0]
    q = q_ref[...]
    qs = jnp.concatenate([q[:, 0:128], q[:, 256:384], q[:, 128:256], q[:, 384:512]], axis=0)
    tcol = t0 + (lax.broadcasted_iota(jnp.int32, (1, 4 * QB), 1) & (QB - 1))
    tq = t0 + lax.broadcasted_iota(jnp.int32, (1, QB), 1)

    def pv(vte, vto, pb):
        return jnp.concatenate([_dot(vte, pb[:, 0:2 * QB]), _dot(vto, pb[:, 2 * QB:4 * QB])], axis=1)

    def cmp_rows(nrows):
        sc = _dot_nt(kc_ref[0:nrows, :], qs)
        n_io = lax.broadcasted_iota(jnp.int32, (nrows, 1), 0)
        mask_c = (n_io * CMP_STRIDE + (CMP_BLOCK - 1)) <= tcol
        sc = jnp.where(mask_c, sc, NEG)
        mc = jnp.max(sc, axis=0, keepdims=True)
        pc = jnp.where(mask_c, jnp.exp(sc - mc), 0.0)
        zc = jnp.sum(pc, axis=0, keepdims=True)
        pc = pc * (1.0 / jnp.where(zc > 0, zc, 1.0))
        oc_ref[...] = pv(vct_ref[g, :, 0:nrows], vct_ref[1 - g, :, 0:nrows], _bf(pc))
        p4 = pc[:, 0:QB] + pc[:, QB:2 * QB] + pc[:, 2 * QB:3 * QB] + pc[:, 3 * QB:4 * QB]
        st_ref[...] = _dot_tri(ovt_ref[:, 0:nrows], p4)

    n_vis = (t0 + QB - CMP_BLOCK) // CMP_STRIDE + 1
    for nrows in range(CMP_ROWS, nc + 1, CMP_ROWS):
        pl.when((n_vis > nrows - CMP_ROWS) & (n_vis <= nrows))(functools.partial(cmp_rows, nrows))
    o_c = oc_ref[...]
    imp = st_ref[...]

    jb = lax.broadcasted_iota(jnp.int32, (LANES, QB), 0)
    cur = tq >> 6
    forced = (jb == 0) | ((jb <= cur) & (jb > cur - N_LOCAL))
    valid = jb <= cur
    score = jnp.where(forced, FORCE_SCORE, imp)
    score = jnp.where(valid, score, -FORCE_SCORE)
    st_ref[...] = score
    n_blk = ((t0 + QB - 1) >> 6) + 1
    sb_ref[...] = jnp.full(sb_ref.shape, NEG, F32)

    def rank_rows(nrows):
        sc_n = score[0:nrows]
        jb_n = jb[0:nrows]

        def rank_body(i, rank):
            row = st_ref[pl.ds(i, 1), :]
            beats = (row > sc_n) | ((row == sc_n) & (i < jb_n))
            return rank + jnp.where(beats, 1.0, 0.0)

        rank = lax.fori_loop(0, n_blk, rank_body, jnp.zeros((nrows, QB), F32))
        sb_ref[0:nrows, :] = jnp.where((rank < SLC_TOPN) & valid[0:nrows], 0.0, NEG)

    for nrows in range(RANK_STEP, LANES + 1, RANK_STEP):
        pl.when((n_blk > nrows - RANK_STEP) & (n_blk <= nrows))(functools.partial(rank_rows, nrows))

    def reset():
        m_ref[...] = jnp.full(m_ref.shape, NEG, F32)
        acc_ref[...] = jnp.zeros(acc_ref.shape, F32)

    def update(kidx, ve, vo, tk, tiles):
        m_old = m_ref[...]
        m_new = m_old
        ss = []
        for k0, bias, causal in tiles:
            s = _dot_nt(kb_ref[kidx, pl.ds(k0, tk), :], qs)
            if bias is not None:
                s = s + bias
            if causal:
                kpos = k0 + lax.broadcasted_iota(jnp.int32, (tk, 1), 0)
                s = jnp.where(kpos <= tcol, s, NEG)
            m_new = jnp.maximum(m_new, jnp.max(s, axis=0, keepdims=True))
            ss.append(s)
        acc = jnp.exp(m_old - m_new) * acc_ref[...]
        for (k0, _, _), s in zip(tiles, ss):
            acc = acc + pv(vt_ref[ve, :, pl.ds(k0, tk)], vt_ref[vo, :, pl.ds(k0, tk)], _bf(jnp.exp(s - m_new)))
        acc_ref[...] = acc
        m_ref[...] = m_new

    def result():
        a = acc_ref[...]
        den = pltpu.roll(a, 64, 0)
        return a / jnp.where(den > 0, den, 1.0)

    def sel_bias(k0):
        rows = sb_ref[pl.ds(pl.multiple_of(k0 // SLC_BLOCK, 8), SEL_TK // SLC_BLOCK), :]
        b = jnp.concatenate([jnp.broadcast_to(rows[r:r + 1, :], (SLC_BLOCK, QB))
                             for r in range(SEL_TK // SLC_BLOCK)], axis=0)
        return jnp.concatenate([b, b, b, b], axis=1)

    def sel_tile(k0, causal):
        return (k0, sel_bias(k0), causal)

    reset()
    vse = 2 * g
    n_full = t0 // SEL_TK

    def sel_body(kp, c):
        k0 = pl.multiple_of(kp * SEL_GROUP * SEL_TK, SEL_GROUP * SEL_TK)
        update(0, vse, vse + 1, SEL_TK, [sel_tile(k0 + i * SEL_TK, False) for i in range(SEL_GROUP)])
        return c

    lax.fori_loop(0, n_full // SEL_GROUP, sel_body, 0)
    k_diag = pl.multiple_of(n_full * SEL_TK, SEL_TK)

    for rem in range(SEL_GROUP):
        @pl.when(n_full % SEL_GROUP == rem)
        def _(rem=rem):
            update(0, vse, vse + 1, SEL_TK,
                   [sel_tile(k_diag - (rem - i) * SEL_TK, False) for i in range(rem)] + [sel_tile(k_diag, True)])

    o_s = result()

    reset()
    vwe = 4 + 2 * g
    n_old = WINDOW // WIN_TK

    def win_tile(kt, bound, causal):
        k0 = pl.multiple_of(kt * WIN_TK, WIN_TK)
        bias = None
        if bound:
            kpos = k0 + lax.broadcasted_iota(jnp.int32, (WIN_TK, 1), 0)
            bias = jnp.where(tcol - kpos <= WINDOW, 0.0, NEG)
        return (k0, bias, causal)

    @pl.when(qi >= n_old)
    def _():
        update(1, vwe, vwe + 1, WIN_TK, [win_tile(qi - n_old, True, False)]
               + [win_tile(qi - n_old + i, False, False) for i in range(1, n_old)] + [win_tile(qi, False, True)])

    @pl.when(qi < n_old)
    def _():
        def win_body(kt, c):
            update(1, vwe, vwe + 1, WIN_TK, [win_tile(kt, False, False)])
            return c

        lax.fori_loop(0, qi, win_body, 0)
        update(1, vwe, vwe + 1, WIN_TK, [win_tile(qi, False, True)])

    o_w = result()

    gx = _dot_split(jax.nn.sigmoid(gate_ref[...]), eg_ref[0])
    lo_rows = lax.broadcasted_iota(jnp.int32, (LANES, QB), 0) < 64
    for k in range(2):
        ca = slice(k * QB, (k + 1) * QB)
        cb = slice(2 * QB + k * QB, 2 * QB + (k + 1) * QB)
        cols = slice(k * 128, (k + 1) * 128)
        tile = lambda o: jnp.where(lo_rows, o[:, ca], o[:, cb]).T
        o = (gx[:, k * 128:(k + 1) * 128] * tile(o_c)
             + gx[:, 256 + k * 128:256 + (k + 1) * 128] * tile(o_s)
             + gx[:, 512 + k * 128:512 + (k + 1) * 128] * tile(o_w))
        o_ref[:, cols] = o * _silu(nz_ref[:, cols])


def _overlap_mat(nc):
    n = np.arange(nc)[:, None]
    j = np.arange(LANES)[None, :]
    ov = ((n * CMP_STRIDE < (j + 1) * SLC_BLOCK) & (n * CMP_STRIDE + CMP_BLOCK - 1 >= j * SLC_BLOCK))
    return ov.astype(np.float32)


def _gate_expand_mat():
    eg = np.zeros((NSA_KV_HEADS, LANES, 3 * 256), np.float32)
    for g in range(NSA_KV_HEADS):
        for hh in range(4):
            for c in range(3):
                eg[g, (g * 4 + hh) * 3 + c, c * 256 + hh * 64:c * 256 + (hh + 1) * 64] = 1.0
    return jnp.asarray(eg, BF16)


def _nsa_prompt(qb, kc, vct, kb, vt, proj):
    l = qb.shape[0]
    nc = kc.shape[0]
    assert l // SLC_BLOCK <= LANES and l % SEL_TK == 0 and nc % CMP_ROWS == 0
    ovt = jnp.asarray(_overlap_mat(nc).T, BF16)
    return pl.pallas_call(
        _nsa_prompt_kernel,
        grid=(NSA_KV_HEADS, l // QB),
        in_specs=[
            pl.BlockSpec((QB, 512), lambda g, i: (i, g)),
            pl.BlockSpec((nc, LANES), lambda g, i: (0, 0)),
            pl.BlockSpec((2, LANES, nc), lambda g, i: (0, 0, 0)),
            pl.BlockSpec((2, l, LANES), lambda g, i: (0, 0, 0)),
            pl.BlockSpec((8, LANES, l), lambda g, i: (0, 0, 0)),
            pl.BlockSpec((LANES, nc), lambda g, i: (0, 0)),
            pl.BlockSpec((QB, LANES), lambda g, i: (i, C_GATE // LANES)),
            pl.BlockSpec((1, LANES, 768), lambda g, i: (g, 0, 0)),
            pl.BlockSpec((QB, 256), lambda g, i: (i, C_NZ // 256 + g)),
        ],
        out_specs=pl.BlockSpec((QB, 256), lambda g, i: (i, g)),
        out_shape=jax.ShapeDtypeStruct((l, NSA_WIDTH), F32),
        scratch_shapes=[
            pltpu.VMEM((1, 4 * QB), F32),
            pltpu.VMEM((LANES, 4 * QB), F32),
            pltpu.VMEM((LANES, QB), F32),
            pltpu.VMEM((LANES, QB), F32),
            pltpu.VMEM((LANES, 4 * QB), F32),
        ],
        compiler_params=_cparams(("arbitrary", "arbitrary")),
        name="nsa_prompt",
    )(qb, kc, vct, kb, vt, ovt, proj, _gate_expand_mat(), proj)


def _softmax_with_new(s, valid, s_new):
    s = jnp.where(valid, s, NEG)
    m = jnp.maximum(jnp.max(s, axis=1, keepdims=True), s_new)
    p = jnp.where(valid, jnp.exp(s - m), 0.0)
    pn = jnp.exp(s_new - m)
    z = jnp.sum(p, axis=1, keepdims=True) + pn
    return p / z, pn / z


DEC_ROWS = 2


def _nsa_decode_kernel(pt_ref, q_ref, cache_hbm, crow_ref, win_ref, wrow_ref, wcol_ref, gate_ref, nz_ref,
                       w1_ref, pe_ref, w2_ref, kn_ref, ov_ref, ek_ref, *rest, qpos, li, fill):
    o_ref, wo_ref, buf, tok_ref, sem = rest[-5:]
    wo_ref = _layer_view(wo_ref, li, fill)
    step = pl.program_id(0)
    slot = step % 2
    n_pages = buf.shape[2]
    page = buf.shape[5]
    t = n_pages * page
    nr = t // CMP_STRIDE
    wb = win_ref.shape[-1]
    n_cmp = (t + 1 - CMP_BLOCK) // CMP_STRIDE + 1

    def page_copies(st, s):
        return [pltpu.make_async_copy(cache_hbm.at[li, pt_ref[st * DEC_ROWS + r, p]], buf.at[s, r, p], sem.at[s])
                for r in range(DEC_ROWS) for p in range(n_pages)]

    @pl.when(step == 0)
    def _():
        for c in page_copies(0, 0):
            c.start()

    @pl.when(step + 1 < pl.num_programs(0))
    def _():
        for c in page_copies(step + 1, 1 - slot):
            c.start()

    for c in page_copies(step, slot):
        c.wait()

    bd = _half_mean_mat()
    r8 = lax.broadcasted_iota(jnp.int32, (8, 8), 0)
    c8 = lax.broadcasted_iota(jnp.int32, (8, 8), 1)
    gsum = jnp.where((r8 >> 2) == (c8 >> 2), 1.0, 0.0).astype(BF16)
    j_io = lax.broadcasted_iota(jnp.int32, (1, LANES), 1)
    cur = qpos // SLC_BLOCK
    forced = (j_io == 0) | ((j_io <= cur) & (j_io > cur - N_LOCAL))
    valid = j_io <= cur
    n_io = lax.broadcasted_iota(jnp.int32, (1, nr), 1)
    mask_c = ((n_io * CMP_STRIDE + (CMP_BLOCK - 1)) <= qpos) & (n_io < n_cmp)
    i_io = lax.broadcasted_iota(jnp.int32, (1, wb), 1)
    valid_w = ((wb - i_io) <= WINDOW) & ((qpos - wb + i_io) >= 0)
    lane8 = lax.broadcasted_iota(jnp.int32, (8, LANES), 1)
    row8 = lax.broadcasted_iota(jnp.int32, (8, LANES), 0)
    lo = lax.broadcasted_iota(jnp.int32, (1, LANES), 1) < 64
    last = lax.broadcasted_iota(jnp.int32, (LANES, wb), 1) == wb - 1
    ncol = wcol_ref.shape[2]

    def one_row(r):
        q8 = q_ref[r]
        q8f = q8.astype(F32)

        for p in range(n_pages):
            for j in range(2):
                tok_ref[r, j, p * page:(p + 1) * page, :] = buf[slot, r, p, j].T

        def mlp(j):
            acc_a = jnp.zeros((nr, LANES), F32)
            acc_b = jnp.zeros((nr, LANES), F32)
            for l2 in range(CMP_STRIDE // 2):
                x = jnp.concatenate([tok_ref[r, j, pl.ds(2 * l2, nr, stride=CMP_STRIDE), :],
                                     tok_ref[r, j, pl.ds(2 * l2 + 1, nr, stride=CMP_STRIDE), :]], axis=1)
                cols = slice(l2 * 256, (l2 + 1) * 256)
                acc_a = acc_a + _dot(_bf(x + pe_ref[2 * j:2 * j + 1, cols]), w1_ref[2 * j, cols, :])
                acc_b = acc_b + _dot(_bf(x + pe_ref[2 * j + 1:2 * j + 2, cols]), w1_ref[2 * j + 1, cols, :])
            hid = acc_a + pltpu.roll(acc_b, nr - 1, 0)
            return _dot(_bf(_silu(hid)), w2_ref[j])

        kcc = _rms64(mlp(0), kn_ref[...], bd)
        vcc = mlp(1)

        sc = jnp.where(mask_c, _dot_nt(q8, _bf(kcc)), NEG)
        mc = jnp.max(sc, axis=1, keepdims=True)
        pc = jnp.where(mask_c, jnp.exp(sc - mc), 0.0)
        zc = jnp.sum(pc, axis=1, keepdims=True)
        pc = pc / jnp.where(zc > 0, zc, 1.0)
        o_c = _dot(_bf(pc), _bf(vcc))

        imp = _dot_split(_dot_tri(gsum, pc), ov_ref[...])
        score = jnp.where(forced, FORCE_SCORE, imp)
        score = jnp.where(valid, score, -FORCE_SCORE)
        rank = jnp.zeros((8, LANES), F32)
        for i in range(cur + 1):
            col = score[:, i:i + 1]
            rank = rank + jnp.where((col > score) | ((col == score) & (i < j_io)), 1.0, 0.0)
        sel = jnp.where((rank < SLC_TOPN) & valid, 1.0, 0.0)

        crow = crow_ref[r]
        ks_new = _bf(crow[:, 256:384]).astype(F32)
        vs_new = _bf(crow[:, 384:512]).astype(F32)
        sel_past = _dot(_bf(sel), ek_ref[...]) > 0.5
        s_s = jnp.concatenate([_dot(q8, _bf(buf[slot, r, p, 2])) for p in range(n_pages)], axis=1)
        s_new = jnp.where(sel[:, cur:cur + 1] > 0.5, jnp.sum(q8f * ks_new, axis=-1, keepdims=True), NEG)
        p_s, pn_s = _softmax_with_new(s_s, sel_past, s_new)
        p_sb = _bf(p_s)
        o_s = _bf(pn_s).astype(F32) * vs_new
        for p in range(n_pages):
            o_s = o_s + _dot_nt(p_sb[:, p * page:(p + 1) * page], _bf(buf[slot, r, p, 3]))

        wrow = wrow_ref[r]
        kw_new = _bf(wrow[:, 0:128]).astype(F32)
        vw_new = _bf(wrow[:, 128:256]).astype(F32)
        s_w = _dot(q8, _bf(win_ref[r, 0]))
        p_w, pn_w = _softmax_with_new(s_w, valid_w, jnp.sum(q8f * kw_new, axis=-1, keepdims=True))
        o_w = _dot_nt(_bf(p_w), _bf(win_ref[r, 1])) + _bf(pn_w).astype(F32) * vw_new

        sig = jnp.broadcast_to(jax.nn.sigmoid(gate_ref[r]), (8, LANES))
        gate = lambda c: jnp.sum(jnp.where(lane8 == row8 * 3 + c, sig, 0.0), axis=-1, keepdims=True)
        o8 = gate(0) * o_c + gate(1) * o_s + gate(2) * o_w

        sw = pltpu.roll(o8, 64, 1)
        flat = jnp.concatenate([
            jnp.where(lo, o8[0:1], sw[1:2]), jnp.where(lo, o8[2:3], sw[3:4]),
            jnp.where(lo, sw[4:5], o8[5:6]), jnp.where(lo, sw[6:7], o8[7:8])], axis=1)
        o_ref[r] = flat * _silu(nz_ref[r])

        own_col = lax.broadcasted_iota(jnp.int32, (LANES, ncol), 1) == ((step * DEC_ROWS + r) % ncol)
        for j in range(2):
            col = jnp.sum(jnp.where(own_col, wcol_ref[j], 0.0), axis=1, keepdims=True)
            wo_ref[r, j] = jnp.where(last, col, pltpu.roll(win_ref[r, j], wb - 1, 1))

    for r in range(DEC_ROWS):
        one_row(r)


def _nsa_decode(pt, q8, cache_t, crow, win_t, wrow, wcol, gate, nz, w1x, pex, w2x, kn, qpos, li, prev):
    db, n_pages = pt.shape
    page = cache_t.shape[-1]
    t = n_pages * page
    wb = win_t.shape[-1]
    nr = t // CMP_STRIDE
    assert nr == LANES and page == LANES and t % SLC_BLOCK == 0 and qpos // SLC_BLOCK < LANES and db % LANES == 0
    ek = (np.arange(LANES)[:, None] == (np.arange(t)[None, :] // SLC_BLOCK))
    ek = jnp.asarray(ek.astype(np.float32), BF16)
    ov = jnp.asarray(_overlap_mat(nr), BF16)
    two = jnp.concatenate([kn, kn]).reshape(1, LANES)
    rows = DEC_ROWS
    row3 = lambda w: pl.BlockSpec((rows, 1, w), lambda i, pt: (i, 0, 0))
    full = lambda a: pl.BlockSpec(a.shape, lambda i, pt: (0,) * a.ndim)
    win_blk = (rows, 2, LANES, wb)
    extra, extra_specs, alias = _layer_out(prev)
    depth = win_t.shape[0]
    return pl.pallas_call(
        functools.partial(_nsa_decode_kernel, qpos=qpos, li=li, fill=depth if prev is None else 0),
        grid_spec=pltpu.PrefetchScalarGridSpec(
            num_scalar_prefetch=1,
            grid=(db // rows,),
            in_specs=[
                pl.BlockSpec((rows, 8, LANES), lambda i, pt: (i, 0, 0)),
                pl.BlockSpec(memory_space=pl.ANY),
                row3(512),
                pl.BlockSpec((None,) + win_blk, lambda i, pt: (li, i, 0, 0, 0)),
                row3(256),
                pl.BlockSpec((2, LANES, LANES), lambda i, pt: (0, 0, (i * rows) // LANES)),
                row3(LANES), row3(512),
                full(w1x), full(pex), full(w2x), full(two), full(ov), full(ek),
            ] + extra_specs,
            out_specs=[row3(512), _layer_spec(prev, depth, li, win_blk, lambda i, pt: (i, 0, 0, 0))],
            scratch_shapes=[
                pltpu.VMEM((2, rows, n_pages, 4, LANES, page), F32),
                pltpu.VMEM((rows, 2, t, LANES), F32),
                pltpu.SemaphoreType.DMA((2,)),
            ],
        ),
        out_shape=[jax.ShapeDtypeStruct((db, 1, NSA_WIDTH), F32), jax.ShapeDtypeStruct(win_t.shape, F32)],
        input_output_aliases={} if alias is None else {15: 1},
        compiler_params=_cparams(("arbitrary",)),
        name="nsa_decode",
    )(pt, q8, cache_t, crow, win_t, wrow, wcol, gate, nz, w1x, pex, w2x, two, ov, ek, *extra)


def _mem_kv_kernel(mem_ref, nw_ref, wk_ref, wv_ref, kn_ref, kv_ref, kvb_ref):
    ml = mem_ref.shape[0]
    x = mem_ref[...]
    m = _bf(x * lax.rsqrt(jnp.mean(x * x, axis=-1, keepdims=True) + EPS) * nw_ref[...])
    k = _dot(m, wk_ref[...])
    v = _dot(m, wv_ref[...])
    for h in range(MEM_HEADS):
        cols = slice(h * 128, (h + 1) * 128)
        kh = k[:, cols]
        kh = kh * lax.rsqrt(jnp.mean(kh * kh, axis=-1, keepdims=True) + EPS) * kn_ref[...]
        kv_ref[pl.ds(h, ml, stride=2 * MEM_HEADS), :] = kh
        kv_ref[pl.ds(MEM_HEADS + h, ml, stride=2 * MEM_HEADS), :] = v[:, cols]
        kvb_ref[:, cols] = _bf(kh)
    kvb_ref[:, MEM_WIDTH:2 * MEM_WIDTH] = _bf(v)


def _mem_kv(mem, nw, wk, wv, kn):
    ml = mem.shape[0]
    return pl.pallas_call(
        _mem_kv_kernel,
        out_shape=[jax.ShapeDtypeStruct((ml * 2 * MEM_HEADS, MEM_HD), F32),
                   jax.ShapeDtypeStruct((ml, 2 * MEM_WIDTH), BF16)],
        compiler_params=pltpu.CompilerParams(vmem_limit_bytes=VMEM_LIMIT),
        name="mem_kv",
    )(mem, nw.reshape(1, -1), wk, wv, kn.reshape(1, -1))


TAIL_T = 512


def _out_proj(x_ref, ret_ref, ssd_ref, nsa_ref, wout_ref):
    return (x_ref[...] + _dot(_bf(ret_ref[...]), wout_ref[0:512, :])
            + _dot(_bf(ssd_ref[...]), wout_ref[512:1536, :])
            + _dot(_bf(nsa_ref[...]), wout_ref[1536:2048, :]))


def _cross_q(x1, ncw_ref, wq_ref, qn_ref):
    h = _bf(x1 * lax.rsqrt(jnp.mean(x1 * x1, axis=-1, keepdims=True) + EPS) * ncw_ref[...])
    q = _dot(h, wq_ref[...])
    out = []
    for hd in range(MEM_HEADS):
        qh = q[:, hd * 128:(hd + 1) * 128]
        out.append(_bf(qh * lax.rsqrt(jnp.mean(qh * qh, axis=-1, keepdims=True) + EPS) * qn_ref[...]))
    return out


def _tail_kernel(x_ref, ret_ref, ssd_ref, nsa_ref, wout_ref, ncw_ref, wq_ref, qn_ref, kvb_ref, wo_ref, y_ref):
    x1 = _out_proj(x_ref, ret_ref, ssd_ref, nsa_ref, wout_ref)
    outs = []
    for hd, qh in enumerate(_cross_q(x1, ncw_ref, wq_ref, qn_ref)):
        s = _dot_nt(qh, kvb_ref[:, hd * 128:(hd + 1) * 128]) * (MEM_HD ** -0.5)
        s = s - jnp.max(s, axis=-1, keepdims=True)
        p = jnp.exp(s)
        p = p / jnp.sum(p, axis=-1, keepdims=True)
        outs.append(_dot(_bf(p), kvb_ref[:, MEM_WIDTH + hd * 128:MEM_WIDTH + (hd + 1) * 128]))
    y_ref[...] = x1 + _dot(_bf(jnp.concatenate(outs, axis=1)), wo_ref[...])


def _tail(x, ret, ssd, nsa, wout, ncw, wq, qn, kvb, wo):
    m = x.shape[0]
    t = min(m, TAIL_T)
    ml = kvb.shape[0]
    full2 = lambda shape: pl.BlockSpec(shape, lambda i: (0, 0))
    return pl.pallas_call(
        _tail_kernel,
        grid=(m // t,),
        in_specs=[
            pl.BlockSpec((t, D_MODEL), lambda i: (i, 0)),
            pl.BlockSpec((t, RET_WIDTH), lambda i: (i, 0)),
            pl.BlockSpec((t, SSD_WIDTH), lambda i: (i, 0)),
            pl.BlockSpec((t, NSA_WIDTH), lambda i: (i, 0)),
            full2((D_MODEL, D_MODEL)),
            full2((1, D_MODEL)),
            full2((D_MODEL, MEM_WIDTH)),
            full2((1, MEM_HD)),
            full2((ml, 2 * MEM_WIDTH)),
            full2((MEM_WIDTH, D_MODEL)),
        ],
        out_specs=pl.BlockSpec((t, D_MODEL), lambda i: (i, 0)),
        out_shape=jax.ShapeDtypeStruct((m, D_MODEL), F32),
        compiler_params=_cparams(("arbitrary",)),
        name="layer_tail",
    )(x, ret, ssd, nsa, wout, ncw.reshape(1, -1), wq, qn.reshape(1, -1), kvb, wo)


def _dec_tail_a_kernel(x_ref, ret_ref, ssd_ref, nsa_ref, wout_ref, ncw_ref, wq_ref, qn_ref, x1_ref, q_ref):
    x1 = _out_proj(x_ref, ret_ref, ssd_ref, nsa_ref, wout_ref)
    x1_ref[...] = x1
    for hd, qh in enumerate(_cross_q(x1, ncw_ref, wq_ref, qn_ref)):
        q_ref[:, hd * 128:(hd + 1) * 128] = qh


def _dec_tail_a(x, ret, ssd, nsa, wout, ncw, wq, qn):
    m = x.shape[0]
    return pl.pallas_call(
        _dec_tail_a_kernel,
        out_shape=[jax.ShapeDtypeStruct((m, D_MODEL), F32), jax.ShapeDtypeStruct((m, MEM_WIDTH), BF16)],
        compiler_params=pltpu.CompilerParams(vmem_limit_bytes=VMEM_LIMIT),
        name="dec_tail_a",
    )(x, ret, ssd, nsa, wout, ncw.reshape(1, -1), wq, qn.reshape(1, -1))


def _dec_xattn_kernel(q_ref, mem_ref, x1_ref, wo_ref, y_ref):
    ml = mem_ref.shape[-2] // (2 * MEM_HEADS)
    lane = lax.broadcasted_iota(jnp.int32, (8, MEM_WIDTH), 1)
    row = lax.broadcasted_iota(jnp.int32, (8, MEM_WIDTH), 0)
    own = (lane >> 7) == row
    outs = []
    for b in range(DEC_BB):
        q4 = _bf(jnp.where(own, jnp.broadcast_to(q_ref[b:b + 1, :].astype(F32), (8, MEM_WIDTH)), 0.0))
        kb = jnp.concatenate([_bf(mem_ref[b, pl.ds(h, ml, stride=2 * MEM_HEADS), :]) for h in range(MEM_HEADS)], axis=1)
        vb = jnp.concatenate([_bf(mem_ref[b, pl.ds(MEM_HEADS + h, ml, stride=2 * MEM_HEADS), :])
                              for h in range(MEM_HEADS)], axis=1)
        s = _dot_nt(q4, kb) * (MEM_HD ** -0.5)
        s = s - jnp.max(s, axis=-1, keepdims=True)
        p = jnp.exp(s)
        p = p / jnp.sum(p, axis=-1, keepdims=True)
        o4 = _dot(_bf(p), vb)
        outs.append(jnp.sum(jnp.where(own, o4, 0.0), axis=0, keepdims=True))
    y_ref[...] = x1_ref[...] + _dot(_bf(jnp.concatenate(outs, axis=0)), wo_ref[...])


def _dec_xattn(q, mem, x1, wo, li):
    db = q.shape[0]
    rows = mem.shape[2]
    return pl.pallas_call(
        _dec_xattn_kernel,
        grid=(db // DEC_BB,),
        in_specs=[
            pl.BlockSpec((DEC_BB, MEM_WIDTH), lambda i: (i, 0)),
            pl.BlockSpec((None, DEC_BB, rows, MEM_HD), lambda i: (li, i, 0, 0)),
            pl.BlockSpec((DEC_BB, D_MODEL), lambda i: (i, 0)),
            pl.BlockSpec((MEM_WIDTH, D_MODEL), lambda i: (0, 0)),
        ],
        out_specs=pl.BlockSpec((DEC_BB, D_MODEL), lambda i: (i, 0)),
        out_shape=jax.ShapeDtypeStruct((db, D_MODEL), F32),
        compiler_params=_cparams(("arbitrary",)),
        name="dec_xattn",
    )(q, mem, x1, wo)


def _prep_w_tail(w_t, li):
    k = w_t.shape[2]
    wt = lambda a, b: w_t[li, a:b]
    nq = wt(4624, 5136).reshape(NSA_HEADS, NSA_HD, k)
    z = jnp.zeros_like(nq)
    nq_pad = jnp.concatenate([
        jnp.concatenate([nq[:4], z[:4]], axis=1),
        jnp.concatenate([z[4:], nq[4:]], axis=1)], axis=0).reshape(NSA_HEADS * LANES, k)
    padr = lambda a: jnp.pad(a, ((0, LANES - a.shape[0]), (0, 0)))
    return jnp.concatenate([
        wt(5928, 6440),
        nq_pad,
        wt(5136, 5904),
        padr(wt(4608, 4624)),
        padr(wt(5904, 5928)),
    ], axis=0)


def _rope_tables(pos, head_dim, rows):
    half = head_dim // 2
    inv = jnp.exp(-math.log(ROPE_THETA) * jnp.arange(half, dtype=F32) / half)
    ang = pos.astype(F32)[:, None] * inv[None, :]
    cos = jnp.cos(ang)
    sin = jnp.sin(ang)
    reps = LANES // head_dim
    cos_t = jnp.tile(jnp.concatenate([cos, cos], axis=-1), (1, reps))
    sin_t = jnp.tile(jnp.concatenate([-sin, sin], axis=-1), (1, reps))
    if cos_t.shape[0] != rows:
        cos_t = jnp.broadcast_to(cos_t, (rows, LANES))
        sin_t = jnp.broadcast_to(sin_t, (rows, LANES))
    return cos_t, sin_t


def kernel(x_prompt, x_sample, mem_prompt, state_ret, state_ssm, state_conv, cache_nsa_kv, cache_win_kv,
           cache_mem_kv, page_table, norm_mix, w_in, ssd_conv_w, ssd_conv_b, ssd_dt_bias, ssd_a_log, ssd_d,
           ssd_norm, nsa_q_norm, nsa_kc_norm, nsa_ks_norm, nsa_kw_norm, nsa_cmp_pe, nsa_cmp_w1, nsa_cmp_w2,
           w_out, norm_cross, norm_mem, mem_wq, mem_wk, mem_wv, mem_q_norm, mem_k_norm, mem_wo):
    b, l, _ = x_prompt.shape
    assert b == 1
    db, dl, _ = x_sample.shape
    assert dl == 1 and db % DEC_BB == 0
    depth = w_in.shape[0]
    n_pages = page_table.shape[1]
    page = cache_nsa_kv.shape[2]
    past_len = n_pages * page
    wbuf = cache_win_kv.shape[2]
    ml = cache_mem_kv.shape[2]
    wp = min(WINDOW, l)
    pos_p = jnp.arange(l, dtype=jnp.int32)
    pos_s = jnp.full((1,), past_len, dtype=jnp.int32)
    cos128, sin128 = _rope_tables(pos_p, RET_DK, l)
    cos64, sin64 = _rope_tables(pos_p, NSA_HD, l)
    cos128s, sin128s = _rope_tables(pos_s, RET_DK, db)
    cos64s, sin64s = _rope_tables(pos_s, NSA_HD, db)

    xp = x_prompt[0]
    xs = x_sample[:, 0, :]
    w_in_t = jnp.swapaxes(w_in, 1, 2)
    cache_t = jnp.transpose(cache_nsa_kv, (0, 1, 3, 4, 5, 2)).reshape(depth, -1, 4, LANES, page)
    win_t = jnp.transpose(cache_win_kv, (0, 1, 3, 4, 5, 2)).reshape(depth, db, 2, LANES, wbuf)
    conv_t = jnp.transpose(state_conv, (0, 2, 1, 3))
    mem_rows = cache_mem_kv.reshape(depth, db, ml * 2 * MEM_HEADS, MEM_HD)
    untok = lambda a: jnp.moveaxis(a.reshape(a.shape[:-2] + (NSA_KV_HEADS, NSA_HD, a.shape[-1])), -1, -4)
    ret_p, ssm_p, conv_p, win_p, mem_p, conv_s = [], [], [], [], [], []
    cache_p = cache_s = ret_s_all = ssm_s_all = win_s_all = None
    for li in range(depth):
        w_tail = _prep_w_tail(w_in_t, li)
        w_out_b = w_out[li].astype(BF16)
        wq_b = mem_wq[li].astype(BF16)
        wo_b = mem_wo[li].astype(BF16)
        w1x, pex, w2x = _compress_weights(nsa_cmp_pe[li], nsa_cmp_w1[li], nsa_cmp_w2[li])
        ssd_w = (ssd_conv_w[li], ssd_conv_b[li], ssd_dt_bias[li], ssd_a_log[li], ssd_d[li], ssd_norm[li])
        nsa_n = (nsa_q_norm[li], nsa_ks_norm[li], nsa_kw_norm[li])
        proj = _inproj(xp, norm_mix[li], w_in_t, w_tail, li)
        ret_out, ret_s = _ret_prompt(proj, cos128, sin128)
        ssd_out, ssm_h, conv8 = _ssd_prompt(proj, *ssd_w)
        qb, cache, _, cache_p, win_tp, kb, vt = _nsa_prep(proj, cos64, sin64, li, depth, cache_p, *nsa_n)
        rk = cache[:, 0:128].reshape(l // CMP_STRIDE, CMP_STRIDE * LANES)
        rv = cache[:, 128:256].reshape(l // CMP_STRIDE, CMP_STRIDE * LANES)
        kc_b, vct = _compress(rk, rv, w1x, pex, w2x, nsa_kc_norm[li])
        nsa_out = _nsa_prompt(qb, kc_b, vct, kb, vt, proj)
        mkv, mkv_b = _mem_kv(mem_prompt[0], norm_mem[li], mem_wk[li].astype(BF16), mem_wv[li].astype(BF16),
                             mem_k_norm[li])
        xp = _tail(xp, ret_out, ssd_out, nsa_out, w_out_b, norm_cross[li], wq_b, mem_q_norm[li], mkv_b, wo_b)
        ret_p.append(ret_s[None])
        ssm_p.append(ssm_h[None])
        conv_p.append(conv8[None, 8 - (SSD_CONV - 1):])
        win_p.append(untok(win_tp[:, :, l - wp:])[None])
        mem_p.append(mkv.reshape(1, ml, 2, MEM_HEADS, MEM_HD))
        sproj = _inproj(xs, norm_mix[li], w_in_t, w_tail, li)
        s_ret_out, ret_s_all = _ret_decode(sproj, cos128s, sin128s, state_ret, li, ret_s_all)
        s_ssd_out, s_conv, ssm_s_all = _ssd_decode(sproj, conv_t, state_ssm, li, ssm_s_all, *ssd_w)
        s_qb, s_cache, s_wrow, cache_s, s_wcol, _, _ = _nsa_prep(sproj, cos64s, sin64s, li, depth, cache_s, *nsa_n)
        s_nsa, win_s_all = _nsa_decode(
            page_table, s_qb.reshape(db, NSA_HEADS, LANES), cache_t, s_cache[:, None, :], win_t,
            s_wrow[:, None, :], s_wcol, sproj[:, None, C_GATE:C_GATE + LANES],
            sproj[:, None, C_NZ:C_NZ + NSA_WIDTH], w1x, pex, w2x, nsa_kc_norm[li], past_len, li, win_s_all)
        x1, s_q = _dec_tail_a(xs, s_ret_out, s_ssd_out, s_nsa[:, 0, :], w_out_b, norm_cross[li], wq_b,
                              mem_q_norm[li])
        xs = _dec_xattn(s_q, mem_rows, x1, wo_b, li)
        conv_s.append(jnp.transpose(s_conv, (1, 0, 2)))
    stack = jnp.stack
    return (xp[None], xs[:, None, :], stack(ret_p), ret_s_all, stack(ssm_p), ssm_s_all, stack(conv_p), stack(conv_s),
            untok(cache_p)[:, None], untok(cache_s)[:, :, None], stack(win_p), untok(win_s_all), stack(mem_p))
```
